```python
import jax, jax.numpy as jnp
from jax import lax
import numpy as np


D_MODEL = 1024
BATCH = 2
SEQ = 16384
DEPTH = 2

N_Q_HEADS = 8
N_KV_HEADS = 2
GQA_GROUP = N_Q_HEADS // N_KV_HEADS
HEAD_DIM = 64
WINDOW = 128
WIN_BLOCK = 128
ROPE_THETA = 10000.0
MASK_VALUE = -1e30
HGRN_HEADS = 4
HGRN_KEY_DIM = 128
HGRN_VAL_DIM = 128
HGRN_CHUNK = 16
LOGF_MIN = -4.0
LB_FLOOR = 1e-30
N_GROUPS = 4
EXPERTS_PER_GROUP = 8
N_EXPERTS = N_GROUPS * EXPERTS_PER_GROUP
TOP_K = 2
D_EXPERT = 512
MOE_BLOCK = 128
NORM_EPS = 1e-6

ATTN_Q_W = N_Q_HEADS * HEAD_DIM
ATTN_KV_W = N_KV_HEADS * HEAD_DIM
HGRN_F_W = HGRN_HEADS * HGRN_KEY_DIM
HGRN_V_W = HGRN_HEADS * HGRN_VAL_DIM
SPLIT_SIZES = (ATTN_Q_W, ATTN_KV_W, ATTN_KV_W,
               HGRN_F_W, HGRN_F_W, HGRN_F_W, HGRN_V_W, HGRN_V_W,
               D_MODEL, D_MODEL)
IN_WIDTH = sum(SPLIT_SIZES)

kernel_name = "hybrid_swa_hgrn2_hmoe_encoder"


def _rmsnorm(x, g):
    x32 = x.astype(jnp.float32)
    y = x32 * lax.rsqrt(jnp.mean(x32 * x32, axis=-1, keepdims=True) + NORM_EPS)
    return (y * g.astype(jnp.float32)).astype(x.dtype)


def _rope(t, pos):
    half = t.shape[-1] // 2
    inv = ROPE_THETA ** (-jnp.arange(half, dtype=jnp.float32) / half)
    ang = pos.astype(jnp.float32)[:, None] * inv[None, :]
    cos = jnp.cos(ang)[None, :, None, :]
    sin = jnp.sin(ang)[None, :, None, :]
    t1, t2 = t[..., :half], t[..., half:]
    return jnp.concatenate([t1 * cos - t2 * sin, t2 * cos + t1 * sin], axis=-1)


def _window_attention(q, k, v, sink):
    B, L = q.shape[0], q.shape[1]
    nb = L // WIN_BLOCK
    q = q.reshape(B, nb, WIN_BLOCK, N_KV_HEADS, GQA_GROUP, HEAD_DIM)

    def band(t):
        tp = jnp.pad(t, ((0, 0), (WIN_BLOCK, WIN_BLOCK), (0, 0), (0, 0)))
        tb = tp.reshape(B, nb + 2, WIN_BLOCK, N_KV_HEADS, HEAD_DIM)
        return jnp.concatenate([tb[:, :-2], tb[:, 1:-1], tb[:, 2:]], axis=2)

    kw, vw = band(k), band(v)
    s = jnp.einsum('bnqkgd,bnskd->bnkgqs', q, kw) * (HEAD_DIM ** -0.5)
    blk = jnp.arange(nb)
    qpos = blk[:, None] * WIN_BLOCK + jnp.arange(WIN_BLOCK)[None, :]
    kpos = (blk[:, None] - 1) * WIN_BLOCK + jnp.arange(3 * WIN_BLOCK)[None, :]
    valid = ((jnp.abs(qpos[:, :, None] - kpos[:, None, :]) <= WINDOW)
             & (kpos >= 0)[:, None, :] & (kpos < L)[:, None, :])
    s = jnp.where(valid[None, :, None, None], s, MASK_VALUE)
    sink_b = sink.reshape(N_KV_HEADS, GQA_GROUP)[None, None, :, :, None, None]
    m = jnp.maximum(jnp.max(s, axis=-1, keepdims=True), sink_b)
    p = jnp.exp(s - m)
    denom = jnp.sum(p, axis=-1, keepdims=True) + jnp.exp(sink_b - m)
    o = jnp.einsum('bnkgqs,bnskd->bnqkgd', p / denom, vw)
    return o.reshape(B, L, ATTN_Q_W)


def _gla_chunkwise(q, k, v, logf):
    B, H, L, DK = q.shape
    DV = v.shape[-1]
    n = L // HGRN_CHUNK
    q, k, logf = [t.reshape(B, H, n, HGRN_CHUNK, DK) for t in (q, k, logf)]
    v = v.reshape(B, H, n, HGRN_CHUNK, DV)
    b = jnp.cumsum(logf, axis=3)
    b_last = b[:, :, :, -1:, :]
    qe = q * jnp.exp(b)
    a = jnp.einsum('bhncd,bhnsd->bhncs', qe, k * jnp.exp(-b))
    tril = jnp.tril(jnp.ones((HGRN_CHUNK, HGRN_CHUNK), dtype=bool))
    a = jnp.where(tril, a, 0.0)
    o = jnp.einsum('bhncs,bhnse->bhnce', a, v)
    kv = jnp.einsum('bhncd,bhnce->bhnde', k * jnp.exp(b_last - b), v)
    decay = jnp.exp(b_last[:, :, :, 0, :])

    def step(S, inp):
        d, u = inp
        return d[..., None] * S + u, S

    _, s_prev = lax.scan(step, jnp.zeros((B, H, DK, DV), jnp.float32),
                         (jnp.moveaxis(decay, 2, 0), jnp.moveaxis(kv, 2, 0)))
    o = o + jnp.einsum('bhncd,nbhde->bhnce', qe, s_prev)
    return o.reshape(B, H, L, DV)


def _hgrn2(qb, zf, zb, ib, ogb, lb, norm_g):
    B, L = qb.shape[0], qb.shape[1]

    def heads(t):
        return t.astype(jnp.float32).reshape(B, L, HGRN_HEADS, -1).transpose(0, 2, 1, 3)

    q = jax.nn.silu(heads(qb))
    v = heads(ib)

    def direction(z, lbd, reverse):
        lbd = lbd.astype(jnp.float32).reshape(1, HGRN_HEADS, 1, HGRN_KEY_DIM)
        logf = jnp.logaddexp(jnp.log(jnp.maximum(lbd, LB_FLOOR)),
                             jnp.log1p(-lbd) + jax.nn.log_sigmoid(heads(z)))
        logf = jnp.maximum(logf, LOGF_MIN)
        k = -jnp.expm1(logf)
        if reverse:
            fl = lambda t: jnp.flip(t, axis=2)
            return fl(_gla_chunkwise(fl(q), fl(k), fl(v), fl(logf)))
        return _gla_chunkwise(q, k, v, logf)

    o = direction(zf, lb[0], False) + direction(zb, lb[1], True)
    o = o.transpose(0, 2, 1, 3).reshape(B, L, HGRN_V_W)
    o = o * lax.rsqrt(jnp.mean(o * o, axis=-1, keepdims=True) + NORM_EPS) * norm_g.astype(jnp.float32)
    return o * jax.nn.silu(ogb.astype(jnp.float32))


def _mixer(h, w_in, sink, lb, hgrn_g, w_up_attn, w_up_hgrn, w_out):
    B, L, _ = h.shape
    proj = h @ w_in
    points = np.cumsum(SPLIT_SIZES)[:-1].tolist()
    qa, ka, va, qb, zf, zb, ib, ogb, ga, gb = jnp.split(proj, points, axis=-1)
    pos = jnp.arange(L)
    qa = _rope(qa.reshape(B, L, N_Q_HEADS, HEAD_DIM).astype(jnp.float32), pos)
    ka = _rope(ka.reshape(B, L, N_KV_HEADS, HEAD_DIM).astype(jnp.float32), pos)
    va = va.reshape(B, L, N_KV_HEADS, HEAD_DIM).astype(jnp.float32)
    y_attn = _window_attention(qa, ka, va, sink.astype(jnp.float32)).astype(h.dtype)
    y_hgrn = _hgrn2(qb, zf, zb, ib, ogb, lb, hgrn_g).astype(h.dtype)
    merged = (jax.nn.sigmoid(ga) * (y_attn @ w_up_attn)
              + jax.nn.sigmoid(gb) * (y_hgrn @ w_up_hgrn))
    return merged @ w_out


def _hier_moe(xf, w_rg, b_rg, w_re, b_re, w_gate, w_up, w_down):
    T, D = xf.shape
    p_g = jax.nn.softmax((xf @ w_rg).astype(jnp.float32) + b_rg.astype(jnp.float32), axis=-1)
    p_top, g_top = lax.top_k(p_g, 1)
    logits_e = jnp.einsum('td,gde->tge', xf, w_re).astype(jnp.float32) + b_re.astype(jnp.float32)
    logits_e = jnp.take_along_axis(logits_e, g_top[:, :, None], axis=1)[:, 0]
    w_top, e_top = lax.top_k(jax.nn.softmax(logits_e, axis=-1), TOP_K)
    w_top = w_top / jnp.sum(w_top, axis=-1, keepdims=True)
    weights = (p_top * w_top).reshape(-1)
    experts = (g_top * EXPERTS_PER_GROUP + e_top).reshape(-1).astype(jnp.int32)
    tokens = jnp.repeat(jnp.arange(T, dtype=jnp.int32), TOP_K)

    n_slots = T * TOP_K
    order = jnp.argsort(experts)
    e_s, t_s, w_s = experts[order], tokens[order], weights[order]
    counts = jnp.bincount(experts, length=N_EXPERTS)
    start = jnp.cumsum(counts) - counts
    padded = (counts + MOE_BLOCK - 1) // MOE_BLOCK * MOE_BLOCK
    pend = jnp.cumsum(padded)
    pstart = pend - padded
    dest = pstart[e_s] + (jnp.arange(n_slots) - start[e_s])
    n_pad = n_slots + N_EXPERTS * MOE_BLOCK
    n_blocks = n_pad // MOE_BLOCK
    slot_token = jnp.full((n_pad,), T, jnp.int32).at[dest].set(t_s)
    slot_weight = jnp.zeros((n_pad,), xf.dtype).at[dest].set(w_s.astype(xf.dtype))
    block_expert = jnp.minimum(
        jnp.searchsorted(pend, jnp.arange(n_blocks) * MOE_BLOCK, side='right'), N_EXPERTS - 1)
    x_pad = jnp.concatenate([xf, jnp.zeros((1, D), xf.dtype)], axis=0)

    def block_fn(args):
        e, idx = args
        hb = x_pad[idx]
        return (jax.nn.silu(hb @ w_gate[e]) * (hb @ w_up[e])) @ w_down[e]

    y = lax.map(block_fn, (block_expert, slot_token.reshape(n_blocks, MOE_BLOCK)))
    y = y.reshape(n_pad, D) * slot_weight[:, None]
    return jax.ops.segment_sum(y, slot_token, num_segments=T + 1)[:T]


def setup_inputs(seed: int = 0) -> dict:
    key = jax.random.key(seed)
    ks = jax.random.split(key, 18)
    f32 = jnp.float32
    nrm = lambda k, shape, scale: jax.random.normal(k, shape, f32) * scale
    return {
        "x": nrm(ks[0], (BATCH, SEQ, D_MODEL), 1.0),
        "attn_norm_g": 1.0 + nrm(ks[1], (DEPTH, D_MODEL), 0.02),
        "w_in": nrm(ks[2], (DEPTH, D_MODEL, IN_WIDTH), D_MODEL ** -0.5),
        "sink_logits": nrm(ks[3], (DEPTH, N_Q_HEADS), 0.5),
        "hgrn_lb_table": nrm(ks[4], (DEPTH, 2, HGRN_F_W), 0.5),
        "hgrn_norm_g": 1.0 + nrm(ks[5], (DEPTH, HGRN_V_W), 0.02),
        "w_up_attn": nrm(ks[6], (DEPTH, ATTN_Q_W, D_MODEL), ATTN_Q_W ** -0.5),
        "w_up_hgrn": nrm(ks[7], (DEPTH, HGRN_V_W, D_MODEL), HGRN_V_W ** -0.5),
        "w_out": nrm(ks[8], (DEPTH, D_MODEL, D_MODEL), D_MODEL ** -0.5),
        "ffn_norm_g": 1.0 + nrm(ks[9], (DEPTH, D_MODEL), 0.02),
        "w_router_group": nrm(ks[10], (DEPTH, D_MODEL, N_GROUPS), D_MODEL ** -0.5),
        "b_router_group": nrm(ks[11], (DEPTH, N_GROUPS), 0.01),
        "w_router_expert": nrm(ks[12], (DEPTH, N_GROUPS, D_MODEL, EXPERTS_PER_GROUP), D_MODEL ** -0.5),
        "b_router_expert": nrm(ks[13], (DEPTH, N_GROUPS, EXPERTS_PER_GROUP), 0.01),
        "w_expert_gate": nrm(ks[14], (DEPTH, N_EXPERTS, D_MODEL, D_EXPERT), D_MODEL ** -0.5),
        "w_expert_up": nrm(ks[15], (DEPTH, N_EXPERTS, D_MODEL, D_EXPERT), D_MODEL ** -0.5),
        "w_expert_down": nrm(ks[16], (DEPTH, N_EXPERTS, D_EXPERT, D_MODEL), D_EXPERT ** -0.5),
        "final_norm_g": 1.0 + nrm(ks[17], (D_MODEL,), 0.02),
    }


def reference(x, attn_norm_g, w_in, sink_logits, hgrn_lb_table, hgrn_norm_g, w_up_attn,
              w_up_hgrn, w_out, ffn_norm_g, w_router_group, b_router_group, w_router_expert,
              b_router_expert, w_expert_gate, w_expert_up, w_expert_down, final_norm_g):
    B, L, D = x.shape
    sm = jax.nn.softmax(hgrn_lb_table.astype(jnp.float32), axis=0)
    lower_bounds = jnp.cumsum(sm, axis=0) - sm[:1]
    for l in range(DEPTH):
        h = _rmsnorm(x, attn_norm_g[l])
        x = x + _mixer(h, w_in[l], sink_logits[l], lower_bounds[l], hgrn_norm_g[l],
                       w_up_attn[l], w_up_hgrn[l], w_out[l])
        h = _rmsnorm(x, ffn_norm_g[l])
        y = _hier_moe(h.reshape(B * L, D), w_router_group[l], b_router_group[l],
                      w_router_expert[l], b_router_expert[l], w_expert_gate[l],
                      w_expert_up[l], w_expert_down[l])
        x = x + y.reshape(B, L, D).astype(x.dtype)
    return _rmsnorm(x, final_norm_g)
```

```python
import functools

import numpy as np
import jax
import jax.numpy as jnp
from jax import lax
from jax.experimental import pallas as pl
from jax.experimental.pallas import tpu as pltpu

F32 = jnp.float32
BF16 = jnp.bfloat16
U32 = jnp.uint32
I32 = jnp.int32

D_MODEL = 1024
N_Q_HEADS = 8
N_KV_HEADS = 2
HEAD_DIM = 64
WINDOW = 128
ROPE_THETA = 10000.0
MASK_VALUE = -1e30
HGRN_HEADS = 4
HGRN_DIM = 128
LOGF_MIN = -4.0
F_MIN = float(np.exp(LOGF_MIN))
LB_FLOOR = 1e-30
N_GROUPS = 4
EXPERTS_PER_GROUP = 8
N_EXPERTS = N_GROUPS * EXPERTS_PER_GROUP
D_EXPERT = 512
NORM_EPS = 1e-6

ATTN_Q_W = N_Q_HEADS * HEAD_DIM
ATTN_KV_W = N_KV_HEADS * HEAD_DIM
HGRN_W = HGRN_HEADS * HGRN_DIM
OFF_QA = 0
OFF_KA = OFF_QA + ATTN_Q_W
OFF_VA = OFF_KA + ATTN_KV_W
OFF_QB = OFF_VA + ATTN_KV_W
OFF_ZF = OFF_QB + HGRN_W
OFF_ZB = OFF_ZF + HGRN_W
OFF_IB = OFF_ZB + HGRN_W
OFF_OG = OFF_IB + HGRN_W
OFF_GA = OFF_OG + HGRN_W
OFF_GB = OFF_GA + D_MODEL

LANES = 128
VMEM_LIMIT = 56 * 1024 * 1024

TM_PROJ = 512
TQ_ATTN = 512
TH_HGRN = 512
TM_POST = 512
MOE_BLK = 256
TD_DISPATCH = 256
TC_COMBINE = 256


def _sigmoid(z):
    return 1.0 / (1.0 + jnp.exp(-z))


def _rms(x):
    return x * lax.rsqrt(jnp.mean(x * x, axis=-1, keepdims=True) + NORM_EPS)


def _nt(a, b):
    return lax.dot_general(a, b, (((1,), (1,)), ((), ())), preferred_element_type=F32)


def _pack_pair(a, b):
    ua = lax.bitcast_convert_type(a.astype(BF16).astype(F32), U32)
    ub = lax.bitcast_convert_type(b.astype(BF16).astype(F32), U32)
    return ua | (ub >> 16)


def _unpack_pair(u):
    hi = lax.bitcast_convert_type(u & jnp.uint32(0xFFFF0000), F32)
    lo = lax.bitcast_convert_type(u << 16, F32)
    return hi, lo


def _inproj_body(x_ref, g_ref, w_ref, cos_ref, s1_ref, s2_ref,
                 qa_ref, ka_ref, va_ref, qb_ref, sf_ref, sb_ref, ib_ref, og_ref, ga_ref, gb_ref):
    h = (_rms(x_ref[0]) * g_ref[...]).astype(BF16)

    def proj(off, width):
        return jnp.dot(h, w_ref[:, off:off + width], preferred_element_type=F32)

    cos, s1, s2 = cos_ref[...], s1_ref[...], s2_ref[...]

    def rope(t):
        return t * cos + pltpu.roll(t, 96, 1) * s1 + pltpu.roll(t, 32, 1) * s2

    q = proj(OFF_QA, ATTN_Q_W)
    for c in range(ATTN_Q_W // LANES):
        sl = slice(c * LANES, (c + 1) * LANES)
        qa_ref[0, :, sl] = (rope(q[:, sl]) * (HEAD_DIM ** -0.5)).astype(BF16)
    k = rope(proj(OFF_KA, ATTN_KV_W))
    ka_ref[0, :, :LANES] = k.astype(BF16)
    ka_ref[0, :, LANES:] = pltpu.roll(k, 64, 1).astype(BF16)
    v = proj(OFF_VA, ATTN_KV_W)
    va_ref[0, :, :LANES] = v.astype(BF16)
    va_ref[0, :, LANES:] = pltpu.roll(v, 64, 1).astype(BF16)
    t = proj(OFF_QB, HGRN_W)
    qb_ref[0] = (t * _sigmoid(t)).astype(BF16)
    sf_ref[0] = _sigmoid(proj(OFF_ZF, HGRN_W)).astype(BF16)
    sb_ref[0] = _sigmoid(proj(OFF_ZB, HGRN_W)).astype(BF16)
    ib_ref[0] = proj(OFF_IB, HGRN_W).astype(BF16)
    t = proj(OFF_OG, HGRN_W)
    og_ref[0] = (t * _sigmoid(t)).astype(BF16)
    ga_ref[0] = _sigmoid(proj(OFF_GA, D_MODEL)).astype(BF16)
    gb_ref[0] = _sigmoid(proj(OFF_GB, D_MODEL)).astype(BF16)


def _inproj(x, g, w_bf16, cos, s1, s2):
    B, L, D = x.shape
    tm = min(TM_PROJ, L)
    widths = (ATTN_Q_W, 2 * ATTN_KV_W, 2 * ATTN_KV_W, HGRN_W, HGRN_W, HGRN_W, HGRN_W, HGRN_W,
              D_MODEL, D_MODEL)
    tab = pl.BlockSpec((tm, LANES), lambda b, i: (i, 0))
    return pl.pallas_call(
        _inproj_body,
        grid=(B, L // tm),
        in_specs=[
            pl.BlockSpec((1, tm, D), lambda b, i: (b, i, 0)),
            pl.BlockSpec((1, D), lambda b, i: (0, 0)),
            pl.BlockSpec(w_bf16.shape, lambda b, i: (0, 0)),
            tab, tab, tab,
        ],
        out_specs=[pl.BlockSpec((1, tm, w), lambda b, i: (b, i, 0)) for w in widths],
        out_shape=[jax.ShapeDtypeStruct((B, L, w), BF16) for w in widths],
        compiler_params=pltpu.CompilerParams(
            dimension_semantics=("arbitrary", "arbitrary"), vmem_limit_bytes=VMEM_LIMIT),
        name="inproj",
    )(x, g, w_bf16, cos, s1, s2)


def _attn_body(sink_ref, q_ref, kp_ref, km_ref, kn_ref, vp_ref, vm_ref, vn_ref, o_ref, *, tq, seq):
    i = pl.program_id(1)
    kw = jnp.concatenate([kp_ref[0], km_ref[0], kn_ref[0]], axis=0)
    vw = jnp.concatenate([vp_ref[0], vm_ref[0], vn_ref[0]], axis=0)
    lo = lax.broadcasted_iota(I32, (1, LANES), 1) < HEAD_DIM
    zero = jnp.zeros((), BF16)

    def variants(t):
        a, b = t[:, :LANES], t[:, LANES:]
        return ((jnp.where(lo, a, zero), jnp.where(lo, zero, b)),
                (jnp.where(lo, b, zero), jnp.where(lo, zero, a)))

    kvar, vvar = variants(kw), variants(vw)
    row = lax.broadcasted_iota(I32, (WINDOW, 3 * WINDOW), 0)
    col = lax.broadcasted_iota(I32, (WINDOW, 3 * WINDOW), 1)
    band = (col >= row) & (col <= row + 2 * WINDOW)
    for j in range(tq // WINDOW):
        kpos = col + (i * tq + (j - 1) * WINDOW)
        valid = band & (kpos >= 0) & (kpos < seq)
        rows = slice(j * WINDOW, (j + 1) * WINDOW)
        win = slice(j * WINDOW, (j + 3) * WINDOW)
        for c in range(ATTN_Q_W // LANES):
            g = (2 * c) // (N_Q_HEADS // N_KV_HEADS)
            qc = q_ref[0, rows, c * LANES:(c + 1) * LANES]
            ps, ls = [], []
            for hh in range(2):
                s = _nt(qc, kvar[g][hh][win])
                s = jnp.where(valid, s, MASK_VALUE)
                sk = sink_ref[2 * c + hh]
                m = jnp.maximum(jnp.max(s, axis=1, keepdims=True), sk)
                p = jnp.exp(s - m)
                ls.append(jnp.sum(p, axis=1, keepdims=True) + jnp.exp(sk - m))
                ps.append(p.astype(BF16))
            pv = jnp.dot(jnp.concatenate(ps, axis=1),
                         jnp.concatenate([vvar[g][0][win], vvar[g][1][win]], axis=0),
                         preferred_element_type=F32)
            inv = jnp.where(lo, 1.0 / ls[0], 1.0 / ls[1])
            o_ref[0, rows, c * LANES:(c + 1) * LANES] = (pv * inv).astype(BF16)


def _attention(qa, ka, va, sink):
    B, L, _ = qa.shape
    tq = min(TQ_ATTN, L)
    r = tq // WINDOW
    nb = L // WINDOW
    kvw = 2 * ATTN_KV_W
    main = pl.BlockSpec((1, tq, kvw), lambda b, i: (b, i, 0))
    prev = pl.BlockSpec((1, WINDOW, kvw), lambda b, i: (b, jnp.maximum(i * r - 1, 0), 0))
    nxt = pl.BlockSpec((1, WINDOW, kvw), lambda b, i: (b, jnp.minimum(i * r + r, nb - 1), 0))
    return pl.pallas_call(
        functools.partial(_attn_body, tq=tq, seq=L),
        grid=(B, L // tq),
        in_specs=[
            pl.BlockSpec(memory_space=pltpu.SMEM),
            pl.BlockSpec((1, tq, ATTN_Q_W), lambda b, i: (b, i, 0)),
            prev, main, nxt, prev, main, nxt,
        ],
        out_specs=pl.BlockSpec((1, tq, ATTN_Q_W), lambda b, i: (b, i, 0)),
        out_shape=jax.ShapeDtypeStruct((B, L, ATTN_Q_W), BF16),
        compiler_params=pltpu.CompilerParams(
            dimension_semantics=("arbitrary", "arbitrary"), vmem_limit_bytes=VMEM_LIMIT),
        name="win_attn",
    )(sink, qa, ka, ka, ka, va, va, va)


HB = 128
LEAF = 16


def _hgrn_masks(reverse):
    r = lax.broadcasted_iota(I32, (HB, HB), 0)
    c = lax.broadcasted_iota(I32, (HB, HB), 1)
    if reverse:
        r, c = c, r
    tri = jnp.where(c <= r, 1.0, 0.0).astype(BF16)
    same = lambda s: (r >> s) == (c >> s)
    cross = lambda bit: ((r & bit) != 0) & ((c & bit) == 0)
    lev = jnp.where(same(4) & (c <= r), 0,
                    jnp.where(same(5) & cross(16), 1,
                              jnp.where(same(6) & cross(32), 2, 3)))
    return tri, lev


def _hgrn_unit(q, s, v, lb, lbc, st_ref, tri, lev, reverse):
    f = lbc + (1.0 - lb) * s
    logf = jnp.maximum(jnp.log(f), LOGF_MIN)
    kk = 1.0 - jnp.maximum(f, F_MIN)
    hi = logf.astype(BF16)
    r1 = logf - hi.astype(F32)
    mid = r1.astype(BF16)
    low = (r1 - mid.astype(F32)).astype(BF16)
    dot = functools.partial(jnp.dot, preferred_element_type=F32)
    cum = dot(tri, hi) + dot(tri, mid) + dot(tri, low)

    def ref_row(idx):
        return cum[idx:idx + 1]

    total = ref_row(0) if reverse else ref_row(HB - 1)
    q_blk = q * jnp.exp(cum)
    k_blk = kk * jnp.exp(total - cum)

    ql, kl = [], []
    for c in range(HB // LEAF):
        rows = slice(c * LEAF, (c + 1) * LEAF)
        edge = (c + 1) * LEAF if reverse else c * LEAF - 1
        loc = cum[rows] - ref_row(edge) if 0 <= edge < HB else cum[rows]
        ql.append(q[rows] * jnp.exp(loc))
        kl.append(kk[rows] * jnp.exp(-loc))
    pairs = [(jnp.concatenate(ql, axis=0), jnp.concatenate(kl, axis=0))]
    for w in (16, 32, 64):
        qp, kp = [], []
        zeros = jnp.zeros((w, HB), F32)
        for s0 in range(0, HB, 2 * w):
            first, second = slice(s0, s0 + w), slice(s0 + w, s0 + 2 * w)
            if reverse:
                ref = ref_row(s0 + w)
                qp += [q[first] * jnp.exp(cum[first] - ref), zeros]
                kp += [zeros, kk[second] * jnp.exp(ref - cum[second])]
            else:
                ref = ref_row(s0 + w - 1)
                qp += [zeros, q[second] * jnp.exp(cum[second] - ref)]
                kp += [kk[first] * jnp.exp(ref - cum[first]), zeros]
        pairs.append((jnp.concatenate(qp, axis=0), jnp.concatenate(kp, axis=0)))
    a0, a1, a2, a3 = [_nt(qq.astype(BF16), kq.astype(BF16)) for qq, kq in pairs]
    a = jnp.where(lev == 0, a0, jnp.where(lev == 1, a1, jnp.where(lev == 2, a2, a3)))
    st = st_ref[...]
    o = dot(a.astype(BF16), v) + _nt(q_blk.astype(BF16), st.astype(BF16))
    st_ref[...] = st * jnp.exp(total) + lax.dot_general(
        v, k_blk.astype(BF16), (((0,), (0,)), ((), ())), preferred_element_type=F32)
    return o


def _hgrn_body(lb_ref, qf_ref, sf_ref, vf_ref, qr_ref, sr_ref, vr_ref, of_ref, or_ref,
               stf_ref, str_ref, *, th):
    @pl.when(pl.program_id(2) == 0)
    def _():
        stf_ref[...] = jnp.zeros_like(stf_ref)
        str_ref[...] = jnp.zeros_like(str_ref)

    lb = lb_ref[...]
    lbc = jnp.maximum(lb, LB_FLOOR)
    tri_f, lev_f = _hgrn_masks(False)
    tri_r, lev_r = _hgrn_masks(True)
    nblk = th // HB

    def body(t, carry):
        rf = pl.multiple_of(t * HB, HB)
        rr = pl.multiple_of((nblk - 1 - t) * HB, HB)
        o = _hgrn_unit(qf_ref[0, pl.ds(rf, HB), :].astype(F32), sf_ref[0, pl.ds(rf, HB), :].astype(F32),
                       vf_ref[0, pl.ds(rf, HB), :], lb[0:1], lbc[0:1], stf_ref, tri_f, lev_f, False)
        of_ref[0, pl.ds(rf, HB), :] = o.astype(BF16)
        o = _hgrn_unit(qr_ref[0, pl.ds(rr, HB), :].astype(F32), sr_ref[0, pl.ds(rr, HB), :].astype(F32),
                       vr_ref[0, pl.ds(rr, HB), :], lb[1:2], lbc[1:2], str_ref, tri_r, lev_r, True)
        or_ref[0, pl.ds(rr, HB), :] = o.astype(BF16)
        return carry

    lax.fori_loop(0, nblk, body, 0)


def _hgrn(qb, sf, sb, ib, lb):
    B, L, _ = qb.shape
    th = min(TH_HGRN, L)
    n = L // th
    fwd = pl.BlockSpec((1, th, HGRN_DIM), lambda b, h, i: (b, i, h))
    rev = pl.BlockSpec((1, th, HGRN_DIM), lambda b, h, i: (b, n - 1 - i, h))
    return pl.pallas_call(
        functools.partial(_hgrn_body, th=th),
        grid=(B, HGRN_HEADS, n),
        in_specs=[pl.BlockSpec((2, HGRN_DIM), lambda b, h, i: (0, h)), fwd, fwd, fwd, rev, rev, rev],
        out_specs=[fwd, rev],
        out_shape=[jax.ShapeDtypeStruct((B, L, HGRN_W), BF16)] * 2,
        scratch_shapes=[pltpu.VMEM((HGRN_DIM, HGRN_DIM), F32)] * 2,
        compiler_params=pltpu.CompilerParams(
            dimension_semantics=("arbitrary", "arbitrary", "arbitrary"), vmem_limit_bytes=VMEM_LIMIT),
        name="hgrn2",
    )(lb, qb, sf, ib, qb, sb, ib)


def _post_body(x_ref, ya_ref, of_ref, or_ref, og_ref, ga_ref, gb_ref, wua_ref, wuh_ref, wo_ref,
               hg_ref, fg_ref, rwh_ref, rwl_ref, rb_ref, ltri_ref,
               xn_ref, hp_ref, rt_ref, cnt_ref):
    @pl.when(pl.program_id(0) == 0)
    def _():
        cnt_ref[...] = jnp.zeros_like(cnt_ref)

    dot = functools.partial(jnp.dot, preferred_element_type=F32)
    o = of_ref[...].astype(F32) + or_ref[...].astype(F32)
    yh = (_rms(o) * hg_ref[...] * og_ref[...].astype(F32)).astype(BF16)
    merged = (ga_ref[...].astype(F32) * dot(ya_ref[...], wua_ref[...])
              + gb_ref[...].astype(F32) * dot(yh, wuh_ref[...]))
    xn = x_ref[...] + dot(merged.astype(BF16), wo_ref[...])
    xn_ref[...] = xn
    h2 = _rms(xn) * fg_ref[...]
    half = D_MODEL // 2
    hp_ref[...] = _pack_pair(h2[:, :half], h2[:, half:])

    hi = h2.astype(BF16)
    lo = (h2 - hi.astype(F32)).astype(BF16)
    logits = dot(hi, rwh_ref[...]) + dot(lo, rwh_ref[...]) + dot(hi, rwl_ref[...]) + rb_ref[...]
    lane = lax.broadcasted_iota(I32, logits.shape, 1)
    lanef = lane.astype(F32)
    big = jnp.float32(1e9)
    ninf = jnp.float32(-jnp.inf)
    red = dict(axis=1, keepdims=True)
    gmask = lane < N_GROUPS
    mg = jnp.max(jnp.where(gmask, logits, ninf), **red)
    p_top = 1.0 / jnp.sum(jnp.where(gmask, jnp.exp(logits - mg), 0.0), **red)
    gi = jnp.min(jnp.where(gmask & (logits == mg), lanef, big), **red)
    emask = ((lane >= N_GROUPS) & (lane < N_GROUPS + N_EXPERTS)
             & (((lane - N_GROUPS) >> 3).astype(F32) == gi))
    le = jnp.where(emask, logits, ninf)
    m1 = jnp.max(le, **red)
    i1 = jnp.min(jnp.where(le == m1, lanef, big), **red)
    le2 = jnp.where(lanef == i1, ninf, le)
    m2 = jnp.max(le2, **red)
    i2 = jnp.min(jnp.where(le2 == m2, lanef, big), **red)
    r = jnp.exp(m2 - m1)
    w1 = 1.0 / (1.0 + r)
    w2 = r * w1
    e1 = i1 - N_GROUPS
    e2 = i2 - N_GROUPS
    is1, is2 = lanef == e1, lanef == e2
    onehot = jnp.where(is1 | is2, 1.0, 0.0)
    pref = dot(ltri_ref[...], onehot.astype(BF16)) + cnt_ref[...]
    rank1 = jnp.sum(jnp.where(is1, pref, 0.0), **red)
    rank2 = jnp.sum(jnp.where(is2, pref, 0.0), **red)
    cnt_ref[...] += jnp.sum(onehot, axis=0, keepdims=True)
    vals = (e1, e2, p_top * w1, p_top * w2, rank1, rank2)
    rt = jnp.zeros_like(logits)
    for idx, val in enumerate(vals):
        rt = jnp.where(lane == idx, val, rt)
    rt_ref[...] = rt


def _post(x, ya, of, orv, og, ga, gb, wua, wuh, wo, hg, fg, rwh, rwl, rb, ltri):
    T, D = x.shape
    tm = ltri.shape[0]
    row = lambda w: pl.BlockSpec((tm, w), lambda i: (i, 0))
    full = lambda a: pl.BlockSpec(a.shape, lambda i: (0,) * a.ndim)
    return pl.pallas_call(
        _post_body,
        grid=(T // tm,),
        in_specs=[row(D), row(ATTN_Q_W), row(HGRN_W), row(HGRN_W), row(HGRN_W), row(D), row(D),
                  full(wua), full(wuh), full(wo), full(hg), full(fg), full(rwh), full(rwl), full(rb),
                  full(ltri)],
        out_specs=[row(D), row(D // 2), row(LANES), pl.BlockSpec((1, LANES), lambda i: (0, 0))],
        out_shape=[jax.ShapeDtypeStruct((T, D), F32), jax.ShapeDtypeStruct((T, D // 2), U32),
                   jax.ShapeDtypeStruct((T, LANES), F32), jax.ShapeDtypeStruct((1, LANES), F32)],
        compiler_params=pltpu.CompilerParams(
            dimension_semantics=("arbitrary",), vmem_limit_bytes=VMEM_LIMIT),
        name="post_mixer_router",
    )(x, ya, of, orv, og, ga, gb, wua, wuh, wo, hg, fg, rwh, rwl, rb, ltri)


def _row_copy(src_ref, src_row, dst_ref, dst_row, sem):
    return pltpu.make_async_copy(src_ref.at[pl.ds(src_row, 1)], dst_ref.at[pl.ds(dst_row, 1)], sem)


def _dispatch_body(dest_ref, h_ref, xs_in_ref, xs_ref, sem, *, td):
    del xs_in_ref

    def issue(r, c):
        _row_copy(h_ref, r, xs_ref, dest_ref[0, 0, 2 * r], sem).start()
        _row_copy(h_ref, r, xs_ref, dest_ref[0, 0, 2 * r + 1], sem).start()
        return c

    lax.fori_loop(0, td, issue, 0)

    def drain(r, c):
        _row_copy(h_ref, 0, xs_ref, 0, sem).wait()
        return c

    lax.fori_loop(0, 2 * td, drain, 0)


def _dispatch(dest, hp, xs_init):
    T, W = hp.shape
    td = dest.shape[2] // 2
    return pl.pallas_call(
        functools.partial(_dispatch_body, td=td),
        grid=(T // td,),
        in_specs=[pl.BlockSpec((1, 1, 2 * td), lambda i: (i, 0, 0), memory_space=pltpu.SMEM),
                  pl.BlockSpec((td, W), lambda i: (i, 0)),
                  pl.BlockSpec(memory_space=pl.ANY)],
        out_specs=pl.BlockSpec(memory_space=pl.ANY),
        out_shape=jax.ShapeDtypeStruct(xs_init.shape, U32),
        scratch_shapes=[pltpu.SemaphoreType.DMA(())],
        input_output_aliases={2: 0},
        compiler_params=pltpu.CompilerParams(
            dimension_semantics=("arbitrary",), vmem_limit_bytes=VMEM_LIMIT),
        name="moe_dispatch",
    )(dest, hp, xs_init)


def _expert_body(be_ref, nused_ref, xs_ref, wg_ref, wu_ref, wd_ref, ys_ref):
    i = pl.program_id(0)

    @pl.when(i < nused_ref[0])
    def _():
        dot = functools.partial(jnp.dot, preferred_element_type=F32)
        ha, hb = _unpack_pair(xs_ref[...])
        h = jnp.concatenate([ha, hb], axis=1).astype(BF16)
        g = dot(h, wg_ref[0])
        u = dot(h, wu_ref[0])
        y = dot((g * _sigmoid(g) * u).astype(BF16), wd_ref[0])
        half = D_MODEL // 2
        ys_ref[...] = _pack_pair(y[:, :half], y[:, half:])

    @pl.when(i >= nused_ref[0])
    def _():
        ys_ref[...] = jnp.zeros_like(ys_ref)


def _experts(block_expert, nused, xs, wg, wu, wd):
    n_pad, W = xs.shape
    blk = pl.BlockSpec((MOE_BLK, W), lambda i, be, nu: (i, 0))
    wspec = lambda a: pl.BlockSpec((1,) + a.shape[1:], lambda i, be, nu: (be[i], 0, 0))
    return pl.pallas_call(
        _expert_body,
        grid_spec=pltpu.PrefetchScalarGridSpec(
            num_scalar_prefetch=2,
            grid=(n_pad // MOE_BLK,),
            in_specs=[blk, wspec(wg), wspec(wu), wspec(wd)],
            out_specs=blk,
        ),
        out_shape=jax.ShapeDtypeStruct((n_pad, W), U32),
        compiler_params=pltpu.CompilerParams(
            dimension_semantics=("arbitrary",), vmem_limit_bytes=VMEM_LIMIT),
        name="moe_experts",
    )(block_expert, nused, xs, wg, wu, wd)


def _combine_body(dest_ref, x_ref, rt_ref, fg_ref, ys_ref, o_ref, buf_ref, sem, *, tc, final_norm):
    def issue(r, c):
        _row_copy(ys_ref, dest_ref[0, 0, 2 * r], buf_ref.at[0], r, sem).start()
        _row_copy(ys_ref, dest_ref[0, 0, 2 * r + 1], buf_ref.at[1], r, sem).start()
        return c

    lax.fori_loop(0, tc, issue, 0)

    def drain(r, c):
        _row_copy(ys_ref, 0, buf_ref.at[0], 0, sem).wait()
        return c

    lax.fori_loop(0, 2 * tc, drain, 0)
    rt = rt_ref[...]
    y1 = jnp.concatenate(_unpack_pair(buf_ref[0]), axis=1)
    y2 = jnp.concatenate(_unpack_pair(buf_ref[1]), axis=1)
    out = x_ref[...] + (rt[:, 2:3] * y1 + rt[:, 3:4] * y2)
    if final_norm:
        out = _rms(out) * fg_ref[...]
    o_ref[...] = out


def _combine(dest, x, rt, fg, ys, final_norm):
    T, D = x.shape
    tc = dest.shape[2] // 2
    return pl.pallas_call(
        functools.partial(_combine_body, tc=tc, final_norm=final_norm),
        grid=(T // tc,),
        in_specs=[pl.BlockSpec((1, 1, 2 * tc), lambda i: (i, 0, 0), memory_space=pltpu.SMEM),
                  pl.BlockSpec((tc, D), lambda i: (i, 0)),
                  pl.BlockSpec((tc, LANES), lambda i: (i, 0)),
                  pl.BlockSpec((1, D), lambda i: (0, 0)),
                  pl.BlockSpec(memory_space=pl.ANY)],
        out_specs=pl.BlockSpec((tc, D), lambda i: (i, 0)),
        out_shape=jax.ShapeDtypeStruct((T, D), F32),
        scratch_shapes=[pltpu.VMEM((2, tc, D // 2), U32), pltpu.SemaphoreType.DMA(())],
        compiler_params=pltpu.CompilerParams(
            dimension_semantics=("arbitrary",), vmem_limit_bytes=VMEM_LIMIT),
        name="moe_combine",
    )(dest, x, rt, fg, ys)


def _rope_tables(L):
    half = HEAD_DIM // 2
    inv = ROPE_THETA ** (-jnp.arange(half, dtype=F32) / half)
    ang = jnp.arange(L, dtype=F32)[:, None] * inv[None, :]
    lane = np.arange(LANES)
    idx = lane % half
    first = jnp.asarray((lane % HEAD_DIM) < half)
    cos = jnp.cos(ang)[:, idx]
    sin = jnp.sin(ang)[:, idx]
    return cos, jnp.where(first, -sin, 0.0), jnp.where(first, 0.0, sin)


def _moe_layout(rt, cnt, T):
    e = rt[:, 0:2].astype(I32)
    rank = rt[:, 4:6].astype(I32)
    counts = cnt[0, :N_EXPERTS].astype(I32)
    padded = (counts + MOE_BLK - 1) // MOE_BLK * MOE_BLK
    pend = jnp.cumsum(padded)
    pstart = pend - padded
    dest = pstart[e] + rank
    n_pad = 2 * T + N_EXPERTS * MOE_BLK
    starts = jnp.arange(n_pad // MOE_BLK, dtype=I32) * MOE_BLK
    block_expert = jnp.minimum(jnp.sum(starts[:, None] >= pend[None, :], axis=1), N_EXPERTS - 1)
    nused = (pend[-1:] // MOE_BLK).astype(I32)
    return dest, block_expert.astype(I32), nused, n_pad


def kernel(x, attn_norm_g, w_in, sink_logits, hgrn_lb_table, hgrn_norm_g, w_up_attn, w_up_hgrn, w_out,
           ffn_norm_g, w_router_group, b_router_group, w_router_expert, b_router_expert,
           w_expert_gate, w_expert_up, w_expert_down, final_norm_g):
    B, L, D = x.shape
    T = B * L
    depth = w_in.shape[0]
    sm = jax.nn.softmax(hgrn_lb_table.astype(F32), axis=0)
    lower_bounds = jnp.cumsum(sm, axis=0) - sm[:1]
    cos, s1, s2 = _rope_tables(L)
    tm_post = min(TM_POST, T)
    ltri = jnp.tril(jnp.ones((tm_post, tm_post), BF16), -1)
    fin_g = final_norm_g.reshape(1, D).astype(F32)

    for l in range(depth):
        qa, ka, va, qb, sf, sb, ib, og, ga, gb = _inproj(
            x, attn_norm_g[l].reshape(1, D), w_in[l].astype(BF16), cos, s1, s2)
        ya = _attention(qa, ka, va, sink_logits[l].astype(F32))
        of, orv = _hgrn(qb, sf, sb, ib, lower_bounds[l])

        rw = jnp.concatenate(
            [w_router_group[l], jnp.transpose(w_router_expert[l], (1, 0, 2)).reshape(D, N_EXPERTS)], axis=1)
        rw = jnp.pad(rw.astype(F32), ((0, 0), (0, LANES - rw.shape[1])))
        rwh = rw.astype(BF16)
        rwl = (rw - rwh.astype(F32)).astype(BF16)
        rb = jnp.concatenate([b_router_group[l], b_router_expert[l].reshape(-1)]).astype(F32)
        rb = jnp.pad(rb, (0, LANES - rb.shape[0])).reshape(1, LANES)
        flat = lambda a: a.reshape(T, a.shape[-1])
        xn, hp, rt, cnt = _post(
            flat(x), flat(ya), flat(of), flat(orv), flat(og), flat(ga), flat(gb),
            w_up_attn[l].astype(BF16), w_up_hgrn[l].astype(BF16), w_out[l].astype(BF16),
            hgrn_norm_g[l].reshape(1, HGRN_W).astype(F32), ffn_norm_g[l].reshape(1, D).astype(F32),
            rwh, rwl, rb, ltri)

        dest, block_expert, nused, n_pad = _moe_layout(rt, cnt, T)
        td = min(TD_DISPATCH, T)
        xs = _dispatch(dest.reshape(T // td, 1, 2 * td), hp, jnp.zeros((n_pad, D // 2), U32))
        ys = _experts(block_expert, nused, xs, w_expert_gate[l].astype(BF16),
                      w_expert_up[l].astype(BF16), w_expert_down[l].astype(BF16))
        tc = min(TC_COMBINE, T)
        x = _combine(dest.reshape(T // tc, 1, 2 * tc), xn, rt, fin_g, ys, l == depth - 1).reshape(B, L, D)
    return x
```

```python
import functools

import numpy as np
import jax
import jax.numpy as jnp
from jax import lax
from jax.experimental import pallas as pl
from jax.experimental.pallas import tpu as pltpu

F32 = jnp.float32
BF16 = jnp.bfloat16
U32 = jnp.uint32
I32 = jnp.int32

D_MODEL = 1024
N_Q_HEADS = 8
N_KV_HEADS = 2
HEAD_DIM = 64
WINDOW = 128
ROPE_THETA = 10000.0
MASK_VALUE = -1e30
HGRN_HEADS = 4
HGRN_DIM = 128
LOGF_MIN = -4.0
F_MIN = float(np.exp(LOGF_MIN))
LB_FLOOR = 1e-30
N_GROUPS = 4
EXPERTS_PER_GROUP = 8
N_EXPERTS = N_GROUPS * EXPERTS_PER_GROUP
D_EXPERT = 512
NORM_EPS = 1e-6

ATTN_Q_W = N_Q_HEADS * HEAD_DIM
ATTN_KV_W = N_KV_HEADS * HEAD_DIM
HGRN_W = HGRN_HEADS * HGRN_DIM
OFF_QA = 0
OFF_KA = OFF_QA + ATTN_Q_W
OFF_VA = OFF_KA + ATTN_KV_W
OFF_QB = OFF_VA + ATTN_KV_W
OFF_ZF = OFF_QB + HGRN_W
OFF_ZB = OFF_ZF + HGRN_W
OFF_IB = OFF_ZB + HGRN_W
OFF_OG = OFF_IB + HGRN_W
OFF_GA = OFF_OG + HGRN_W
OFF_GB = OFF_GA + D_MODEL

LANES = 128
VMEM_LIMIT = 56 * 1024 * 1024

TM_PROJ = 512
TQ_ATTN = 512
TH_HGRN = 512
TM_POST = 512
MOE_BLK = 256


def _sigmoid(z):
    return 1.0 / (1.0 + jnp.exp(-z))


def _rms(x):
    return x * lax.rsqrt(jnp.mean(x * x, axis=-1, keepdims=True) + NORM_EPS)


def _nt(a, b):
    return lax.dot_general(a, b, (((1,), (1,)), ((), ())), preferred_element_type=F32)


def _pack_pair(a, b):
    ua = lax.bitcast_convert_type(a.astype(BF16).astype(F32), U32)
    ub = lax.bitcast_convert_type(b.astype(BF16).astype(F32), U32)
    return ua | (ub >> 16)


def _unpack_pair(u):
    hi = lax.bitcast_convert_type(u & jnp.uint32(0xFFFF0000), F32)
    lo = lax.bitcast_convert_type(u << 16, F32)
    return hi, lo


def _inproj_body(x_ref, g_ref, w_ref, cos_ref, s1_ref, s2_ref,
                 qa_ref, ka_ref, va_ref, qb_ref, sf_ref, sb_ref, ib_ref, og_ref, ga_ref, gb_ref):
    h = (_rms(x_ref[0]) * g_ref[...]).astype(BF16)

    def proj(off, width):
        return jnp.dot(h, w_ref[:, off:off + width], preferred_element_type=F32)

    cos, s1, s2 = cos_ref[...], s1_ref[...], s2_ref[...]

    def rope(t):
        return t * cos + pltpu.roll(t, 96, 1) * s1 + pltpu.roll(t, 32, 1) * s2

    q = proj(OFF_QA, ATTN_Q_W)
    for c in range(ATTN_Q_W // LANES):
        sl = slice(c * LANES, (c + 1) * LANES)
        qa_ref[0, :, sl] = (rope(q[:, sl]) * (HEAD_DIM ** -0.5)).astype(BF16)
    k = rope(proj(OFF_KA, ATTN_KV_W))
    ka_ref[0, :, :LANES] = k.astype(BF16)
    ka_ref[0, :, LANES:] = pltpu.roll(k, 64, 1).astype(BF16)
    v = proj(OFF_VA, ATTN_KV_W)
    va_ref[0, :, :LANES] = v.astype(BF16)
    va_ref[0, :, LANES:] = pltpu.roll(v, 64, 1).astype(BF16)
    t = proj(OFF_QB, HGRN_W)
    qb_ref[0] = (t * _sigmoid(t)).astype(BF16)
    sf_ref[0] = _sigmoid(proj(OFF_ZF, HGRN_W)).astype(BF16)
    sb_ref[0] = _sigmoid(proj(OFF_ZB, HGRN_W)).astype(BF16)
    ib_ref[0] = proj(OFF_IB, HGRN_W).astype(BF16)
    t = proj(OFF_OG, HGRN_W)
    og_ref[0] = (t * _sigmoid(t)).astype(BF16)
    ga_ref[0] = _sigmoid(proj(OFF_GA, D_MODEL)).astype(BF16)
    gb_ref[0] = _sigmoid(proj(OFF_GB, D_MODEL)).astype(BF16)


def _inproj(x, g, w_bf16, cos, s1, s2):
    B, L, D = x.shape
    tm = min(TM_PROJ, L)
    widths = (ATTN_Q_W, 2 * ATTN_KV_W, 2 * ATTN_KV_W, HGRN_W, HGRN_W, HGRN_W, HGRN_W, HGRN_W,
              D_MODEL, D_MODEL)
    tab = pl.BlockSpec((tm, LANES), lambda b, i: (i, 0))
    return pl.pallas_call(
        _inproj_body,
        grid=(B, L // tm),
        in_specs=[
            pl.BlockSpec((1, tm, D), lambda b, i: (b, i, 0)),
            pl.BlockSpec((1, D), lambda b, i: (0, 0)),
            pl.BlockSpec(w_bf16.shape, lambda b, i: (0, 0)),
            tab, tab, tab,
        ],
        out_specs=[pl.BlockSpec((1, tm, w), lambda b, i: (b, i, 0)) for w in widths],
        out_shape=[jax.ShapeDtypeStruct((B, L, w), BF16) for w in widths],
        compiler_params=pltpu.CompilerParams(
            dimension_semantics=("arbitrary", "arbitrary"), vmem_limit_bytes=VMEM_LIMIT),
        name="inproj",
    )(x, g, w_bf16, cos, s1, s2)


def _attn_body(sink_ref, q_ref, kp_ref, km_ref, kn_ref, vp_ref, vm_ref, vn_ref, o_ref, *, tq, seq):
    i = pl.program_id(1)
    kw = jnp.concatenate([kp_ref[0], km_ref[0], kn_ref[0]], axis=0)
    vw = jnp.concatenate([vp_ref[0], vm_ref[0], vn_ref[0]], axis=0)
    lo = lax.broadcasted_iota(I32, (1, LANES), 1) < HEAD_DIM
    zero = jnp.zeros((), BF16)

    def variants(t):
        a, b = t[:, :LANES], t[:, LANES:]
        return ((jnp.where(lo, a, zero), jnp.where(lo, zero, b)),
                (jnp.where(lo, b, zero), jnp.where(lo, zero, a)))

    kvar, vvar = variants(kw), variants(vw)
    row = lax.broadcasted_iota(I32, (WINDOW, 3 * WINDOW), 0)
    col = lax.broadcasted_iota(I32, (WINDOW, 3 * WINDOW), 1)
    band = (col >= row) & (col <= row + 2 * WINDOW)
    for j in range(tq // WINDOW):
        kpos = col + (i * tq + (j - 1) * WINDOW)
        valid = band & (kpos >= 0) & (kpos < seq)
        rows = slice(j * WINDOW, (j + 1) * WINDOW)
        win = slice(j * WINDOW, (j + 3) * WINDOW)
        for c in range(ATTN_Q_W // LANES):
            g = (2 * c) // (N_Q_HEADS // N_KV_HEADS)
            qc = q_ref[0, rows, c * LANES:(c + 1) * LANES]
            ps, ls = [], []
            for hh in range(2):
                s = _nt(qc, kvar[g][hh][win])
                s = jnp.where(valid, s, MASK_VALUE)
                sk = sink_ref[2 * c + hh]
                m = jnp.maximum(jnp.max(s, axis=1, keepdims=True), sk)
                p = jnp.exp(s - m)
                ls.append(jnp.sum(p, axis=1, keepdims=True) + jnp.exp(sk - m))
                ps.append(p.astype(BF16))
            pv = jnp.dot(jnp.concatenate(ps, axis=1),
                         jnp.concatenate([vvar[g][0][win], vvar[g][1][win]], axis=0),
                         preferred_element_type=F32)
            inv = jnp.where(lo, 1.0 / ls[0], 1.0 / ls[1])
            o_ref[0, rows, c * LANES:(c + 1) * LANES] = (pv * inv).astype(BF16)


def _attention(qa, ka, va, sink):
    B, L, _ = qa.shape
    tq = min(TQ_ATTN, L)
    r = tq // WINDOW
    nb = L // WINDOW
    kvw = 2 * ATTN_KV_W
    main = pl.BlockSpec((1, tq, kvw), lambda b, i: (b, i, 0))
    prev = pl.BlockSpec((1, WINDOW, kvw), lambda b, i: (b, jnp.maximum(i * r - 1, 0), 0))
    nxt = pl.BlockSpec((1, WINDOW, kvw), lambda b, i: (b, jnp.minimum(i * r + r, nb - 1), 0))
    return pl.pallas_call(
        functools.partial(_attn_body, tq=tq, seq=L),
        grid=(B, L // tq),
        in_specs=[
            pl.BlockSpec(memory_space=pltpu.SMEM),
            pl.BlockSpec((1, tq, ATTN_Q_W), lambda b, i: (b, i, 0)),
            prev, main, nxt, prev, main, nxt,
        ],
        out_specs=pl.BlockSpec((1, tq, ATTN_Q_W), lambda b, i: (b, i, 0)),
        out_shape=jax.ShapeDtypeStruct((B, L, ATTN_Q_W), BF16),
        compiler_params=pltpu.CompilerParams(
            dimension_semantics=("arbitrary", "arbitrary"), vmem_limit_bytes=VMEM_LIMIT),
        name="win_attn",
    )(sink, qa, ka, ka, ka, va, va, va)


HB = 128
LEAF = 16


def _hgrn_masks(reverse):
    r = lax.broadcasted_iota(I32, (HB, HB), 0)
    c = lax.broadcasted_iota(I32, (HB, HB), 1)
    if reverse:
        r, c = c, r
    tri = jnp.where(c <= r, 1.0, 0.0).astype(BF16)
    same = lambda s: (r >> s) == (c >> s)
    cross = lambda bit: ((r & bit) != 0) & ((c & bit) == 0)
    lev = jnp.where(same(4) & (c <= r), 0,
                    jnp.where(same(5) & cross(16), 1,
                              jnp.where(same(6) & cross(32), 2, 3)))
    return tri, lev


def _hgrn_unit(q, s, v, lb, lbc, st_ref, tri, lev, reverse):
    f = lbc + (1.0 - lb) * s
    logf = jnp.maximum(jnp.log(f), LOGF_MIN)
    kk = 1.0 - jnp.maximum(f, F_MIN)
    hi = logf.astype(BF16)
    r1 = logf - hi.astype(F32)
    mid = r1.astype(BF16)
    low = (r1 - mid.astype(F32)).astype(BF16)
    dot = functools.partial(jnp.dot, preferred_element_type=F32)
    cum = dot(tri, hi) + dot(tri, mid) + dot(tri, low)

    def ref_row(idx):
        return cum[idx:idx + 1]

    total = ref_row(0) if reverse else ref_row(HB - 1)
    q_blk = q * jnp.exp(cum)
    k_blk = kk * jnp.exp(total - cum)

    ql, kl = [], []
    for c in range(HB // LEAF):
        rows = slice(c * LEAF, (c + 1) * LEAF)
        edge = (c + 1) * LEAF if reverse else c * LEAF - 1
        loc = cum[rows] - ref_row(edge) if 0 <= edge < HB else cum[rows]
        ql.append(q[rows] * jnp.exp(loc))
        kl.append(kk[rows] * jnp.exp(-loc))
    pairs = [(jnp.concatenate(ql, axis=0), jnp.concatenate(kl, axis=0))]
    for w in (16, 32, 64):
        qp, kp = [], []
        zeros = jnp.zeros((w, HB), F32)
        for s0 in range(0, HB, 2 * w):
            first, second = slice(s0, s0 + w), slice(s0 + w, s0 + 2 * w)
            if reverse:
                ref = ref_row(s0 + w)
                qp += [q[first] * jnp.exp(cum[first] - ref), zeros]
                kp += [zeros, kk[second] * jnp.exp(ref - cum[second])]
            else:
                ref = ref_row(s0 + w - 1)
                qp += [zeros, q[second] * jnp.exp(cum[second] - ref)]
                kp += [kk[first] * jnp.exp(ref - cum[first]), zeros]
        pairs.append((jnp.concatenate(qp, axis=0), jnp.concatenate(kp, axis=0)))
    a0, a1, a2, a3 = [_nt(qq.astype(BF16), kq.astype(BF16)) for qq, kq in pairs]
    a = jnp.where(lev == 0, a0, jnp.where(lev == 1, a1, jnp.where(lev == 2, a2, a3)))
    st = st_ref[...]
    o = dot(a.astype(BF16), v) + _nt(q_blk.astype(BF16), st.astype(BF16))
    st_ref[...] = st * jnp.exp(total) + lax.dot_general(
        v, k_blk.astype(BF16), (((0,), (0,)), ((), ())), preferred_element_type=F32)
    return o


def _hgrn_body(lb_ref, qf_ref, sf_ref, vf_ref, qr_ref, sr_ref, vr_ref, of_ref, or_ref,
               stf_ref, str_ref, *, th):
    @pl.when(pl.program_id(2) == 0)
    def _():
        stf_ref[...] = jnp.zeros_like(stf_ref)
        str_ref[...] = jnp.zeros_like(str_ref)

    lb = lb_ref[...]
    lbc = jnp.maximum(lb, LB_FLOOR)
    tri_f, lev_f = _hgrn_masks(False)
    tri_r, lev_r = _hgrn_masks(True)
    nblk = th // HB

    def body(t, carry):
        rf = pl.multiple_of(t * HB, HB)
        rr = pl.multiple_of((nblk - 1 - t) * HB, HB)
        o = _hgrn_unit(qf_ref[0, pl.ds(rf, HB), :].astype(F32), sf_ref[0, pl.ds(rf, HB), :].astype(F32),
                       vf_ref[0, pl.ds(rf, HB), :], lb[0:1], lbc[0:1], stf_ref, tri_f, lev_f, False)
        of_ref[0, pl.ds(rf, HB), :] = o.astype(BF16)
        o = _hgrn_unit(qr_ref[0, pl.ds(rr, HB), :].astype(F32), sr_ref[0, pl.ds(rr, HB), :].astype(F32),
                       vr_ref[0, pl.ds(rr, HB), :], lb[1:2], lbc[1:2], str_ref, tri_r, lev_r, True)
        or_ref[0, pl.ds(rr, HB), :] = o.astype(BF16)
        return carry

    lax.fori_loop(0, nblk, body, 0)


def _hgrn(qb, sf, sb, ib, lb):
    B, L, _ = qb.shape
    th = min(TH_HGRN, L)
    n = L // th
    fwd = pl.BlockSpec((1, th, HGRN_DIM), lambda b, h, i: (b, i, h))
    rev = pl.BlockSpec((1, th, HGRN_DIM), lambda b, h, i: (b, n - 1 - i, h))
    return pl.pallas_call(
        functools.partial(_hgrn_body, th=th),
        grid=(B, HGRN_HEADS, n),
        in_specs=[pl.BlockSpec((2, HGRN_DIM), lambda b, h, i: (0, h)), fwd, fwd, fwd, rev, rev, rev],
        out_specs=[fwd, rev],
        out_shape=[jax.ShapeDtypeStruct((B, L, HGRN_W), BF16)] * 2,
        scratch_shapes=[pltpu.VMEM((HGRN_DIM, HGRN_DIM), F32)] * 2,
        compiler_params=pltpu.CompilerParams(
            dimension_semantics=("arbitrary", "arbitrary", "arbitrary"), vmem_limit_bytes=VMEM_LIMIT),
        name="hgrn2",
    )(lb, qb, sf, ib, qb, sb, ib)


def _post_body(x_ref, ya_ref, of_ref, or_ref, og_ref, ga_ref, gb_ref, wua_ref, wuh_ref, wo_ref,
               hg_ref, fg_ref, rwh_ref, rwl_ref, rb_ref, ltri_ref,
               xn_ref, h2_ref, rt_ref, cnt_ref):
    dot = functools.partial(jnp.dot, preferred_element_type=F32)
    o = of_ref[...].astype(F32) + or_ref[...].astype(F32)
    yh = (_rms(o) * hg_ref[...] * og_ref[...].astype(F32)).astype(BF16)
    merged = (ga_ref[...].astype(F32) * dot(ya_ref[...], wua_ref[...])
              + gb_ref[...].astype(F32) * dot(yh, wuh_ref[...]))
    xn = x_ref[...] + dot(merged.astype(BF16), wo_ref[...])
    xn_ref[...] = xn
    h2 = _rms(xn) * fg_ref[...]
    hi = h2.astype(BF16)
    h2_ref[...] = hi

    lo = (h2 - hi.astype(F32)).astype(BF16)
    logits = dot(hi, rwh_ref[...]) + dot(lo, rwh_ref[...]) + dot(hi, rwl_ref[...]) + rb_ref[...]
    lane = lax.broadcasted_iota(I32, logits.shape, 1)
    lanef = lane.astype(F32)
    big = jnp.float32(1e9)
    ninf = jnp.float32(-jnp.inf)
    red = dict(axis=1, keepdims=True)
    gmask = lane < N_GROUPS
    mg = jnp.max(jnp.where(gmask, logits, ninf), **red)
    p_top = 1.0 / jnp.sum(jnp.where(gmask, jnp.exp(logits - mg), 0.0), **red)
    gi = jnp.min(jnp.where(gmask & (logits == mg), lanef, big), **red)
    emask = ((lane >= N_GROUPS) & (lane < N_GROUPS + N_EXPERTS)
             & (((lane - N_GROUPS) >> 3).astype(F32) == gi))
    le = jnp.where(emask, logits, ninf)
    m1 = jnp.max(le, **red)
    i1 = jnp.min(jnp.where(le == m1, lanef, big), **red)
    le2 = jnp.where(lanef == i1, ninf, le)
    m2 = jnp.max(le2, **red)
    i2 = jnp.min(jnp.where(le2 == m2, lanef, big), **red)
    r = jnp.exp(m2 - m1)
    w1 = 1.0 / (1.0 + r)
    w2 = r * w1
    e1 = i1 - N_GROUPS
    e2 = i2 - N_GROUPS
    is1, is2 = lanef == e1, lanef == e2
    onehot = jnp.where(is1 | is2, 1.0, 0.0)
    pref = dot(ltri_ref[...], onehot.astype(BF16))
    rank1 = jnp.sum(jnp.where(is1, pref, 0.0), **red)
    rank2 = jnp.sum(jnp.where(is2, pref, 0.0), **red)
    cnt_ref[...] = jnp.broadcast_to(jnp.sum(onehot, axis=0, keepdims=True), cnt_ref.shape)
    vals = (e1, e2, rank1, rank2, p_top * w1, p_top * w2)
    rt = jnp.zeros_like(logits)
    for idx, val in enumerate(vals):
        rt = jnp.where(lane == idx, val, rt)
    rt_ref[...] = rt


RT_E1, RT_E2, RT_RANK1, RT_RANK2, RT_W1, RT_W2 = range(6)
SUBLANES = 8


def _post(x, ya, of, orv, og, ga, gb, wua, wuh, wo, hg, fg, rwh, rwl, rb, ltri):
    T, D = x.shape
    tm = ltri.shape[0]
    row = lambda w: pl.BlockSpec((tm, w), lambda i: (i, 0))
    full = lambda a: pl.BlockSpec(a.shape, lambda i: (0,) * a.ndim)
    return pl.pallas_call(
        _post_body,
        grid=(T // tm,),
        in_specs=[row(D), row(ATTN_Q_W), row(HGRN_W), row(HGRN_W), row(HGRN_W), row(D), row(D),
                  full(wua), full(wuh), full(wo), full(hg), full(fg), full(rwh), full(rwl), full(rb),
                  full(ltri)],
        out_specs=[row(D), row(D), row(LANES), pl.BlockSpec((SUBLANES, LANES), lambda i: (i, 0))],
        out_shape=[jax.ShapeDtypeStruct((T, D), F32), jax.ShapeDtypeStruct((T, D), BF16),
                   jax.ShapeDtypeStruct((T, LANES), F32),
                   jax.ShapeDtypeStruct((T // tm * SUBLANES, LANES), F32)],
        compiler_params=pltpu.CompilerParams(
            dimension_semantics=("arbitrary",), vmem_limit_bytes=VMEM_LIMIT),
        name="post_mixer_router",
    )(x, ya, of, orv, og, ga, gb, wua, wuh, wo, hg, fg, rwh, rwl, rb, ltri)


XS_W = D_MODEL // 2 + LANES
SEG_BIG = 32
META_DST, META_N8, META_LOFF = 0, N_EXPERTS, 2 * N_EXPERTS


def _local_rows(tm):
    return 2 * tm + N_EXPERTS * SUBLANES


def _local_perm(rt, loff_row, rows):
    lanef = lax.broadcasted_iota(I32, rt.shape, 1).astype(F32)
    col = lax.broadcasted_iota(I32, (rt.shape[0], rows), 1)
    masks = []
    for e_lane, r_lane in ((RT_E1, RT_RANK1), (RT_E2, RT_RANK2)):
        e = rt[:, e_lane:e_lane + 1]
        off = jnp.sum(jnp.where(lanef == e, loff_row, 0.0), axis=1, keepdims=True)
        hit = col == (off + rt[:, r_lane:r_lane + 1]).astype(I32)
        masks.append(jnp.where(hit, 1.0, 0.0).astype(BF16))
    return masks


def _segment_copies(meta_ref, make_copy, action):
    def per_expert(e, c):
        dst = meta_ref[0, 0, META_DST + e]
        n8 = meta_ref[0, 0, META_N8 + e]
        loff = meta_ref[0, 0, META_LOFF + e]
        n_big = n8 // (SEG_BIG // SUBLANES)

        def big(k, c):
            o = k * SEG_BIG
            action(make_copy(pl.multiple_of(loff + o, SUBLANES), pl.multiple_of(dst + o, SUBLANES), SEG_BIG))
            return c

        lax.fori_loop(0, n_big, big, 0)

        def small(k, c):
            o = k * SUBLANES
            action(make_copy(pl.multiple_of(loff + o, SUBLANES), pl.multiple_of(dst + o, SUBLANES), SUBLANES))
            return c

        lax.fori_loop(n_big * (SEG_BIG // SUBLANES), n8, small, 0)
        return c

    lax.fori_loop(0, N_EXPERTS, per_expert, 0)


def _dispatch_body(meta_ref, loff_ref, rt_ref, h_ref, xs_in_ref, xs_ref, buf_ref, sem):
    del xs_in_ref
    tm = rt_ref.shape[0]
    rows = buf_ref.shape[0]
    rt = rt_ref[...]
    p1, p2 = _local_perm(rt, loff_ref[0], rows)
    tn = lambda a, b: lax.dot_general(a, b, (((0,), (0,)), ((), ())), preferred_element_type=F32)
    xl = tn(p1 + p2, h_ref[...])
    half = D_MODEL // 2
    buf_ref[:, :half] = _pack_pair(xl[:, :half], xl[:, half:])
    lane = lax.broadcasted_iota(I32, (tm, LANES), 1)

    def wcols(w):
        hi = w.astype(BF16).astype(F32)
        return jnp.where(lane == 0, hi, jnp.where(lane == 1, w - hi, 0.0)).astype(BF16)

    wl = tn(p1, wcols(rt[:, RT_W1:RT_W1 + 1])) + tn(p2, wcols(rt[:, RT_W2:RT_W2 + 1]))
    buf_ref[:, half:] = lax.bitcast_convert_type(wl, U32)

    def make_copy(src_row, dst_row, n):
        return pltpu.make_async_copy(buf_ref.at[pl.ds(src_row, n)], xs_ref.at[pl.ds(dst_row, n)], sem)

    _segment_copies(meta_ref, make_copy, lambda cp: cp.start())
    _segment_copies(meta_ref, make_copy, lambda cp: cp.wait())


def _dispatch(meta, loff, rt, h2, xs_init):
    T, D = h2.shape
    nt = meta.shape[0]
    tm = T // nt
    return pl.pallas_call(
        _dispatch_body,
        grid=(nt,),
        in_specs=[pl.BlockSpec((1, 1, meta.shape[2]), lambda i: (i, 0, 0), memory_space=pltpu.SMEM),
                  pl.BlockSpec((1, 1, LANES), lambda i: (i, 0, 0)),
                  pl.BlockSpec((tm, LANES), lambda i: (i, 0)),
                  pl.BlockSpec((tm, D), lambda i: (i, 0)),
                  pl.BlockSpec(memory_space=pl.ANY)],
        out_specs=pl.BlockSpec(memory_space=pl.ANY),
        out_shape=jax.ShapeDtypeStruct(xs_init.shape, U32),
        scratch_shapes=[pltpu.VMEM((_local_rows(tm), XS_W), U32), pltpu.SemaphoreType.DMA(())],
        input_output_aliases={4: 0},
        compiler_params=pltpu.CompilerParams(
            dimension_semantics=("arbitrary",), vmem_limit_bytes=VMEM_LIMIT),
        name="moe_dispatch",
    )(meta, loff, rt, h2, xs_init)


def _expert_body(be_ref, nused_ref, xs_ref, wg_ref, wu_ref, wd_ref, ys_ref):
    i = pl.program_id(0)

    @pl.when(i < nused_ref[0])
    def _():
        dot = functools.partial(jnp.dot, preferred_element_type=F32)
        half = D_MODEL // 2
        ha, hb = _unpack_pair(xs_ref[:, :half])
        h = jnp.concatenate([ha, hb], axis=1).astype(BF16)
        wl = lax.bitcast_convert_type(xs_ref[:, half:], F32)
        w = wl[:, 0:1] + wl[:, 1:2]
        g = dot(h, wg_ref[0])
        u = dot(h, wu_ref[0])
        y = dot((g * _sigmoid(g) * u).astype(BF16), wd_ref[0]) * w
        ys_ref[...] = _pack_pair(y[:, :half], y[:, half:])

    @pl.when(i >= nused_ref[0])
    def _():
        ys_ref[...] = jnp.zeros_like(ys_ref)


def _experts(block_expert, nused, xs, wg, wu, wd):
    n_pad = xs.shape[0]
    blk = lambda w: pl.BlockSpec((MOE_BLK, w), lambda i, be, nu: (i, 0))
    wspec = lambda a: pl.BlockSpec((1,) + a.shape[1:], lambda i, be, nu: (be[i], 0, 0))
    return pl.pallas_call(
        _expert_body,
        grid_spec=pltpu.PrefetchScalarGridSpec(
            num_scalar_prefetch=2,
            grid=(n_pad // MOE_BLK,),
            in_specs=[blk(XS_W), wspec(wg), wspec(wu), wspec(wd)],
            out_specs=blk(D_MODEL // 2),
        ),
        out_shape=jax.ShapeDtypeStruct((n_pad, D_MODEL // 2), U32),
        compiler_params=pltpu.CompilerParams(
            dimension_semantics=("arbitrary",), vmem_limit_bytes=VMEM_LIMIT),
        name="moe_experts",
    )(block_expert, nused, xs, wg, wu, wd)


def _combine_body(meta_ref, loff_ref, rt_ref, x_ref, fg_ref, ys_ref, o_ref, buf_ref, sem, *, final_norm):
    @pl.when(pl.program_id(0) == 0)
    def _():
        buf_ref[...] = jnp.zeros_like(buf_ref)

    def make_copy(loc_row, glob_row, n):
        return pltpu.make_async_copy(ys_ref.at[pl.ds(glob_row, n)], buf_ref.at[pl.ds(loc_row, n)], sem)

    _segment_copies(meta_ref, make_copy, lambda cp: cp.start())
    p1, p2 = _local_perm(rt_ref[...], loff_ref[0], buf_ref.shape[0])
    pt = p1 + p2
    _segment_copies(meta_ref, make_copy, lambda cp: cp.wait())
    yl = jnp.concatenate(_unpack_pair(buf_ref[...]), axis=1).astype(BF16)
    out = x_ref[...] + jnp.dot(pt, yl, preferred_element_type=F32)
    if final_norm:
        out = _rms(out) * fg_ref[...]
    o_ref[...] = out


def _combine(meta, loff, rt, x, fg, ys, final_norm):
    T, D = x.shape
    nt = meta.shape[0]
    tm = T // nt
    return pl.pallas_call(
        functools.partial(_combine_body, final_norm=final_norm),
        grid=(nt,),
        in_specs=[pl.BlockSpec((1, 1, meta.shape[2]), lambda i: (i, 0, 0), memory_space=pltpu.SMEM),
                  pl.BlockSpec((1, 1, LANES), lambda i: (i, 0, 0)),
                  pl.BlockSpec((tm, LANES), lambda i: (i, 0)),
                  pl.BlockSpec((tm, D), lambda i: (i, 0)),
                  pl.BlockSpec((1, D), lambda i: (0, 0)),
                  pl.BlockSpec(memory_space=pl.ANY)],
        out_specs=pl.BlockSpec((tm, D), lambda i: (i, 0)),
        out_shape=jax.ShapeDtypeStruct((T, D), F32),
        scratch_shapes=[pltpu.VMEM((_local_rows(tm), D // 2), U32), pltpu.SemaphoreType.DMA(())],
        compiler_params=pltpu.CompilerParams(
            dimension_semantics=("arbitrary",), vmem_limit_bytes=VMEM_LIMIT),
        name="moe_combine",
    )(meta, loff, rt, x, fg, ys)


def _rope_tables(L):
    half = HEAD_DIM // 2
    inv = ROPE_THETA ** (-jnp.arange(half, dtype=F32) / half)
    ang = jnp.arange(L, dtype=F32)[:, None] * inv[None, :]
    lane = np.arange(LANES)
    idx = lane % half
    first = jnp.asarray((lane % HEAD_DIM) < half)
    cos = jnp.cos(ang)[:, idx]
    sin = jnp.sin(ang)[:, idx]
    return cos, jnp.where(first, -sin, 0.0), jnp.where(first, 0.0, sin)


def _moe_layout(cnt, T):
    nt = cnt.shape[0] // SUBLANES
    c = cnt.reshape(nt, SUBLANES, LANES)[:, 0, :N_EXPERTS].astype(I32)
    c8 = (c + SUBLANES - 1) // SUBLANES * SUBLANES
    tile_off = jnp.cumsum(c8, axis=0) - c8
    total = jnp.sum(c8, axis=0)
    padded = (total + MOE_BLK - 1) // MOE_BLK * MOE_BLK
    pend = jnp.cumsum(padded)
    dst = (pend - padded)[None, :] + tile_off
    loff = jnp.cumsum(c8, axis=1) - c8
    meta = jnp.concatenate([dst, c8 // SUBLANES, loff], axis=1).reshape(nt, 1, 3 * N_EXPERTS)
    loff_f = jnp.pad(loff.astype(F32), ((0, 0), (0, LANES - N_EXPERTS))).reshape(nt, 1, LANES)
    n_pad = 2 * T + nt * N_EXPERTS * SUBLANES + N_EXPERTS * MOE_BLK
    n_pad = (n_pad + MOE_BLK - 1) // MOE_BLK * MOE_BLK
    starts = jnp.arange(n_pad // MOE_BLK, dtype=I32) * MOE_BLK
    block_expert = jnp.minimum(jnp.sum(starts[:, None] >= pend[None, :], axis=1), N_EXPERTS - 1)
    nused = (pend[-1:] // MOE_BLK).astype(I32)
    return meta, loff_f, block_expert.astype(I32), nused, n_pad


def kernel(x, attn_norm_g, w_in, sink_logits, hgrn_lb_table, hgrn_norm_g, w_up_attn, w_up_hgrn, w_out,
           ffn_norm_g, w_router_group, b_router_group, w_router_expert, b_router_expert,
           w_expert_gate, w_expert_up, w_expert_down, final_norm_g):
    B, L, D = x.shape
    T = B * L
    depth = w_in.shape[0]
    sm = jax.nn.softmax(hgrn_lb_table.astype(F32), axis=0)
    lower_bounds = jnp.cumsum(sm, axis=0) - sm[:1]
    cos, s1, s2 = _rope_tables(L)
    tm_post = min(TM_POST, T)
    ltri = jnp.tril(jnp.ones((tm_post, tm_post), BF16), -1)
    fin_g = final_norm_g.reshape(1, D).astype(F32)

    for l in range(depth):
        qa, ka, va, qb, sf, sb, ib, og, ga, gb = _inproj(
            x, attn_norm_g[l].reshape(1, D), w_in[l].astype(BF16), cos, s1, s2)
        ya = _attention(qa, ka, va, sink_logits[l].astype(F32))
        of, orv = _hgrn(qb, sf, sb, ib, lower_bounds[l])

        rw = jnp.concatenate(
            [w_router_group[l], jnp.transpose(w_router_expert[l], (1, 0, 2)).reshape(D, N_EXPERTS)], axis=1)
        rw = jnp.pad(rw.astype(F32), ((0, 0), (0, LANES - rw.shape[1])))
        rwh = rw.astype(BF16)
        rwl = (rw - rwh.astype(F32)).astype(BF16)
        rb = jnp.concatenate([b_router_group[l], b_router_expert[l].reshape(-1)]).astype(F32)
        rb = jnp.pad(rb, (0, LANES - rb.shape[0])).reshape(1, LANES)
        flat = lambda a: a.reshape(T, a.shape[-1])
        xn, h2, rt, cnt = _post(
            flat(x), flat(ya), flat(of), flat(orv), flat(og), flat(ga), flat(gb),
            w_up_attn[l].astype(BF16), w_up_hgrn[l].astype(BF16), w_out[l].astype(BF16),
            hgrn_norm_g[l].reshape(1, HGRN_W).astype(F32), ffn_norm_g[l].reshape(1, D).astype(F32),
            rwh, rwl, rb, ltri)

        meta, loff, block_expert, nused, n_pad = _moe_layout(cnt, T)
        xs = _dispatch(meta, loff, rt, h2, jnp.zeros((n_pad, XS_W), U32))
        ys = _experts(block_expert, nused, xs, w_expert_gate[l].astype(BF16),
                      w_expert_up[l].astype(BF16), w_expert_down[l].astype(BF16))
        x = _combine(meta, loff, rt, xn, fin_g, ys, l == depth - 1).reshape(B, L, D)
    return x
```

```python
import functools

import numpy as np
import jax
import jax.numpy as jnp
from jax import lax
from jax.experimental import pallas as pl
from jax.experimental.pallas import tpu as pltpu

F32 = jnp.float32
BF16 = jnp.bfloat16
U32 = jnp.uint32
I32 = jnp.int32

D_MODEL = 1024
N_Q_HEADS = 8
N_KV_HEADS = 2
HEAD_DIM = 64
WINDOW = 128
ROPE_THETA = 10000.0
MASK_VALUE = -1e30
HGRN_HEADS = 4
HGRN_DIM = 128
LOGF_MIN = -4.0
F_MIN = float(np.exp(LOGF_MIN))
LOG2E = float(np.log2(np.e))
LB_FLOOR = 1e-30
N_GROUPS = 4
EXPERTS_PER_GROUP = 8
N_EXPERTS = N_GROUPS * EXPERTS_PER_GROUP
D_EXPERT = 512
NORM_EPS = 1e-6

ATTN_Q_W = N_Q_HEADS * HEAD_DIM
ATTN_KV_W = N_KV_HEADS * HEAD_DIM
HGRN_W = HGRN_HEADS * HGRN_DIM
OFF_QA = 0
OFF_KA = OFF_QA + ATTN_Q_W
OFF_VA = OFF_KA + ATTN_KV_W
OFF_QB = OFF_VA + ATTN_KV_W
OFF_ZF = OFF_QB + HGRN_W
OFF_ZB = OFF_ZF + HGRN_W
OFF_IB = OFF_ZB + HGRN_W
OFF_OG = OFF_IB + HGRN_W
OFF_GA = OFF_OG + HGRN_W
OFF_GB = OFF_GA + D_MODEL

LANES = 128
VMEM_LIMIT = 56 * 1024 * 1024

TM_PROJ = 512
TQ_ATTN = 512
TH_HGRN = 512
TM_POST = 512
MOE_BLK = 256


def _sigmoid(z):
    return 1.0 / (1.0 + jnp.exp(-z))


def _rms(x):
    return x * lax.rsqrt(jnp.mean(x * x, axis=-1, keepdims=True) + NORM_EPS)


def _nt(a, b):
    return lax.dot_general(a, b, (((1,), (1,)), ((), ())), preferred_element_type=F32)


def _pack_pair(a, b):
    ua = lax.bitcast_convert_type(a.astype(BF16).astype(F32), U32)
    ub = lax.bitcast_convert_type(b.astype(BF16).astype(F32), U32)
    return ua | (ub >> 16)


def _unpack_pair(u):
    hi = lax.bitcast_convert_type(u & jnp.uint32(0xFFFF0000), F32)
    lo = lax.bitcast_convert_type(u << 16, F32)
    return hi, lo


def _inproj_body(x_ref, g_ref, w_ref, cos_ref, s1_ref, s2_ref,
                 qa_ref, ka_ref, va_ref, qb_ref, sf_ref, sb_ref, ib_ref, og_ref, ga_ref, gb_ref):
    h = (_rms(x_ref[0]) * g_ref[...]).astype(BF16)

    def proj(off, width):
        return jnp.dot(h, w_ref[:, off:off + width], preferred_element_type=F32)

    cos, s1, s2 = cos_ref[...], s1_ref[...], s2_ref[...]

    def rope(t):
        return t * cos + pltpu.roll(t, 96, 1) * s1 + pltpu.roll(t, 32, 1) * s2

    q = proj(OFF_QA, ATTN_Q_W)
    for c in range(ATTN_Q_W // LANES):
        sl = slice(c * LANES, (c + 1) * LANES)
        qa_ref[0, :, sl] = (rope(q[:, sl]) * (HEAD_DIM ** -0.5)).astype(BF16)
    k = rope(proj(OFF_KA, ATTN_KV_W))
    ka_ref[0, :, :LANES] = k.astype(BF16)
    ka_ref[0, :, LANES:] = pltpu.roll(k, 64, 1).astype(BF16)
    v = proj(OFF_VA, ATTN_KV_W)
    va_ref[0, :, :LANES] = v.astype(BF16)
    va_ref[0, :, LANES:] = pltpu.roll(v, 64, 1).astype(BF16)
    t = proj(OFF_QB, HGRN_W)
    qb_ref[0] = (t * _sigmoid(t)).astype(BF16)
    sf_ref[0] = _sigmoid(proj(OFF_ZF, HGRN_W)).astype(BF16)
    sb_ref[0] = _sigmoid(proj(OFF_ZB, HGRN_W)).astype(BF16)
    ib_ref[0] = proj(OFF_IB, HGRN_W).astype(BF16)
    t = proj(OFF_OG, HGRN_W)
    og_ref[0] = (t * _sigmoid(t)).astype(BF16)
    ga_ref[0] = _sigmoid(proj(OFF_GA, D_MODEL)).astype(BF16)
    gb_ref[0] = _sigmoid(proj(OFF_GB, D_MODEL)).astype(BF16)


def _inproj(x, g, w_all, layer, cos, s1, s2):
    B, L, D = x.shape
    tm = min(TM_PROJ, L)
    widths = (ATTN_Q_W, 2 * ATTN_KV_W, 2 * ATTN_KV_W, HGRN_W, HGRN_W, HGRN_W, HGRN_W, HGRN_W,
              D_MODEL, D_MODEL)
    tab = pl.BlockSpec((tm, LANES), lambda b, i: (i, 0))
    return pl.pallas_call(
        _inproj_body,
        grid=(B, L // tm),
        in_specs=[
            pl.BlockSpec((1, tm, D), lambda b, i: (b, i, 0)),
            pl.BlockSpec((1, D), lambda b, i: (0, 0)),
            pl.BlockSpec((None,) + w_all.shape[1:], lambda b, i: (layer, 0, 0)),
            tab, tab, tab,
        ],
        out_specs=[pl.BlockSpec((1, tm, w), lambda b, i: (b, i, 0)) for w in widths],
        out_shape=[jax.ShapeDtypeStruct((B, L, w), BF16) for w in widths],
        compiler_params=pltpu.CompilerParams(
            dimension_semantics=("arbitrary", "arbitrary"), vmem_limit_bytes=VMEM_LIMIT),
        name="inproj",
    )(x, g, w_all, cos, s1, s2)


def _attn_body(sink_ref, q_ref, kp_ref, km_ref, kn_ref, vp_ref, vm_ref, vn_ref, o_ref, *, tq, seq):
    i = pl.program_id(1)
    kw = jnp.concatenate([kp_ref[0], km_ref[0], kn_ref[0]], axis=0)
    vw = jnp.concatenate([vp_ref[0], vm_ref[0], vn_ref[0]], axis=0)
    lo = lax.broadcasted_iota(I32, (1, LANES), 1) < HEAD_DIM
    zero = jnp.zeros((), BF16)

    def variants(t):
        a, b = t[:, :LANES], t[:, LANES:]
        return ((jnp.where(lo, a, zero), jnp.where(lo, zero, b)),
                (jnp.where(lo, b, zero), jnp.where(lo, zero, a)))

    kvar, vvar = variants(kw), variants(vw)
    row = lax.broadcasted_iota(I32, (WINDOW, 3 * WINDOW), 0)
    col = lax.broadcasted_iota(I32, (WINDOW, 3 * WINDOW), 1)
    band = (col >= row) & (col <= row + 2 * WINDOW)
    for j in range(tq // WINDOW):
        kpos = col + (i * tq + (j - 1) * WINDOW)
        valid = band & (kpos >= 0) & (kpos < seq)
        rows = slice(j * WINDOW, (j + 1) * WINDOW)
        win = slice(j * WINDOW, (j + 3) * WINDOW)
        for c in range(ATTN_Q_W // LANES):
            g = (2 * c) // (N_Q_HEADS // N_KV_HEADS)
            qc = q_ref[0, rows, c * LANES:(c + 1) * LANES]
            ps, ls = [], []
            for hh in range(2):
                s = _nt(qc, kvar[g][hh][win])
                s = jnp.where(valid, s, MASK_VALUE)
                sk = sink_ref[2 * c + hh]
                m = jnp.maximum(jnp.max(s, axis=1, keepdims=True), sk)
                p = jnp.exp(s - m)
                ls.append(jnp.sum(p, axis=1, keepdims=True) + jnp.exp(sk - m))
                ps.append(p.astype(BF16))
            pv = jnp.dot(jnp.concatenate(ps, axis=1),
                         jnp.concatenate([vvar[g][0][win], vvar[g][1][win]], axis=0),
                         preferred_element_type=F32)
            inv = jnp.where(lo, 1.0 / ls[0], 1.0 / ls[1])
            o_ref[0, rows, c * LANES:(c + 1) * LANES] = (pv * inv).astype(BF16)


def _attention(qa, ka, va, sink):
    B, L, _ = qa.shape
    tq = min(TQ_ATTN, L)
    r = tq // WINDOW
    nb = L // WINDOW
    kvw = 2 * ATTN_KV_W
    main = pl.BlockSpec((1, tq, kvw), lambda b, i: (b, i, 0))
    prev = pl.BlockSpec((1, WINDOW, kvw), lambda b, i: (b, jnp.maximum(i * r - 1, 0), 0))
    nxt = pl.BlockSpec((1, WINDOW, kvw), lambda b, i: (b, jnp.minimum(i * r + r, nb - 1), 0))
    return pl.pallas_call(
        functools.partial(_attn_body, tq=tq, seq=L),
        grid=(B, L // tq),
        in_specs=[
            pl.BlockSpec(memory_space=pltpu.SMEM),
            pl.BlockSpec((1, tq, ATTN_Q_W), lambda b, i: (b, i, 0)),
            prev, main, nxt, prev, main, nxt,
        ],
        out_specs=pl.BlockSpec((1, tq, ATTN_Q_W), lambda b, i: (b, i, 0)),
        out_shape=jax.ShapeDtypeStruct((B, L, ATTN_Q_W), BF16),
        compiler_params=pltpu.CompilerParams(
            dimension_semantics=("arbitrary", "arbitrary"), vmem_limit_bytes=VMEM_LIMIT),
        name="win_attn",
    )(sink, qa, ka, ka, ka, va, va, va)


HB = 128
LEAF = 16


def _hgrn_masks(reverse):
    r = lax.broadcasted_iota(I32, (HB, HB), 0)
    c = lax.broadcasted_iota(I32, (HB, HB), 1)
    if reverse:
        r, c = c, r
    tri = jnp.where(c <= r, 1.0, 0.0).astype(BF16)
    same = lambda s: (r >> s) == (c >> s)
    cross = lambda bit: ((r & bit) != 0) & ((c & bit) == 0)
    lev = jnp.where(same(4) & (c <= r), 0,
                    jnp.where(same(5) & cross(16), 1,
                              jnp.where(same(6) & cross(32), 2, 3)))
    return tri, lev


def _hgrn_gates(s, lb, lbc, tri):
    f = lbc + (1.0 - lb) * s
    logf = jnp.maximum(jnp.log(f), LOGF_MIN)
    kk = 1.0 - jnp.maximum(f, F_MIN)
    hi = logf.astype(BF16)
    r1 = logf - hi.astype(F32)
    mid = r1.astype(BF16)
    low = (r1 - mid.astype(F32)).astype(BF16)
    dot = functools.partial(jnp.dot, preferred_element_type=F32)
    return kk, (dot(tri, hi) + dot(tri, mid) + dot(tri, low)) * LOG2E


def _hgrn_scores(q, kk, cum, v, st_ref, reverse):
    def ref_row(idx):
        return cum[idx:idx + 1]

    total = ref_row(0) if reverse else ref_row(HB - 1)
    q_blk = q * jnp.exp2(cum)
    k_blk = kk * jnp.exp2(total - cum)

    ql, kl = [], []
    for c in range(HB // LEAF):
        rows = slice(c * LEAF, (c + 1) * LEAF)
        edge = (c + 1) * LEAF if reverse else c * LEAF - 1
        loc = cum[rows] - ref_row(edge) if 0 <= edge < HB else cum[rows]
        ql.append(q[rows] * jnp.exp2(loc))
        kl.append(kk[rows] * jnp.exp2(-loc))
    pairs = [(jnp.concatenate(ql, axis=0), jnp.concatenate(kl, axis=0))]
    for w in (16, 32, 64):
        qp, kp = [], []
        zeros = jnp.zeros((w, HB), F32)
        for s0 in range(0, HB, 2 * w):
            first, second = slice(s0, s0 + w), slice(s0 + w, s0 + 2 * w)
            if reverse:
                ref = ref_row(s0 + w)
                qp += [q[first] * jnp.exp2(cum[first] - ref), zeros]
                kp += [zeros, kk[second] * jnp.exp2(ref - cum[second])]
            else:
                ref = ref_row(s0 + w - 1)
                qp += [zeros, q[second] * jnp.exp2(cum[second] - ref)]
                kp += [kk[first] * jnp.exp2(ref - cum[first]), zeros]
        pairs.append((jnp.concatenate(qp, axis=0), jnp.concatenate(kp, axis=0)))
    scores = [_nt(qq.astype(BF16), kq.astype(BF16)) for qq, kq in pairs]
    st = st_ref[...]
    o_state = _nt(q_blk.astype(BF16), st.astype(BF16))
    st_ref[...] = st * jnp.exp2(total) + lax.dot_general(
        v, k_blk.astype(BF16), (((0,), (0,)), ((), ())), preferred_element_type=F32)
    return scores, o_state


def _hgrn_output(scores, o_state, v, lev):
    a0, a1, a2, a3 = scores
    a = jnp.where(lev == 0, a0, jnp.where(lev == 1, a1, jnp.where(lev == 2, a2, a3)))
    return jnp.dot(a.astype(BF16), v, preferred_element_type=F32) + o_state


def _hgrn_body(lb_ref, qf_ref, sf_ref, vf_ref, qr_ref, sr_ref, vr_ref, of_ref, or_ref, st_ref, *, th):
    @pl.when(pl.program_id(1) == 0)
    def _():
        st_ref[...] = jnp.zeros_like(st_ref)

    lb = lb_ref[...]
    lbc = jnp.maximum(lb, LB_FLOOR)
    masks = (_hgrn_masks(False), _hgrn_masks(True))
    nblk = th // HB
    streams = ((qf_ref, sf_ref, vf_ref, of_ref), (qr_ref, sr_ref, vr_ref, or_ref))

    def body(t, carry):
        rows = [pl.multiple_of((nblk - 1 - t if d else t) * HB, HB) for d in range(2)]
        units = [(d, slice(h * HGRN_DIM, (h + 1) * HGRN_DIM)) for d in range(2) for h in range(HGRN_HEADS)]
        gates = [_hgrn_gates(streams[d][1][0, pl.ds(rows[d], HB), sl].astype(F32),
                             lb[d:d + 1, sl], lbc[d:d + 1, sl], masks[d][0]) for d, sl in units]
        mids = [_hgrn_scores(streams[d][0][0, pl.ds(rows[d], HB), sl].astype(F32), kk, cum,
                             streams[d][2][0, pl.ds(rows[d], HB), sl], st_ref.at[d, sl.start // HGRN_DIM],
                             bool(d)) for (d, sl), (kk, cum) in zip(units, gates)]
        for (d, sl), (scores, o_state) in zip(units, mids):
            o = _hgrn_output(scores, o_state, streams[d][2][0, pl.ds(rows[d], HB), sl], masks[d][1])
            streams[d][3][0, pl.ds(rows[d], HB), sl] = o.astype(BF16)
        return carry

    lax.fori_loop(0, nblk, body, 0)


def _hgrn(qb, sf, sb, ib, lb):
    B, L, _ = qb.shape
    th = min(TH_HGRN, L)
    n = L // th
    fwd = pl.BlockSpec((1, th, HGRN_W), lambda b, i: (b, i, 0))
    rev = pl.BlockSpec((1, th, HGRN_W), lambda b, i: (b, n - 1 - i, 0))
    return pl.pallas_call(
        functools.partial(_hgrn_body, th=th),
        grid=(B, n),
        in_specs=[pl.BlockSpec((2, HGRN_W), lambda b, i: (0, 0)), fwd, fwd, fwd, rev, rev, rev],
        out_specs=[fwd, rev],
        out_shape=[jax.ShapeDtypeStruct((B, L, HGRN_W), BF16)] * 2,
        scratch_shapes=[pltpu.VMEM((2, HGRN_HEADS, HGRN_DIM, HGRN_DIM), F32)],
        compiler_params=pltpu.CompilerParams(
            dimension_semantics=("arbitrary", "arbitrary"), vmem_limit_bytes=VMEM_LIMIT),
        name="hgrn2",
    )(lb, qb, sf, ib, qb, sb, ib)


def _post_body(x_ref, ya_ref, of_ref, or_ref, og_ref, ga_ref, gb_ref, wua_ref, wuh_ref, wo_ref,
               hg_ref, fg_ref, rwh_ref, rwl_ref, rb_ref, ltri_ref,
               xn_ref, h2_ref, rt_ref, cnt_ref):
    dot = functools.partial(jnp.dot, preferred_element_type=F32)
    o = of_ref[...].astype(F32) + or_ref[...].astype(F32)
    yh = (_rms(o) * hg_ref[...] * og_ref[...].astype(F32)).astype(BF16)
    merged = (ga_ref[...].astype(F32) * dot(ya_ref[...], wua_ref[...])
              + gb_ref[...].astype(F32) * dot(yh, wuh_ref[...]))
    xn = x_ref[...] + dot(merged.astype(BF16), wo_ref[...])
    xn_ref[...] = xn
    h2 = _rms(xn) * fg_ref[...]
    hi = h2.astype(BF16)
    h2_ref[...] = hi

    lo = (h2 - hi.astype(F32)).astype(BF16)
    logits = dot(hi, rwh_ref[...]) + dot(lo, rwh_ref[...]) + dot(hi, rwl_ref[...]) + rb_ref[...]
    lane = lax.broadcasted_iota(I32, logits.shape, 1)
    lanef = lane.astype(F32)
    big = jnp.float32(1e9)
    ninf = jnp.float32(-jnp.inf)
    red = dict(axis=1, keepdims=True)
    gmask = lane < N_GROUPS
    mg = jnp.max(jnp.where(gmask, logits, ninf), **red)
    p_top = 1.0 / jnp.sum(jnp.where(gmask, jnp.exp(logits - mg), 0.0), **red)
    gi = jnp.min(jnp.where(gmask & (logits == mg), lanef, big), **red)
    emask = ((lane >= N_GROUPS) & (lane < N_GROUPS + N_EXPERTS)
             & (((lane - N_GROUPS) >> 3).astype(F32) == gi))
    le = jnp.where(emask, logits, ninf)
    m1 = jnp.max(le, **red)
    i1 = jnp.min(jnp.where(le == m1, lanef, big), **red)
    le2 = jnp.where(lanef == i1, ninf, le)
    m2 = jnp.max(le2, **red)
    i2 = jnp.min(jnp.where(le2 == m2, lanef, big), **red)
    r = jnp.exp(m2 - m1)
    w1 = 1.0 / (1.0 + r)
    w2 = r * w1
    e1 = i1 - N_GROUPS
    e2 = i2 - N_GROUPS
    is1, is2 = lanef == e1, lanef == e2
    onehot = jnp.where(is1 | is2, 1.0, 0.0)
    pref = dot(ltri_ref[...], onehot.astype(BF16))
    rank1 = jnp.sum(jnp.where(is1, pref, 0.0), **red)
    rank2 = jnp.sum(jnp.where(is2, pref, 0.0), **red)
    cnt_ref[...] = jnp.broadcast_to(jnp.sum(onehot, axis=0, keepdims=True), cnt_ref.shape)
    vals = (e1, e2, rank1, rank2, p_top * w1, p_top * w2)
    rt = jnp.zeros_like(logits)
    for idx, val in enumerate(vals):
        rt = jnp.where(lane == idx, val, rt)
    rt_ref[...] = rt


RT_E1, RT_E2, RT_RANK1, RT_RANK2, RT_W1, RT_W2 = range(6)
SUBLANES = 8


def _post(x, ya, of, orv, og, ga, gb, wua, wuh, wo, hg, fg, rwh, rwl, rb, ltri):
    T, D = x.shape
    tm = ltri.shape[0]
    row = lambda w: pl.BlockSpec((tm, w), lambda i: (i, 0))
    full = lambda a: pl.BlockSpec(a.shape, lambda i: (0,) * a.ndim)
    return pl.pallas_call(
        _post_body,
        grid=(T // tm,),
        in_specs=[row(D), row(ATTN_Q_W), row(HGRN_W), row(HGRN_W), row(HGRN_W), row(D), row(D),
                  full(wua), full(wuh), full(wo), full(hg), full(fg), full(rwh), full(rwl), full(rb),
                  full(ltri)],
        out_specs=[row(D), row(D), row(LANES), pl.BlockSpec((SUBLANES, LANES), lambda i: (i, 0))],
        out_shape=[jax.ShapeDtypeStruct((T, D), F32), jax.ShapeDtypeStruct((T, D), BF16),
                   jax.ShapeDtypeStruct((T, LANES), F32),
                   jax.ShapeDtypeStruct((T // tm * SUBLANES, LANES), F32)],
        compiler_params=pltpu.CompilerParams(
            dimension_semantics=("arbitrary",), vmem_limit_bytes=VMEM_LIMIT),
        name="post_mixer_router",
    )(x, ya, of, orv, og, ga, gb, wua, wuh, wo, hg, fg, rwh, rwl, rb, ltri)


XS_W = D_MODEL // 2 + LANES
SEG_BIG = 32
META_DST, META_N8, META_LOFF = 0, N_EXPERTS, 2 * N_EXPERTS


def _local_rows(tm):
    return 2 * tm + N_EXPERTS * SUBLANES


def _local_perm(rt, loff_row, rows):
    lanef = lax.broadcasted_iota(I32, rt.shape, 1).astype(F32)
    col = lax.broadcasted_iota(I32, (rt.shape[0], rows), 1)
    masks = []
    for e_lane, r_lane in ((RT_E1, RT_RANK1), (RT_E2, RT_RANK2)):
        e = rt[:, e_lane:e_lane + 1]
        off = jnp.sum(jnp.where(lanef == e, loff_row, 0.0), axis=1, keepdims=True)
        hit = col == (off + rt[:, r_lane:r_lane + 1]).astype(I32)
        masks.append(jnp.where(hit, 1.0, 0.0).astype(BF16))
    return masks


def _segment_copies(meta_ref, make_copy, action):
    def per_expert(e, c):
        dst = meta_ref[0, 0, META_DST + e]
        n8 = meta_ref[0, 0, META_N8 + e]
        loff = meta_ref[0, 0, META_LOFF + e]
        n_big = n8 // (SEG_BIG // SUBLANES)

        def big(k, c):
            o = k * SEG_BIG
            action(make_copy(pl.multiple_of(loff + o, SUBLANES), pl.multiple_of(dst + o, SUBLANES), SEG_BIG))
            return c

        lax.fori_loop(0, n_big, big, 0)

        def small(k, c):
            o = k * SUBLANES
            action(make_copy(pl.multiple_of(loff + o, SUBLANES), pl.multiple_of(dst + o, SUBLANES), SUBLANES))
            return c

        lax.fori_loop(n_big * (SEG_BIG // SUBLANES), n8, small, 0)
        return c

    lax.fori_loop(0, N_EXPERTS, per_expert, 0)


TAIL_PIECES = tuple(SUBLANES << b for b in range((MOE_BLK // SUBLANES).bit_length() - 1))


def _tail_copies(tail_ref, make_copy, action):
    def per_expert(e, c):
        start = tail_ref[0, 0, e]
        n8 = tail_ref[0, 0, N_EXPERTS + e]
        for rows in TAIL_PIECES:
            bit = rows // SUBLANES

            @pl.when((n8 & bit) != 0)
            def _():
                off = (n8 & ~(2 * bit - 1)) * SUBLANES
                action(make_copy(0, pl.multiple_of(start + off, SUBLANES), rows))
        return c

    lax.fori_loop(0, N_EXPERTS, per_expert, 0)


def _dispatch_body(meta_ref, tail_ref, loff_ref, rt_ref, h_ref, xs_ref, buf_ref, sem):
    tm = rt_ref.shape[0]
    rows = buf_ref.shape[0]
    rt = rt_ref[...]
    p1, p2 = _local_perm(rt, loff_ref[0], rows)
    tn = lambda a, b: lax.dot_general(a, b, (((0,), (0,)), ((), ())), preferred_element_type=F32)
    xl = tn(p1 + p2, h_ref[...])
    half = D_MODEL // 2
    buf_ref[:, :half] = _pack_pair(xl[:, :half], xl[:, half:])
    lane = lax.broadcasted_iota(I32, (tm, LANES), 1)

    def wcols(w):
        hi = w.astype(BF16).astype(F32)
        return jnp.where(lane == 0, hi, jnp.where(lane == 1, w - hi, 0.0)).astype(BF16)

    wl = tn(p1, wcols(rt[:, RT_W1:RT_W1 + 1])) + tn(p2, wcols(rt[:, RT_W2:RT_W2 + 1]))
    buf_ref[:, half:] = lax.bitcast_convert_type(wl, U32)

    def make_copy(src_row, dst_row, n):
        return pltpu.make_async_copy(buf_ref.at[pl.ds(src_row, n)], xs_ref.at[pl.ds(dst_row, n)], sem)

    _segment_copies(meta_ref, make_copy, lambda cp: cp.start())
    _segment_copies(meta_ref, make_copy, lambda cp: cp.wait())

    @pl.when(pl.program_id(0) == pl.num_programs(0) - 1)
    def _():
        buf_ref[:TAIL_PIECES[-1]] = jnp.zeros((TAIL_PIECES[-1], XS_W), U32)
        _tail_copies(tail_ref, make_copy, lambda cp: cp.start())
        _tail_copies(tail_ref, make_copy, lambda cp: cp.wait())


def _dispatch(meta, tail, loff, rt, h2, n_pad):
    T, D = h2.shape
    nt = meta.shape[0]
    tm = T // nt
    return pl.pallas_call(
        _dispatch_body,
        grid=(nt,),
        in_specs=[pl.BlockSpec((1, 1, meta.shape[2]), lambda i: (i, 0, 0), memory_space=pltpu.SMEM),
                  pl.BlockSpec((1, 1, tail.shape[2]), lambda i: (0, 0, 0), memory_space=pltpu.SMEM),
                  pl.BlockSpec((1, 1, LANES), lambda i: (i, 0, 0)),
                  pl.BlockSpec((tm, LANES), lambda i: (i, 0)),
                  pl.BlockSpec((tm, D), lambda i: (i, 0))],
        out_specs=pl.BlockSpec(memory_space=pl.ANY),
        out_shape=jax.ShapeDtypeStruct((n_pad, XS_W), U32),
        scratch_shapes=[pltpu.VMEM((_local_rows(tm), XS_W), U32), pltpu.SemaphoreType.DMA(())],
        compiler_params=pltpu.CompilerParams(
            dimension_semantics=("arbitrary",), vmem_limit_bytes=VMEM_LIMIT),
        name="moe_dispatch",
    )(meta, tail, loff, rt, h2)


def _expert_body(be_ref, nused_ref, xs_ref, wg_ref, wu_ref, wd_ref, ys_ref, wgb_ref, wub_ref, wdb_ref):
    i = pl.program_id(0)

    @pl.when((i == 0) | (be_ref[i] != be_ref[jnp.maximum(i - 1, 0)]))
    def _():
        wgb_ref[...] = wg_ref[0].astype(BF16)
        wub_ref[...] = wu_ref[0].astype(BF16)
        wdb_ref[...] = wd_ref[0].astype(BF16)

    @pl.when(i < nused_ref[0])
    def _():
        dot = functools.partial(jnp.dot, preferred_element_type=F32)
        half = D_MODEL // 2
        ha, hb = _unpack_pair(xs_ref[:, :half])
        h = jnp.concatenate([ha, hb], axis=1).astype(BF16)
        wl = lax.bitcast_convert_type(xs_ref[:, half:], F32)
        w = wl[:, 0:1] + wl[:, 1:2]
        g = dot(h, wgb_ref[...])
        u = dot(h, wub_ref[...])
        y = dot((g * _sigmoid(g) * u).astype(BF16), wdb_ref[...]) * w
        ys_ref[...] = _pack_pair(y[:, :half], y[:, half:])


def _experts(block_expert, nused, xs, wg, wu, wd):
    n_pad = xs.shape[0]
    blk = lambda w: pl.BlockSpec((MOE_BLK, w), lambda i, be, nu: (jnp.minimum(i, nu[0] - 1), 0))
    wspec = lambda a: pl.BlockSpec((1,) + a.shape[1:], lambda i, be, nu: (be[i], 0, 0))
    return pl.pallas_call(
        _expert_body,
        grid_spec=pltpu.PrefetchScalarGridSpec(
            num_scalar_prefetch=2,
            grid=(n_pad // MOE_BLK,),
            in_specs=[blk(XS_W), wspec(wg), wspec(wu), wspec(wd)],
            out_specs=blk(D_MODEL // 2),
            scratch_shapes=[pltpu.VMEM(wg.shape[1:], BF16), pltpu.VMEM(wu.shape[1:], BF16),
                            pltpu.VMEM(wd.shape[1:], BF16)],
        ),
        out_shape=jax.ShapeDtypeStruct((n_pad, D_MODEL // 2), U32),
        compiler_params=pltpu.CompilerParams(
            dimension_semantics=("arbitrary",), vmem_limit_bytes=VMEM_LIMIT),
        name="moe_experts",
    )(block_expert, nused, xs, wg, wu, wd)


def _combine_body(meta_ref, loff_ref, rt_ref, x_ref, fg_ref, ys_ref, o_ref, buf_ref, sem, *, final_norm):
    @pl.when(pl.program_id(0) == 0)
    def _():
        buf_ref[...] = jnp.zeros_like(buf_ref)

    def make_copy(loc_row, glob_row, n):
        return pltpu.make_async_copy(ys_ref.at[pl.ds(glob_row, n)], buf_ref.at[pl.ds(loc_row, n)], sem)

    _segment_copies(meta_ref, make_copy, lambda cp: cp.start())
    p1, p2 = _local_perm(rt_ref[...], loff_ref[0], buf_ref.shape[0])
    pt = p1 + p2
    _segment_copies(meta_ref, make_copy, lambda cp: cp.wait())
    yl = jnp.concatenate(_unpack_pair(buf_ref[...]), axis=1).astype(BF16)
    out = x_ref[...] + jnp.dot(pt, yl, preferred_element_type=F32)
    if final_norm:
        out = _rms(out) * fg_ref[...]
    o_ref[...] = out


def _combine(meta, loff, rt, x, fg, ys, final_norm):
    T, D = x.shape
    nt = meta.shape[0]
    tm = T // nt
    return pl.pallas_call(
        functools.partial(_combine_body, final_norm=final_norm),
        grid=(nt,),
        in_specs=[pl.BlockSpec((1, 1, meta.shape[2]), lambda i: (i, 0, 0), memory_space=pltpu.SMEM),
                  pl.BlockSpec((1, 1, LANES), lambda i: (i, 0, 0)),
                  pl.BlockSpec((tm, LANES), lambda i: (i, 0)),
                  pl.BlockSpec((tm, D), lambda i: (i, 0)),
                  pl.BlockSpec((1, D), lambda i: (0, 0)),
                  pl.BlockSpec(memory_space=pl.ANY)],
        out_specs=pl.BlockSpec((tm, D), lambda i: (i, 0)),
        out_shape=jax.ShapeDtypeStruct((T, D), F32),
        scratch_shapes=[pltpu.VMEM((_local_rows(tm), D // 2), U32), pltpu.SemaphoreType.DMA(())],
        compiler_params=pltpu.CompilerParams(
            dimension_semantics=("arbitrary",), vmem_limit_bytes=VMEM_LIMIT),
        name="moe_combine",
    )(meta, loff, rt, x, fg, ys)


def _rope_tables(L):
    half = HEAD_DIM // 2
    inv = ROPE_THETA ** (-jnp.arange(half, dtype=F32) / half)
    ang = jnp.arange(L, dtype=F32)[:, None] * inv[None, :]
    lane = np.arange(LANES)
    idx = lane % half
    first = jnp.asarray((lane % HEAD_DIM) < half)
    cos = jnp.cos(ang)[:, idx]
    sin = jnp.sin(ang)[:, idx]
    return cos, jnp.where(first, -sin, 0.0), jnp.where(first, 0.0, sin)


def _moe_layout(cnt, T, layer):
    nt = cnt.shape[0] // SUBLANES
    c = cnt.reshape(nt, SUBLANES, LANES)[:, 0, :N_EXPERTS].astype(I32)
    c8 = (c + SUBLANES - 1) // SUBLANES * SUBLANES
    tile_off = jnp.cumsum(c8, axis=0) - c8
    total = jnp.sum(c8, axis=0)
    padded = (total + MOE_BLK - 1) // MOE_BLK * MOE_BLK
    pend = jnp.cumsum(padded)
    dst = (pend - padded)[None, :] + tile_off
    loff = jnp.cumsum(c8, axis=1) - c8
    meta = jnp.concatenate([dst, c8 // SUBLANES, loff], axis=1).reshape(nt, 1, 3 * N_EXPERTS)
    tail = jnp.concatenate([pend - padded + total, (padded - total) // SUBLANES]).reshape(1, 1, 2 * N_EXPERTS)
    loff_f = jnp.pad(loff.astype(F32), ((0, 0), (0, LANES - N_EXPERTS))).reshape(nt, 1, LANES)
    n_pad = 2 * T + nt * N_EXPERTS * SUBLANES + N_EXPERTS * MOE_BLK
    n_pad = (n_pad + MOE_BLK - 1) // MOE_BLK * MOE_BLK
    starts = jnp.arange(n_pad // MOE_BLK, dtype=I32) * MOE_BLK
    block_expert = jnp.minimum(jnp.sum(starts[:, None] >= pend[None, :], axis=1), N_EXPERTS - 1)
    block_expert = block_expert.astype(I32) + layer * N_EXPERTS
    nused = (pend[-1:] // MOE_BLK).astype(I32)
    return meta, tail, loff_f, block_expert, nused, n_pad


def kernel(x, attn_norm_g, w_in, sink_logits, hgrn_lb_table, hgrn_norm_g, w_up_attn, w_up_hgrn, w_out,
           ffn_norm_g, w_router_group, b_router_group, w_router_expert, b_router_expert,
           w_expert_gate, w_expert_up, w_expert_down, final_norm_g):
    B, L, D = x.shape
    T = B * L
    depth = w_in.shape[0]
    sm = jax.nn.softmax(hgrn_lb_table.astype(F32), axis=0)
    lower_bounds = jnp.cumsum(sm, axis=0) - sm[:1]
    cos, s1, s2 = _rope_tables(L)
    tm_post = min(TM_POST, T)
    ltri = jnp.tril(jnp.ones((tm_post, tm_post), BF16), -1)
    fin_g = final_norm_g.reshape(1, D).astype(F32)
    w_in_bf16 = w_in.astype(BF16)
    all_layers = lambda w: w.reshape((depth * N_EXPERTS,) + w.shape[2:])
    wg_all, wu_all, wd_all = all_layers(w_expert_gate), all_layers(w_expert_up), all_layers(w_expert_down)

    for l in range(depth):
        qa, ka, va, qb, sf, sb, ib, og, ga, gb = _inproj(
            x, attn_norm_g[l].reshape(1, D), w_in_bf16, l, cos, s1, s2)
        ya = _attention(qa, ka, va, sink_logits[l].astype(F32))
        of, orv = _hgrn(qb, sf, sb, ib, lower_bounds[l])

        rw = jnp.concatenate(
            [w_router_group[l], jnp.transpose(w_router_expert[l], (1, 0, 2)).reshape(D, N_EXPERTS)], axis=1)
        rw = jnp.pad(rw.astype(F32), ((0, 0), (0, LANES - rw.shape[1])))
        rwh = rw.astype(BF16)
        rwl = (rw - rwh.astype(F32)).astype(BF16)
        rb = jnp.concatenate([b_router_group[l], b_router_expert[l].reshape(-1)]).astype(F32)
        rb = jnp.pad(rb, (0, LANES - rb.shape[0])).reshape(1, LANES)
        flat = lambda a: a.reshape(T, a.shape[-1])
        xn, h2, rt, cnt = _post(
            flat(x), flat(ya), flat(of), flat(orv), flat(og), flat(ga), flat(gb),
            w_up_attn[l].astype(BF16), w_up_hgrn[l].astype(BF16), w_out[l].astype(BF16),
            hgrn_norm_g[l].reshape(1, HGRN_W).astype(F32), ffn_norm_g[l].reshape(1, D).astype(F32),
            rwh, rwl, rb, ltri)

        meta, tail, loff, block_expert, nused, n_pad = _moe_layout(cnt, T, l)
        xs = _dispatch(meta, tail, loff, rt, h2, n_pad)
        ys = _experts(block_expert, nused, xs, wg_all, wu_all, wd_all)
        x = _combine(meta, loff, rt, xn, fin_g, ys, l == depth - 1).reshape(B, L, D)
    return x
```

```python
import functools

import numpy as np
import jax
import jax.numpy as jnp
from jax import lax
from jax.experimental import pallas as pl
from jax.experimental.pallas import tpu as pltpu

F32 = jnp.float32
BF16 = jnp.bfloat16
U32 = jnp.uint32
I32 = jnp.int32

D_MODEL = 1024
N_Q_HEADS = 8
N_KV_HEADS = 2
HEAD_DIM = 64
WINDOW = 128
ROPE_THETA = 10000.0
MASK_VALUE = -1e30
HGRN_HEADS = 4
HGRN_DIM = 128
LOGF_MIN = -4.0
F_MIN = float(np.exp(LOGF_MIN))
LOG2E = float(np.log2(np.e))
LB_FLOOR = 1e-30
N_GROUPS = 4
EXPERTS_PER_GROUP = 8
N_EXPERTS = N_GROUPS * EXPERTS_PER_GROUP
D_EXPERT = 512
NORM_EPS = 1e-6

ATTN_Q_W = N_Q_HEADS * HEAD_DIM
ATTN_KV_W = N_KV_HEADS * HEAD_DIM
HGRN_W = HGRN_HEADS * HGRN_DIM
OFF_QA = 0
OFF_KA = OFF_QA + ATTN_Q_W
OFF_VA = OFF_KA + ATTN_KV_W
OFF_QB = OFF_VA + ATTN_KV_W
OFF_ZF = OFF_QB + HGRN_W
OFF_ZB = OFF_ZF + HGRN_W
OFF_IB = OFF_ZB + HGRN_W
OFF_OG = OFF_IB + HGRN_W
OFF_GA = OFF_OG + HGRN_W
OFF_GB = OFF_GA + D_MODEL

LANES = 128
VMEM_LIMIT = 56 * 1024 * 1024

TM_PROJ = 512
TQ_ATTN = 512
TH_HGRN = 512
TM_POST = 512
MOE_BLK = 256


def _sigmoid(z):
    return 1.0 / (1.0 + jnp.exp(-z))


def _rms(x):
    return x * lax.rsqrt(jnp.mean(x * x, axis=-1, keepdims=True) + NORM_EPS)


def _nt(a, b):
    return lax.dot_general(a, b, (((1,), (1,)), ((), ())), preferred_element_type=F32)


def _pack_pair(a, b):
    ua = lax.bitcast_convert_type(a.astype(BF16).astype(F32), U32)
    ub = lax.bitcast_convert_type(b.astype(BF16).astype(F32), U32)
    return ua | (ub >> 16)


def _unpack_pair(u):
    hi = lax.bitcast_convert_type(u & jnp.uint32(0xFFFF0000), F32)
    lo = lax.bitcast_convert_type(u << 16, F32)
    return hi, lo


def _inproj_body(x_ref, g_ref, w_ref, cos_ref, s1_ref, s2_ref,
                 qa_ref, ka_ref, va_ref, qb_ref, sf_ref, sb_ref, ib_ref, og_ref, ga_ref, gb_ref):
    h = (_rms(x_ref[0]) * g_ref[...]).astype(BF16)

    def proj(off, width):
        return jnp.dot(h, w_ref[:, off:off + width], preferred_element_type=F32)

    cos, s1, s2 = cos_ref[...], s1_ref[...], s2_ref[...]

    def rope(t):
        return t * cos + pltpu.roll(t, 96, 1) * s1 + pltpu.roll(t, 32, 1) * s2

    q = proj(OFF_QA, ATTN_Q_W)
    for c in range(ATTN_Q_W // LANES):
        sl = slice(c * LANES, (c + 1) * LANES)
        qa_ref[0, :, sl] = (rope(q[:, sl]) * (HEAD_DIM ** -0.5)).astype(BF16)
    k = rope(proj(OFF_KA, ATTN_KV_W))
    ka_ref[0, :, :LANES] = k.astype(BF16)
    ka_ref[0, :, LANES:] = pltpu.roll(k, 64, 1).astype(BF16)
    v = proj(OFF_VA, ATTN_KV_W)
    va_ref[0, :, :LANES] = v.astype(BF16)
    va_ref[0, :, LANES:] = pltpu.roll(v, 64, 1).astype(BF16)
    t = proj(OFF_QB, HGRN_W)
    qb_ref[0] = (t * _sigmoid(t)).astype(BF16)
    sf_ref[0] = _sigmoid(proj(OFF_ZF, HGRN_W)).astype(BF16)
    sb_ref[0] = _sigmoid(proj(OFF_ZB, HGRN_W)).astype(BF16)
    ib_ref[0] = proj(OFF_IB, HGRN_W).astype(BF16)
    t = proj(OFF_OG, HGRN_W)
    og_ref[0] = (t * _sigmoid(t)).astype(BF16)
    ga_ref[0] = _sigmoid(proj(OFF_GA, D_MODEL)).astype(BF16)
    gb_ref[0] = _sigmoid(proj(OFF_GB, D_MODEL)).astype(BF16)


def _inproj(x, g, w_all, layer, cos, s1, s2):
    B, L, D = x.shape
    tm = min(TM_PROJ, L)
    widths = (ATTN_Q_W, 2 * ATTN_KV_W, 2 * ATTN_KV_W, HGRN_W, HGRN_W, HGRN_W, HGRN_W, HGRN_W,
              D_MODEL, D_MODEL)
    tab = pl.BlockSpec((tm, LANES), lambda b, i: (i, 0))
    return pl.pallas_call(
        _inproj_body,
        grid=(B, L // tm),
        in_specs=[
            pl.BlockSpec((1, tm, D), lambda b, i: (b, i, 0)),
            pl.BlockSpec((1, D), lambda b, i: (0, 0)),
            pl.BlockSpec((None,) + w_all.shape[1:], lambda b, i: (layer, 0, 0)),
            tab, tab, tab,
        ],
        out_specs=[pl.BlockSpec((1, tm, w), lambda b, i: (b, i, 0)) for w in widths],
        out_shape=[jax.ShapeDtypeStruct((B, L, w), BF16) for w in widths],
        compiler_params=pltpu.CompilerParams(
            dimension_semantics=("arbitrary", "arbitrary"), vmem_limit_bytes=VMEM_LIMIT),
        name="inproj",
    )(x, g, w_all, cos, s1, s2)


def _attn_body(sink_ref, q_ref, kp_ref, km_ref, kn_ref, vp_ref, vm_ref, vn_ref, o_ref, *, tq, seq):
    i = pl.program_id(1)
    kw = jnp.concatenate([kp_ref[0], km_ref[0], kn_ref[0]], axis=0)
    vw = jnp.concatenate([vp_ref[0], vm_ref[0], vn_ref[0]], axis=0)
    lo = lax.broadcasted_iota(I32, (1, LANES), 1) < HEAD_DIM
    zero = jnp.zeros((), BF16)

    def variants(t):
        a, b = t[:, :LANES], t[:, LANES:]
        return ((jnp.where(lo, a, zero), jnp.where(lo, zero, b)),
                (jnp.where(lo, b, zero), jnp.where(lo, zero, a)))

    kvar, vvar = variants(kw), variants(vw)
    row = lax.broadcasted_iota(I32, (WINDOW, 3 * WINDOW), 0)
    col = lax.broadcasted_iota(I32, (WINDOW, 3 * WINDOW), 1)
    band = (col >= row) & (col <= row + 2 * WINDOW)
    for j in range(tq // WINDOW):
        kpos = col + (i * tq + (j - 1) * WINDOW)
        valid = band & (kpos >= 0) & (kpos < seq)
        rows = slice(j * WINDOW, (j + 1) * WINDOW)
        win = slice(j * WINDOW, (j + 3) * WINDOW)
        for c in range(ATTN_Q_W // LANES):
            g = (2 * c) // (N_Q_HEADS // N_KV_HEADS)
            qc = q_ref[0, rows, c * LANES:(c + 1) * LANES]
            ps, ls = [], []
            for hh in range(2):
                s = _nt(qc, kvar[g][hh][win])
                s = jnp.where(valid, s, MASK_VALUE)
                sk = sink_ref[2 * c + hh]
                m = jnp.maximum(jnp.max(s, axis=1, keepdims=True), sk)
                p = jnp.exp(s - m)
                ls.append(jnp.sum(p, axis=1, keepdims=True) + jnp.exp(sk - m))
                ps.append(p.astype(BF16))
            pv = jnp.dot(jnp.concatenate(ps, axis=1),
                         jnp.concatenate([vvar[g][0][win], vvar[g][1][win]], axis=0),
                         preferred_element_type=F32)
            inv = jnp.where(lo, 1.0 / ls[0], 1.0 / ls[1])
            o_ref[0, rows, c * LANES:(c + 1) * LANES] = (pv * inv).astype(BF16)


def _attention(qa, ka, va, sink):
    B, L, _ = qa.shape
    tq = min(TQ_ATTN, L)
    r = tq // WINDOW
    nb = L // WINDOW
    kvw = 2 * ATTN_KV_W
    main = pl.BlockSpec((1, tq, kvw), lambda b, i: (b, i, 0))
    prev = pl.BlockSpec((1, WINDOW, kvw), lambda b, i: (b, jnp.maximum(i * r - 1, 0), 0))
    nxt = pl.BlockSpec((1, WINDOW, kvw), lambda b, i: (b, jnp.minimum(i * r + r, nb - 1), 0))
    return pl.pallas_call(
        functools.partial(_attn_body, tq=tq, seq=L),
        grid=(B, L // tq),
        in_specs=[
            pl.BlockSpec(memory_space=pltpu.SMEM),
            pl.BlockSpec((1, tq, ATTN_Q_W), lambda b, i: (b, i, 0)),
            prev, main, nxt, prev, main, nxt,
        ],
        out_specs=pl.BlockSpec((1, tq, ATTN_Q_W), lambda b, i: (b, i, 0)),
        out_shape=jax.ShapeDtypeStruct((B, L, ATTN_Q_W), BF16),
        compiler_params=pltpu.CompilerParams(
            dimension_semantics=("arbitrary", "arbitrary"), vmem_limit_bytes=VMEM_LIMIT),
        name="win_attn",
    )(sink, qa, ka, ka, ka, va, va, va)


HB = 128
LEAF = 16


def _hgrn_masks(reverse):
    r = lax.broadcasted_iota(I32, (HB, HB), 0)
    c = lax.broadcasted_iota(I32, (HB, HB), 1)
    if reverse:
        r, c = c, r
    tri = jnp.where(c <= r, 1.0, 0.0).astype(BF16)
    same = lambda s: (r >> s) == (c >> s)
    cross = lambda bit: ((r & bit) != 0) & ((c & bit) == 0)
    lev = jnp.where(same(4) & (c <= r), 0,
                    jnp.where(same(5) & cross(16), 1,
                              jnp.where(same(6) & cross(32), 2, 3)))
    return tri, lev


def _hgrn_gates(s, lb, lbc, tri):
    f = lbc + (1.0 - lb) * s
    logf = jnp.maximum(jnp.log(f), LOGF_MIN)
    kk = 1.0 - jnp.maximum(f, F_MIN)
    hi = logf.astype(BF16)
    r1 = logf - hi.astype(F32)
    mid = r1.astype(BF16)
    low = (r1 - mid.astype(F32)).astype(BF16)
    dot = functools.partial(jnp.dot, preferred_element_type=F32)
    return kk, (dot(tri, hi) + dot(tri, mid) + dot(tri, low)) * LOG2E


def _hgrn_scores(q, kk, cum, v, st_ref, reverse):
    def ref_row(idx):
        return cum[idx:idx + 1]

    total = ref_row(0) if reverse else ref_row(HB - 1)
    q_blk = q * jnp.exp2(cum)
    k_blk = kk * jnp.exp2(total - cum)

    ql, kl = [], []
    for c in range(HB // LEAF):
        rows = slice(c * LEAF, (c + 1) * LEAF)
        edge = (c + 1) * LEAF if reverse else c * LEAF - 1
        loc = cum[rows] - ref_row(edge) if 0 <= edge < HB else cum[rows]
        ql.append(q[rows] * jnp.exp2(loc))
        kl.append(kk[rows] * jnp.exp2(-loc))
    pairs = [(jnp.concatenate(ql, axis=0), jnp.concatenate(kl, axis=0))]
    for w in (16, 32, 64):
        qp, kp = [], []
        zeros = jnp.zeros((w, HB), F32)
        for s0 in range(0, HB, 2 * w):
            first, second = slice(s0, s0 + w), slice(s0 + w, s0 + 2 * w)
            if reverse:
                ref = ref_row(s0 + w)
                qp += [q[first] * jnp.exp2(cum[first] - ref), zeros]
                kp += [zeros, kk[second] * jnp.exp2(ref - cum[second])]
            else:
                ref = ref_row(s0 + w - 1)
                qp += [zeros, q[second] * jnp.exp2(cum[second] - ref)]
                kp += [kk[first] * jnp.exp2(ref - cum[first]), zeros]
        pairs.append((jnp.concatenate(qp, axis=0), jnp.concatenate(kp, axis=0)))
    scores = [_nt(qq.astype(BF16), kq.astype(BF16)) for qq, kq in pairs]
    st = st_ref[...]
    o_state = _nt(q_blk.astype(BF16), st.astype(BF16))
    st_ref[...] = st * jnp.exp2(total) + lax.dot_general(
        v, k_blk.astype(BF16), (((0,), (0,)), ((), ())), preferred_element_type=F32)
    return scores, o_state


def _hgrn_output(scores, o_state, v, lev):
    a0, a1, a2, a3 = scores
    a = jnp.where(lev == 0, a0, jnp.where(lev == 1, a1, jnp.where(lev == 2, a2, a3)))
    return jnp.dot(a.astype(BF16), v, preferred_element_type=F32) + o_state


def _hgrn_body(lb_ref, qf_ref, sf_ref, vf_ref, qr_ref, sr_ref, vr_ref, of_ref, or_ref, st_ref, *, th):
    @pl.when(pl.program_id(1) == 0)
    def _():
        st_ref[...] = jnp.zeros_like(st_ref)

    lb = lb_ref[...]
    lbc = jnp.maximum(lb, LB_FLOOR)
    masks = (_hgrn_masks(False), _hgrn_masks(True))
    nblk = th // HB
    streams = ((qf_ref, sf_ref, vf_ref, of_ref), (qr_ref, sr_ref, vr_ref, or_ref))

    def body(t, carry):
        rows = [pl.multiple_of((nblk - 1 - t if d else t) * HB, HB) for d in range(2)]
        units = [(d, slice(h * HGRN_DIM, (h + 1) * HGRN_DIM)) for d in range(2) for h in range(HGRN_HEADS)]
        gates = [_hgrn_gates(streams[d][1][0, pl.ds(rows[d], HB), sl].astype(F32),
                             lb[d:d + 1, sl], lbc[d:d + 1, sl], masks[d][0]) for d, sl in units]
        mids = [_hgrn_scores(streams[d][0][0, pl.ds(rows[d], HB), sl].astype(F32), kk, cum,
                             streams[d][2][0, pl.ds(rows[d], HB), sl], st_ref.at[d, sl.start // HGRN_DIM],
                             bool(d)) for (d, sl), (kk, cum) in zip(units, gates)]
        for (d, sl), (scores, o_state) in zip(units, mids):
            o = _hgrn_output(scores, o_state, streams[d][2][0, pl.ds(rows[d], HB), sl], masks[d][1])
            streams[d][3][0, pl.ds(rows[d], HB), sl] = o.astype(BF16)
        return carry

    lax.fori_loop(0, nblk, body, 0)


def _hgrn(qb, sf, sb, ib, lb):
    B, L, _ = qb.shape
    th = min(TH_HGRN, L)
    n = L // th
    fwd = pl.BlockSpec((1, th, HGRN_W), lambda b, i: (b, i, 0))
    rev = pl.BlockSpec((1, th, HGRN_W), lambda b, i: (b, n - 1 - i, 0))
    return pl.pallas_call(
        functools.partial(_hgrn_body, th=th),
        grid=(B, n),
        in_specs=[pl.BlockSpec((2, HGRN_W), lambda b, i: (0, 0)), fwd, fwd, fwd, rev, rev, rev],
        out_specs=[fwd, rev],
        out_shape=[jax.ShapeDtypeStruct((B, L, HGRN_W), BF16)] * 2,
        scratch_shapes=[pltpu.VMEM((2, HGRN_HEADS, HGRN_DIM, HGRN_DIM), F32)],
        compiler_params=pltpu.CompilerParams(
            dimension_semantics=("arbitrary", "arbitrary"), vmem_limit_bytes=VMEM_LIMIT),
        name="hgrn2",
    )(lb, qb, sf, ib, qb, sb, ib)


def _post_body(x_ref, ya_ref, of_ref, or_ref, og_ref, ga_ref, gb_ref, wua_ref, wuh_ref, wo_ref,
               hg_ref, fg_ref, rwh_ref, rwl_ref, rb_ref, ltri_ref,
               xn_ref, h2_ref, rt_ref, cnt_ref):
    dot = functools.partial(jnp.dot, preferred_element_type=F32)
    o = of_ref[...].astype(F32) + or_ref[...].astype(F32)
    yh = (_rms(o) * hg_ref[...] * og_ref[...].astype(F32)).astype(BF16)
    merged = (ga_ref[...].astype(F32) * dot(ya_ref[...], wua_ref[...])
              + gb_ref[...].astype(F32) * dot(yh, wuh_ref[...]))
    xn = x_ref[...] + dot(merged.astype(BF16), wo_ref[...])
    xn_ref[...] = xn
    h2 = _rms(xn) * fg_ref[...]
    hi = h2.astype(BF16)
    h2_ref[...] = hi

    lo = (h2 - hi.astype(F32)).astype(BF16)
    logits = dot(hi, rwh_ref[...]) + dot(lo, rwh_ref[...]) + dot(hi, rwl_ref[...]) + rb_ref[...]
    lane = lax.broadcasted_iota(I32, logits.shape, 1)
    lanef = lane.astype(F32)
    big = jnp.float32(1e9)
    ninf = jnp.float32(-jnp.inf)
    red = dict(axis=1, keepdims=True)
    gmask = lane < N_GROUPS
    mg = jnp.max(jnp.where(gmask, logits, ninf), **red)
    p_top = 1.0 / jnp.sum(jnp.where(gmask, jnp.exp(logits - mg), 0.0), **red)
    gi = jnp.min(jnp.where(gmask & (logits == mg), lanef, big), **red)
    emask = ((lane >= N_GROUPS) & (lane < N_GROUPS + N_EXPERTS)
             & (((lane - N_GROUPS) >> 3).astype(F32) == gi))
    le = jnp.where(emask, logits, ninf)
    m1 = jnp.max(le, **red)
    i1 = jnp.min(jnp.where(le == m1, lanef, big), **red)
    le2 = jnp.where(lanef == i1, ninf, le)
    m2 = jnp.max(le2, **red)
    i2 = jnp.min(jnp.where(le2 == m2, lanef, big), **red)
    r = jnp.exp(m2 - m1)
    w1 = 1.0 / (1.0 + r)
    w2 = r * w1
    e1 = i1 - N_GROUPS
    e2 = i2 - N_GROUPS
    is1, is2 = lanef == e1, lanef == e2
    onehot = jnp.where(is1 | is2, 1.0, 0.0)
    pref = dot(ltri_ref[...], onehot.astype(BF16))
    rank1 = jnp.sum(jnp.where(is1, pref, 0.0), **red)
    rank2 = jnp.sum(jnp.where(is2, pref, 0.0), **red)
    cnt_ref[...] = jnp.broadcast_to(jnp.sum(onehot, axis=0, keepdims=True), cnt_ref.shape)
    vals = (e1, e2, rank1, rank2, p_top * w1, p_top * w2)
    rt = jnp.zeros_like(logits)
    for idx, val in enumerate(vals):
        rt = jnp.where(lane == idx, val, rt)
    rt_ref[...] = rt


RT_E1, RT_E2, RT_RANK1, RT_RANK2, RT_W1, RT_W2 = range(6)
SUBLANES = 8


def _post(x, ya, of, orv, og, ga, gb, wua, wuh, wo, hg, fg, rwh, rwl, rb, ltri):
    T, D = x.shape
    tm = ltri.shape[0]
    row = lambda w: pl.BlockSpec((tm, w), lambda i: (i, 0))
    full = lambda a: pl.BlockSpec(a.shape, lambda i: (0,) * a.ndim)
    return pl.pallas_call(
        _post_body,
        grid=(T // tm,),
        in_specs=[row(D), row(ATTN_Q_W), row(HGRN_W), row(HGRN_W), row(HGRN_W), row(D), row(D),
                  full(wua), full(wuh), full(wo), full(hg), full(fg), full(rwh), full(rwl), full(rb),
                  full(ltri)],
        out_specs=[row(D), row(D), row(LANES), pl.BlockSpec((SUBLANES, LANES), lambda i: (i, 0))],
        out_shape=[jax.ShapeDtypeStruct((T, D), F32), jax.ShapeDtypeStruct((T, D), BF16),
                   jax.ShapeDtypeStruct((T, LANES), F32),
                   jax.ShapeDtypeStruct((T // tm * SUBLANES, LANES), F32)],
        compiler_params=pltpu.CompilerParams(
            dimension_semantics=("arbitrary",), vmem_limit_bytes=VMEM_LIMIT),
        name="post_mixer_router",
    )(x, ya, of, orv, og, ga, gb, wua, wuh, wo, hg, fg, rwh, rwl, rb, ltri)


XS_W = D_MODEL // 2 + LANES
SEG_BIG = 32
META_DST, META_N8, META_LOFF = 0, N_EXPERTS, 2 * N_EXPERTS


def _local_rows(tm):
    return 2 * tm + N_EXPERTS * SUBLANES


def _local_perm(rt, loff_row, rows):
    lanef = lax.broadcasted_iota(I32, rt.shape, 1).astype(F32)
    col = lax.broadcasted_iota(I32, (rt.shape[0], rows), 1)
    masks = []
    for e_lane, r_lane in ((RT_E1, RT_RANK1), (RT_E2, RT_RANK2)):
        e = rt[:, e_lane:e_lane + 1]
        off = jnp.sum(jnp.where(lanef == e, loff_row, 0.0), axis=1, keepdims=True)
        hit = col == (off + rt[:, r_lane:r_lane + 1]).astype(I32)
        masks.append(jnp.where(hit, 1.0, 0.0).astype(BF16))
    return masks


def _segment_copies(meta_ref, make_copy, action):
    def per_expert(e, c):
        dst = meta_ref[0, 0, META_DST + e]
        n8 = meta_ref[0, 0, META_N8 + e]
        loff = meta_ref[0, 0, META_LOFF + e]
        n_big = n8 // (SEG_BIG // SUBLANES)

        def big(k, c):
            o = k * SEG_BIG
            action(make_copy(pl.multiple_of(loff + o, SUBLANES), pl.multiple_of(dst + o, SUBLANES), SEG_BIG))
            return c

        lax.fori_loop(0, n_big, big, 0)

        def small(k, c):
            o = k * SUBLANES
            action(make_copy(pl.multiple_of(loff + o, SUBLANES), pl.multiple_of(dst + o, SUBLANES), SUBLANES))
            return c

        lax.fori_loop(n_big * (SEG_BIG // SUBLANES), n8, small, 0)
        return c

    lax.fori_loop(0, N_EXPERTS, per_expert, 0)


TAIL_PIECES = tuple(SUBLANES << b for b in range((MOE_BLK // SUBLANES).bit_length() - 1))


def _tail_copies(tail_ref, make_copy, action):
    def per_expert(e, c):
        start = tail_ref[0, 0, e]
        n8 = tail_ref[0, 0, N_EXPERTS + e]
        for rows in TAIL_PIECES:
            bit = rows // SUBLANES

            @pl.when((n8 & bit) != 0)
            def _():
                off = (n8 & ~(2 * bit - 1)) * SUBLANES
                action(make_copy(0, pl.multiple_of(start + off, SUBLANES), rows))
        return c

    lax.fori_loop(0, N_EXPERTS, per_expert, 0)

    def per_block(b, c):
        action(make_copy(0, pl.multiple_of(b * MOE_BLK, MOE_BLK), MOE_BLK))
        return c

    lax.fori_loop(tail_ref[0, 0, 2 * N_EXPERTS], tail_ref[0, 0, 2 * N_EXPERTS + 1], per_block, 0)


def _dispatch_body(meta_ref, prev_meta_ref, tail_ref, loff_ref, rt_ref, h_ref, xs_ref, buf_ref, sem):
    i = pl.program_id(0)
    last = pl.num_programs(0) - 1
    slot = i % 2
    tm = rt_ref.shape[0]
    rows = buf_ref.shape[1]
    rt = rt_ref[...]
    p1, p2 = _local_perm(rt, loff_ref[0], rows)
    tn = lambda a, b: lax.dot_general(a, b, (((0,), (0,)), ((), ())), preferred_element_type=F32)
    xl = tn(p1 + p2, h_ref[...])
    half = D_MODEL // 2
    buf_ref[slot, :, :half] = _pack_pair(xl[:, :half], xl[:, half:])
    lane = lax.broadcasted_iota(I32, (tm, LANES), 1)

    def wcols(w):
        hi = w.astype(BF16).astype(F32)
        return jnp.where(lane == 0, hi, jnp.where(lane == 1, w - hi, 0.0)).astype(BF16)

    wl = tn(p1, wcols(rt[:, RT_W1:RT_W1 + 1])) + tn(p2, wcols(rt[:, RT_W2:RT_W2 + 1]))
    buf_ref[slot, :, half:] = lax.bitcast_convert_type(wl, U32)

    def copier(k):
        def make_copy(src_row, dst_row, n):
            return pltpu.make_async_copy(buf_ref.at[k, pl.ds(src_row, n)], xs_ref.at[pl.ds(dst_row, n)],
                                         sem.at[k])
        return make_copy

    _segment_copies(meta_ref, copier(slot), lambda cp: cp.start())

    @pl.when(i > 0)
    def _():
        _segment_copies(prev_meta_ref, copier(1 - slot), lambda cp: cp.wait())

    @pl.when(i == last)
    def _():
        _segment_copies(meta_ref, copier(slot), lambda cp: cp.wait())
        buf_ref[slot, :MOE_BLK] = jnp.zeros((MOE_BLK, XS_W), U32)
        _tail_copies(tail_ref, copier(slot), lambda cp: cp.start())
        _tail_copies(tail_ref, copier(slot), lambda cp: cp.wait())


def _dispatch(meta, tail, loff, rt, h2, n_pad):
    T, D = h2.shape
    nt = meta.shape[0]
    tm = T // nt
    return pl.pallas_call(
        _dispatch_body,
        grid=(nt,),
        in_specs=[pl.BlockSpec((1, 1, meta.shape[2]), lambda i: (i, 0, 0), memory_space=pltpu.SMEM),
                  pl.BlockSpec((1, 1, meta.shape[2]), lambda i: (jnp.maximum(i - 1, 0), 0, 0),
                               memory_space=pltpu.SMEM),
                  pl.BlockSpec((1, 1, tail.shape[2]), lambda i: (0, 0, 0), memory_space=pltpu.SMEM),
                  pl.BlockSpec((1, 1, LANES), lambda i: (i, 0, 0)),
                  pl.BlockSpec((tm, LANES), lambda i: (i, 0)),
                  pl.BlockSpec((tm, D), lambda i: (i, 0))],
        out_specs=pl.BlockSpec(memory_space=pl.ANY),
        out_shape=jax.ShapeDtypeStruct((n_pad, XS_W), U32),
        scratch_shapes=[pltpu.VMEM((2, _local_rows(tm), XS_W), U32), pltpu.SemaphoreType.DMA((2,))],
        compiler_params=pltpu.CompilerParams(
            dimension_semantics=("arbitrary",), vmem_limit_bytes=VMEM_LIMIT),
        name="moe_dispatch",
    )(meta, meta, tail, loff, rt, h2)


def _expert_body(be_ref, nused_ref, xs_ref, wg_ref, wu_ref, wd_ref, ys_ref, wgb_ref, wub_ref, wdb_ref):
    i = pl.program_id(0)

    @pl.when((i == 0) | (be_ref[i] != be_ref[jnp.maximum(i - 1, 0)]))
    def _():
        wgb_ref[...] = wg_ref[0].astype(BF16)
        wub_ref[...] = wu_ref[0].astype(BF16)
        wdb_ref[...] = wd_ref[0].astype(BF16)

    @pl.when(i >= nused_ref[0])
    def _():
        ys_ref[...] = jnp.zeros_like(ys_ref)

    @pl.when(i < nused_ref[0])
    def _():
        dot = functools.partial(jnp.dot, preferred_element_type=F32)
        half = D_MODEL // 2
        ha, hb = _unpack_pair(xs_ref[:, :half])
        h = jnp.concatenate([ha, hb], axis=1).astype(BF16)
        wl = lax.bitcast_convert_type(xs_ref[:, half:], F32)
        w = wl[:, 0:1] + wl[:, 1:2]
        g = dot(h, wgb_ref[...])
        u = dot(h, wub_ref[...])
        y = dot((g * _sigmoid(g) * u).astype(BF16), wdb_ref[...]) * w
        ys_ref[...] = _pack_pair(y[:, :half], y[:, half:])


def _experts(block_expert, nused, xs, wg, wu, wd):
    n_pad = xs.shape[0]
    blk_in = pl.BlockSpec((MOE_BLK, XS_W), lambda i, be, nu: (jnp.minimum(i, nu[0] - 1), 0))
    blk_out = pl.BlockSpec((MOE_BLK, D_MODEL // 2), lambda i, be, nu: (i, 0))
    wspec = lambda a: pl.BlockSpec((1,) + a.shape[1:], lambda i, be, nu: (be[i], 0, 0))
    return pl.pallas_call(
        _expert_body,
        grid_spec=pltpu.PrefetchScalarGridSpec(
            num_scalar_prefetch=2,
            grid=(n_pad // MOE_BLK,),
            in_specs=[blk_in, wspec(wg), wspec(wu), wspec(wd)],
            out_specs=blk_out,
            scratch_shapes=[pltpu.VMEM(wg.shape[1:], BF16), pltpu.VMEM(wu.shape[1:], BF16),
                            pltpu.VMEM(wd.shape[1:], BF16)],
        ),
        out_shape=jax.ShapeDtypeStruct((n_pad, D_MODEL // 2), U32),
        compiler_params=pltpu.CompilerParams(
            dimension_semantics=("arbitrary",), vmem_limit_bytes=VMEM_LIMIT),
        name="moe_experts",
    )(block_expert, nused, xs, wg, wu, wd)


def _combine_body(meta_ref, next_meta_ref, loff_ref, rt_ref, x_ref, fg_ref, ys_ref, o_ref, buf_ref, sem,
                  *, final_norm):
    i = pl.program_id(0)
    slot = i % 2

    def copier(k):
        def make_copy(loc_row, glob_row, n):
            return pltpu.make_async_copy(ys_ref.at[pl.ds(glob_row, n)], buf_ref.at[k, pl.ds(loc_row, n)],
                                         sem.at[k])
        return make_copy

    @pl.when(i == 0)
    def _():
        buf_ref[...] = jnp.zeros_like(buf_ref)
        _segment_copies(meta_ref, copier(0), lambda cp: cp.start())

    @pl.when(i + 1 < pl.num_programs(0))
    def _():
        _segment_copies(next_meta_ref, copier(1 - slot), lambda cp: cp.start())

    p1, p2 = _local_perm(rt_ref[...], loff_ref[0], buf_ref.shape[1])
    pt = p1 + p2
    _segment_copies(meta_ref, copier(slot), lambda cp: cp.wait())
    yl = jnp.concatenate(_unpack_pair(buf_ref[slot]), axis=1).astype(BF16)
    out = x_ref[...] + jnp.dot(pt, yl, preferred_element_type=F32)
    if final_norm:
        out = _rms(out) * fg_ref[...]
    o_ref[...] = out


def _combine(meta, loff, rt, x, fg, ys, final_norm):
    T, D = x.shape
    nt = meta.shape[0]
    tm = T // nt
    return pl.pallas_call(
        functools.partial(_combine_body, final_norm=final_norm),
        grid=(nt,),
        in_specs=[pl.BlockSpec((1, 1, meta.shape[2]), lambda i: (i, 0, 0), memory_space=pltpu.SMEM),
                  pl.BlockSpec((1, 1, meta.shape[2]), lambda i: (jnp.minimum(i + 1, nt - 1), 0, 0),
                               memory_space=pltpu.SMEM),
                  pl.BlockSpec((1, 1, LANES), lambda i: (i, 0, 0)),
                  pl.BlockSpec((tm, LANES), lambda i: (i, 0)),
                  pl.BlockSpec((tm, D), lambda i: (i, 0)),
                  pl.BlockSpec((1, D), lambda i: (0, 0)),
                  pl.BlockSpec(memory_space=pl.ANY)],
        out_specs=pl.BlockSpec((tm, D), lambda i: (i, 0)),
        out_shape=jax.ShapeDtypeStruct((T, D), F32),
        scratch_shapes=[pltpu.VMEM((2, _local_rows(tm), D // 2), U32), pltpu.SemaphoreType.DMA((2,))],
        compiler_params=pltpu.CompilerParams(
            dimension_semantics=("arbitrary",), vmem_limit_bytes=VMEM_LIMIT),
        name="moe_combine",
    )(meta, meta, loff, rt, x, fg, ys)


def _rope_tables(L):
    half = HEAD_DIM // 2
    inv = ROPE_THETA ** (-jnp.arange(half, dtype=F32) / half)
    ang = jnp.arange(L, dtype=F32)[:, None] * inv[None, :]
    lane = np.arange(LANES)
    idx = lane % half
    first = jnp.asarray((lane % HEAD_DIM) < half)
    cos = jnp.cos(ang)[:, idx]
    sin = jnp.sin(ang)[:, idx]
    return cos, jnp.where(first, -sin, 0.0), jnp.where(first, 0.0, sin)


def _moe_layout(cnt, T, layer):
    nt = cnt.shape[0] // SUBLANES
    c = cnt.reshape(nt, SUBLANES, LANES)[:, 0, :N_EXPERTS].astype(I32)
    c8 = (c + SUBLANES - 1) // SUBLANES * SUBLANES
    tile_off = jnp.cumsum(c8, axis=0) - c8
    total = jnp.sum(c8, axis=0)
    padded = (total + MOE_BLK - 1) // MOE_BLK * MOE_BLK
    pend = jnp.cumsum(padded)
    dst = (pend - padded)[None, :] + tile_off
    loff = jnp.cumsum(c8, axis=1) - c8
    meta = jnp.concatenate([dst, c8 // SUBLANES, loff], axis=1).reshape(nt, 1, 3 * N_EXPERTS)
    loff_f = jnp.pad(loff.astype(F32), ((0, 0), (0, LANES - N_EXPERTS))).reshape(nt, 1, LANES)
    n_pad = 2 * T + nt * N_EXPERTS * SUBLANES + N_EXPERTS * MOE_BLK
    n_pad = (n_pad + MOE_BLK - 1) // MOE_BLK * MOE_BLK
    starts = jnp.arange(n_pad // MOE_BLK, dtype=I32) * MOE_BLK
    block_expert = jnp.minimum(jnp.sum(starts[:, None] >= pend[None, :], axis=1), N_EXPERTS - 1)
    block_expert = block_expert.astype(I32) + layer * N_EXPERTS
    nused = (pend[-1:] // MOE_BLK).astype(I32)
    tail = jnp.concatenate([pend - padded + total, (padded - total) // SUBLANES, nused,
                            jnp.full((1,), n_pad // MOE_BLK, I32)]).reshape(1, 1, 2 * N_EXPERTS + 2)
    return meta, tail, loff_f, block_expert, nused, n_pad


def kernel(x, attn_norm_g, w_in, sink_logits, hgrn_lb_table, hgrn_norm_g, w_up_attn, w_up_hgrn, w_out,
           ffn_norm_g, w_router_group, b_router_group, w_router_expert, b_router_expert,
           w_expert_gate, w_expert_up, w_expert_down, final_norm_g):
    B, L, D = x.shape
    T = B * L
    depth = w_in.shape[0]
    sm = jax.nn.softmax(hgrn_lb_table.astype(F32), axis=0)
    lower_bounds = jnp.cumsum(sm, axis=0) - sm[:1]
    cos, s1, s2 = _rope_tables(L)
    tm_post = min(TM_POST, T)
    ltri = jnp.tril(jnp.ones((tm_post, tm_post), BF16), -1)
    fin_g = final_norm_g.reshape(1, D).astype(F32)
    w_in_bf16 = w_in.astype(BF16)
    all_layers = lambda w: w.reshape((depth * N_EXPERTS,) + w.shape[2:])
    wg_all, wu_all, wd_all = all_layers(w_expert_gate), all_layers(w_expert_up), all_layers(w_expert_down)

    for l in range(depth):
        qa, ka, va, qb, sf, sb, ib, og, ga, gb = _inproj(
            x, attn_norm_g[l].reshape(1, D), w_in_bf16, l, cos, s1, s2)
        ya = _attention(qa, ka, va, sink_logits[l].astype(F32))
        of, orv = _hgrn(qb, sf, sb, ib, lower_bounds[l])

        rw = jnp.concatenate(
            [w_router_group[l], jnp.transpose(w_router_expert[l], (1, 0, 2)).reshape(D, N_EXPERTS)], axis=1)
        rw = jnp.pad(rw.astype(F32), ((0, 0), (0, LANES - rw.shape[1])))
        rwh = rw.astype(BF16)
        rwl = (rw - rwh.astype(F32)).astype(BF16)
        rb = jnp.concatenate([b_router_group[l], b_router_expert[l].reshape(-1)]).astype(F32)
        rb = jnp.pad(rb, (0, LANES - rb.shape[0])).reshape(1, LANES)
        flat = lambda a: a.reshape(T, a.shape[-1])
        xn, h2, rt, cnt = _post(
            flat(x), flat(ya), flat(of), flat(orv), flat(og), flat(ga), flat(gb),
            w_up_attn[l].astype(BF16), w_up_hgrn[l].astype(BF16), w_out[l].astype(BF16),
            hgrn_norm_g[l].reshape(1, HGRN_W).astype(F32), ffn_norm_g[l].reshape(1, D).astype(F32),
            rwh, rwl, rb, ltri)

        meta, tail, loff, block_expert, nused, n_pad = _moe_layout(cnt, T, l)
        xs = _dispatch(meta, tail, loff, rt, h2, n_pad)
        ys = _experts(block_expert, nused, xs, wg_all, wu_all, wd_all)
        x = _combine(meta, loff, rt, xn, fin_g, ys, l == depth - 1).reshape(B, L, D)
    return x
```

```python
import functools

import numpy as np
import jax
import jax.numpy as jnp
from jax import lax
from jax.experimental import pallas as pl
from jax.experimental.pallas import tpu as pltpu

F32 = jnp.float32
BF16 = jnp.bfloat16
U32 = jnp.uint32
I32 = jnp.int32

D_MODEL = 1024
N_Q_HEADS = 8
N_KV_HEADS = 2
HEAD_DIM = 64
WINDOW = 128
ROPE_THETA = 10000.0
MASK_VALUE = -1e30
HGRN_HEADS = 4
HGRN_DIM = 128
LOGF_MIN = -4.0
F_MIN = float(np.exp(LOGF_MIN))
LOG2E = float(np.log2(np.e))
LB_FLOOR = 1e-30
N_GROUPS = 4
EXPERTS_PER_GROUP = 8
N_EXPERTS = N_GROUPS * EXPERTS_PER_GROUP
D_EXPERT = 512
NORM_EPS = 1e-6

ATTN_Q_W = N_Q_HEADS * HEAD_DIM
ATTN_KV_W = N_KV_HEADS * HEAD_DIM
HGRN_W = HGRN_HEADS * HGRN_DIM
OFF_QA = 0
OFF_KA = OFF_QA + ATTN_Q_W
OFF_VA = OFF_KA + ATTN_KV_W
OFF_QB = OFF_VA + ATTN_KV_W
OFF_ZF = OFF_QB + HGRN_W
OFF_ZB = OFF_ZF + HGRN_W
OFF_IB = OFF_ZB + HGRN_W
OFF_OG = OFF_IB + HGRN_W
OFF_GA = OFF_OG + HGRN_W
OFF_GB = OFF_GA + D_MODEL

LANES = 128
VMEM_LIMIT = 56 * 1024 * 1024

TM_PROJ = 512
TQ_ATTN = 512
TH_HGRN = 512
TM_POST = 512
MOE_BLK = 256


def _sigmoid(z):
    return 1.0 / (1.0 + jnp.exp(-z))


def _rms(x):
    return x * lax.rsqrt(jnp.mean(x * x, axis=-1, keepdims=True) + NORM_EPS)


def _nt(a, b):
    return lax.dot_general(a, b, (((1,), (1,)), ((), ())), preferred_element_type=F32)


def _pack_pair(a, b):
    ua = lax.bitcast_convert_type(a.astype(BF16).astype(F32), U32)
    ub = lax.bitcast_convert_type(b.astype(BF16).astype(F32), U32)
    return ua | (ub >> 16)


def _unpack_pair(u):
    hi = lax.bitcast_convert_type(u & jnp.uint32(0xFFFF0000), F32)
    lo = lax.bitcast_convert_type(u << 16, F32)
    return hi, lo


def _inproj_body(x_ref, g_ref, w_ref, cos_ref, s1_ref, s2_ref,
                 qa_ref, ka_ref, va_ref, qb_ref, sf_ref, sb_ref, ib_ref, og_ref, ga_ref, gb_ref):
    h = (_rms(x_ref[0]) * g_ref[...]).astype(BF16)

    def proj(off, width):
        return jnp.dot(h, w_ref[:, off:off + width], preferred_element_type=F32)

    cos, s1, s2 = cos_ref[...], s1_ref[...], s2_ref[...]

    def rope(t):
        return t * cos + pltpu.roll(t, 96, 1) * s1 + pltpu.roll(t, 32, 1) * s2

    q = proj(OFF_QA, ATTN_Q_W)
    for c in range(ATTN_Q_W // LANES):
        sl = slice(c * LANES, (c + 1) * LANES)
        qa_ref[0, :, sl] = (rope(q[:, sl]) * (HEAD_DIM ** -0.5)).astype(BF16)
    k = rope(proj(OFF_KA, ATTN_KV_W))
    ka_ref[0, :, :LANES] = k.astype(BF16)
    ka_ref[0, :, LANES:] = pltpu.roll(k, 64, 1).astype(BF16)
    v = proj(OFF_VA, ATTN_KV_W)
    va_ref[0, :, :LANES] = v.astype(BF16)
    va_ref[0, :, LANES:] = pltpu.roll(v, 64, 1).astype(BF16)
    t = proj(OFF_QB, HGRN_W)
    qb_ref[0] = (t * _sigmoid(t)).astype(BF16)
    sf_ref[0] = _sigmoid(proj(OFF_ZF, HGRN_W)).astype(BF16)
    sb_ref[0] = _sigmoid(proj(OFF_ZB, HGRN_W)).astype(BF16)
    ib_ref[0] = proj(OFF_IB, HGRN_W).astype(BF16)
    t = proj(OFF_OG, HGRN_W)
    og_ref[0] = (t * _sigmoid(t)).astype(BF16)
    ga_ref[0] = _sigmoid(proj(OFF_GA, D_MODEL)).astype(BF16)
    gb_ref[0] = _sigmoid(proj(OFF_GB, D_MODEL)).astype(BF16)


def _inproj(x, g, w_all, layer, cos, s1, s2):
    B, L, D = x.shape
    tm = min(TM_PROJ, L)
    widths = (ATTN_Q_W, 2 * ATTN_KV_W, 2 * ATTN_KV_W, HGRN_W, HGRN_W, HGRN_W, HGRN_W, HGRN_W,
              D_MODEL, D_MODEL)
    tab = pl.BlockSpec((tm, LANES), lambda b, i: (i, 0))
    return pl.pallas_call(
        _inproj_body,
        grid=(B, L // tm),
        in_specs=[
            pl.BlockSpec((1, tm, D), lambda b, i: (b, i, 0)),
            pl.BlockSpec((1, D), lambda b, i: (0, 0)),
            pl.BlockSpec((None,) + w_all.shape[1:], lambda b, i: (layer, 0, 0)),
            tab, tab, tab,
        ],
        out_specs=[pl.BlockSpec((1, tm, w), lambda b, i: (b, i, 0)) for w in widths],
        out_shape=[jax.ShapeDtypeStruct((B, L, w), BF16) for w in widths],
        compiler_params=pltpu.CompilerParams(
            dimension_semantics=("arbitrary", "arbitrary"), vmem_limit_bytes=VMEM_LIMIT),
        name="inproj",
    )(x, g, w_all, cos, s1, s2)


def _attn_body(sink_ref, q_ref, kp_ref, km_ref, kn_ref, vp_ref, vm_ref, vn_ref, o_ref, *, tq, seq):
    i = pl.program_id(1)
    kw = jnp.concatenate([kp_ref[0], km_ref[0], kn_ref[0]], axis=0)
    vw = jnp.concatenate([vp_ref[0], vm_ref[0], vn_ref[0]], axis=0)
    lo = lax.broadcasted_iota(I32, (1, LANES), 1) < HEAD_DIM
    zero = jnp.zeros((), BF16)

    def variants(t):
        a, b = t[:, :LANES], t[:, LANES:]
        return ((jnp.where(lo, a, zero), jnp.where(lo, zero, b)),
                (jnp.where(lo, b, zero), jnp.where(lo, zero, a)))

    kvar, vvar = variants(kw), variants(vw)
    row = lax.broadcasted_iota(I32, (WINDOW, 3 * WINDOW), 0)
    col = lax.broadcasted_iota(I32, (WINDOW, 3 * WINDOW), 1)
    band = (col >= row) & (col <= row + 2 * WINDOW)
    ncol = ATTN_Q_W // LANES
    group = lambda c: (2 * c) // (N_Q_HEADS // N_KV_HEADS)
    heads = [(c, hh) for c in range(ncol) for hh in range(2)]
    for j in range(tq // WINDOW):
        kpos = col + (i * tq + (j - 1) * WINDOW)
        bias = jnp.where(band & (kpos >= 0) & (kpos < seq), 0.0, MASK_VALUE)
        rows = slice(j * WINDOW, (j + 1) * WINDOW)
        win = slice(j * WINDOW, (j + 3) * WINDOW)
        scores = [_nt(q_ref[0, rows, c * LANES:(c + 1) * LANES], kvar[group(c)][hh][win]) for c, hh in heads]
        probs, denoms = [], []
        for (c, hh), s in zip(heads, scores):
            s = s + bias
            sk = sink_ref[2 * c + hh]
            m = jnp.maximum(jnp.max(s, axis=1, keepdims=True), sk)
            p = jnp.exp(s - m)
            denoms.append(jnp.sum(p, axis=1, keepdims=True) + jnp.exp(sk - m))
            probs.append(p.astype(BF16))
        for c in range(ncol):
            g = group(c)
            pv = jnp.dot(jnp.concatenate(probs[2 * c:2 * c + 2], axis=1),
                         jnp.concatenate([vvar[g][0][win], vvar[g][1][win]], axis=0),
                         preferred_element_type=F32)
            inv = jnp.where(lo, 1.0 / denoms[2 * c], 1.0 / denoms[2 * c + 1])
            o_ref[0, rows, c * LANES:(c + 1) * LANES] = (pv * inv).astype(BF16)


def _attention(qa, ka, va, sink):
    B, L, _ = qa.shape
    tq = min(TQ_ATTN, L)
    r = tq // WINDOW
    nb = L // WINDOW
    kvw = 2 * ATTN_KV_W
    main = pl.BlockSpec((1, tq, kvw), lambda b, i: (b, i, 0))
    prev = pl.BlockSpec((1, WINDOW, kvw), lambda b, i: (b, jnp.maximum(i * r - 1, 0), 0))
    nxt = pl.BlockSpec((1, WINDOW, kvw), lambda b, i: (b, jnp.minimum(i * r + r, nb - 1), 0))
    return pl.pallas_call(
        functools.partial(_attn_body, tq=tq, seq=L),
        grid=(B, L // tq),
        in_specs=[
            pl.BlockSpec(memory_space=pltpu.SMEM),
            pl.BlockSpec((1, tq, ATTN_Q_W), lambda b, i: (b, i, 0)),
            prev, main, nxt, prev, main, nxt,
        ],
        out_specs=pl.BlockSpec((1, tq, ATTN_Q_W), lambda b, i: (b, i, 0)),
        out_shape=jax.ShapeDtypeStruct((B, L, ATTN_Q_W), BF16),
        compiler_params=pltpu.CompilerParams(
            dimension_semantics=("arbitrary", "arbitrary"), vmem_limit_bytes=VMEM_LIMIT),
        name="win_attn",
    )(sink, qa, ka, ka, ka, va, va, va)


HB = 128
LEAF = 16


def _hgrn_masks(reverse):
    r = lax.broadcasted_iota(I32, (HB, HB), 0)
    c = lax.broadcasted_iota(I32, (HB, HB), 1)
    if reverse:
        r, c = c, r
    tri = jnp.where(c <= r, 1.0, 0.0).astype(BF16)
    same = lambda s: (r >> s) == (c >> s)
    cross = lambda bit: ((r & bit) != 0) & ((c & bit) == 0)
    lev = jnp.where(same(4) & (c <= r), 0,
                    jnp.where(same(5) & cross(16), 1,
                              jnp.where(same(6) & cross(32), 2, 3)))
    return tri, lev


def _hgrn_gates(s, lb, lbc, tri):
    f = lbc + (1.0 - lb) * s
    logf = jnp.maximum(jnp.log(f), LOGF_MIN)
    kk = 1.0 - jnp.maximum(f, F_MIN)
    hi = logf.astype(BF16)
    mid = (logf - hi.astype(F32)).astype(BF16)
    dot = functools.partial(jnp.dot, preferred_element_type=F32)
    return kk, (dot(tri, hi) + dot(tri, mid)) * LOG2E


def _hgrn_scores(q, kk, cum, v, st_ref, reverse):
    def ref_row(idx):
        return cum[idx:idx + 1]

    total = ref_row(0) if reverse else ref_row(HB - 1)
    q_blk = q * jnp.exp2(cum)
    k_blk = kk * jnp.exp2(total - cum)

    ql, kl = [], []
    for c in range(HB // LEAF):
        rows = slice(c * LEAF, (c + 1) * LEAF)
        edge = (c + 1) * LEAF if reverse else c * LEAF - 1
        loc = cum[rows] - ref_row(edge) if 0 <= edge < HB else cum[rows]
        ql.append(q[rows] * jnp.exp2(loc))
        kl.append(kk[rows] * jnp.exp2(-loc))
    pairs = [(jnp.concatenate(ql, axis=0), jnp.concatenate(kl, axis=0))]
    for w in (16, 32, 64):
        qp, kp = [], []
        zeros = jnp.zeros((w, HB), F32)
        for s0 in range(0, HB, 2 * w):
            first, second = slice(s0, s0 + w), slice(s0 + w, s0 + 2 * w)
            if reverse:
                ref = ref_row(s0 + w)
                qp += [q[first] * jnp.exp2(cum[first] - ref), zeros]
                kp += [zeros, kk[second] * jnp.exp2(ref - cum[second])]
            else:
                ref = ref_row(s0 + w - 1)
                qp += [zeros, q[second] * jnp.exp2(cum[second] - ref)]
                kp += [kk[first] * jnp.exp2(ref - cum[first]), zeros]
        pairs.append((jnp.concatenate(qp, axis=0), jnp.concatenate(kp, axis=0)))
    scores = [_nt(qq.astype(BF16), kq.astype(BF16)) for qq, kq in pairs]
    st = st_ref[...]
    o_state = _nt(q_blk.astype(BF16), st.astype(BF16))
    st_ref[...] = st * jnp.exp2(total) + lax.dot_general(
        v, k_blk.astype(BF16), (((0,), (0,)), ((), ())), preferred_element_type=F32)
    return scores, o_state


def _hgrn_output(scores, o_state, v, lev):
    a0, a1, a2, a3 = scores
    a = jnp.where(lev == 0, a0, jnp.where(lev == 1, a1, jnp.where(lev == 2, a2, a3)))
    return jnp.dot(a.astype(BF16), v, preferred_element_type=F32) + o_state


def _hgrn_body(lb_ref, qf_ref, sf_ref, vf_ref, qr_ref, sr_ref, vr_ref, of_ref, or_ref, st_ref, *, th):
    @pl.when(pl.program_id(1) == 0)
    def _():
        st_ref[...] = jnp.zeros_like(st_ref)

    lb = lb_ref[...]
    lbc = jnp.maximum(lb, LB_FLOOR)
    masks = (_hgrn_masks(False), _hgrn_masks(True))
    nblk = th // HB
    streams = ((qf_ref, sf_ref, vf_ref, of_ref), (qr_ref, sr_ref, vr_ref, or_ref))

    def body(t, carry):
        rows = [pl.multiple_of((nblk - 1 - t if d else t) * HB, HB) for d in range(2)]
        units = [(d, slice(h * HGRN_DIM, (h + 1) * HGRN_DIM)) for d in range(2) for h in range(HGRN_HEADS)]
        gates = [_hgrn_gates(streams[d][1][0, pl.ds(rows[d], HB), sl].astype(F32),
                             lb[d:d + 1, sl], lbc[d:d + 1, sl], masks[d][0]) for d, sl in units]
        mids = [_hgrn_scores(streams[d][0][0, pl.ds(rows[d], HB), sl].astype(F32), kk, cum,
                             streams[d][2][0, pl.ds(rows[d], HB), sl], st_ref.at[d, sl.start // HGRN_DIM],
                             bool(d)) for (d, sl), (kk, cum) in zip(units, gates)]
        for (d, sl), (scores, o_state) in zip(units, mids):
            o = _hgrn_output(scores, o_state, streams[d][2][0, pl.ds(rows[d], HB), sl], masks[d][1])
            streams[d][3][0, pl.ds(rows[d], HB), sl] = o.astype(BF16)
        return carry

    lax.fori_loop(0, nblk, body, 0)


def _hgrn(qb, sf, sb, ib, lb):
    B, L, _ = qb.shape
    th = min(TH_HGRN, L)
    n = L // th
    fwd = pl.BlockSpec((1, th, HGRN_W), lambda b, i: (b, i, 0))
    rev = pl.BlockSpec((1, th, HGRN_W), lambda b, i: (b, n - 1 - i, 0))
    return pl.pallas_call(
        functools.partial(_hgrn_body, th=th),
        grid=(B, n),
        in_specs=[pl.BlockSpec((2, HGRN_W), lambda b, i: (0, 0)), fwd, fwd, fwd, rev, rev, rev],
        out_specs=[fwd, rev],
        out_shape=[jax.ShapeDtypeStruct((B, L, HGRN_W), BF16)] * 2,
        scratch_shapes=[pltpu.VMEM((2, HGRN_HEADS, HGRN_DIM, HGRN_DIM), F32)],
        compiler_params=pltpu.CompilerParams(
            dimension_semantics=("arbitrary", "arbitrary"), vmem_limit_bytes=VMEM_LIMIT),
        name="hgrn2",
    )(lb, qb, sf, ib, qb, sb, ib)


def _post_body(x_ref, ya_ref, of_ref, or_ref, og_ref, ga_ref, gb_ref, wua_ref, wuh_ref, wo_ref,
               hg_ref, fg_ref, rwh_ref, rwl_ref, rb_ref, ltri_ref,
               xn_ref, h2_ref, rt_ref, cnt_ref):
    dot = functools.partial(jnp.dot, preferred_element_type=F32)
    o = of_ref[...].astype(F32) + or_ref[...].astype(F32)
    yh = (_rms(o) * hg_ref[...] * og_ref[...].astype(F32)).astype(BF16)
    merged = (ga_ref[...].astype(F32) * dot(ya_ref[...], wua_ref[...])
              + gb_ref[...].astype(F32) * dot(yh, wuh_ref[...]))
    xn = x_ref[...] + dot(merged.astype(BF16), wo_ref[...])
    xn_ref[...] = xn
    h2 = _rms(xn) * fg_ref[...]
    hi = h2.astype(BF16)
    h2_ref[...] = hi

    lo = (h2 - hi.astype(F32)).astype(BF16)
    logits = dot(hi, rwh_ref[...]) + dot(lo, rwh_ref[...]) + dot(hi, rwl_ref[...]) + rb_ref[...]
    lane = lax.broadcasted_iota(I32, logits.shape, 1)
    lanef = lane.astype(F32)
    big = jnp.float32(1e9)
    ninf = jnp.float32(-jnp.inf)
    red = dict(axis=1, keepdims=True)
    gmask = lane < N_GROUPS
    mg = jnp.max(jnp.where(gmask, logits, ninf), **red)
    p_top = 1.0 / jnp.sum(jnp.where(gmask, jnp.exp(logits - mg), 0.0), **red)
    gi = jnp.min(jnp.where(gmask & (logits == mg), lanef, big), **red)
    emask = ((lane >= N_GROUPS) & (lane < N_GROUPS + N_EXPERTS)
             & (((lane - N_GROUPS) >> 3).astype(F32) == gi))
    le = jnp.where(emask, logits, ninf)
    m1 = jnp.max(le, **red)
    i1 = jnp.min(jnp.where(le == m1, lanef, big), **red)
    le2 = jnp.where(lanef == i1, ninf, le)
    m2 = jnp.max(le2, **red)
    i2 = jnp.min(jnp.where(le2 == m2, lanef, big), **red)
    r = jnp.exp(m2 - m1)
    w1 = 1.0 / (1.0 + r)
    w2 = r * w1
    e1 = i1 - N_GROUPS
    e2 = i2 - N_GROUPS
    is1, is2 = lanef == e1, lanef == e2
    onehot = jnp.where(is1 | is2, 1.0, 0.0)
    pref = dot(ltri_ref[...], onehot.astype(BF16))
    rank1 = jnp.sum(jnp.where(is1, pref, 0.0), **red)
    rank2 = jnp.sum(jnp.where(is2, pref, 0.0), **red)
    cnt_ref[...] = jnp.broadcast_to(jnp.sum(onehot, axis=0, keepdims=True), cnt_ref.shape)
    vals = (e1, e2, rank1, rank2, p_top * w1, p_top * w2)
    rt = jnp.zeros_like(logits)
    for idx, val in enumerate(vals):
        rt = jnp.where(lane == idx, val, rt)
    rt_ref[...] = rt


RT_E1, RT_E2, RT_RANK1, RT_RANK2, RT_W1, RT_W2 = range(6)
SUBLANES = 8


def _post(x, ya, of, orv, og, ga, gb, wua, wuh, wo, hg, fg, rwh, rwl, rb, ltri):
    T, D = x.shape
    tm = ltri.shape[0]
    row = lambda w: pl.BlockSpec((tm, w), lambda i: (i, 0))
    full = lambda a: pl.BlockSpec(a.shape, lambda i: (0,) * a.ndim)
    return pl.pallas_call(
        _post_body,
        grid=(T // tm,),
        in_specs=[row(D), row(ATTN_Q_W), row(HGRN_W), row(HGRN_W), row(HGRN_W), row(D), row(D),
                  full(wua), full(wuh), full(wo), full(hg), full(fg), full(rwh), full(rwl), full(rb),
                  full(ltri)],
        out_specs=[row(D), row(D), row(LANES), pl.BlockSpec((SUBLANES, LANES), lambda i: (i, 0))],
        out_shape=[jax.ShapeDtypeStruct((T, D), F32), jax.ShapeDtypeStruct((T, D), BF16),
                   jax.ShapeDtypeStruct((T, LANES), F32),
                   jax.ShapeDtypeStruct((T // tm * SUBLANES, LANES), F32)],
        compiler_params=pltpu.CompilerParams(
            dimension_semantics=("arbitrary",), vmem_limit_bytes=VMEM_LIMIT),
        name="post_mixer_router",
    )(x, ya, of, orv, og, ga, gb, wua, wuh, wo, hg, fg, rwh, rwl, rb, ltri)


XS_W = D_MODEL // 2 + LANES
SEG_BIG = 32
META_DST, META_LOFF, META_NBIG, META_NSMALL, META_TOTAL8 = (k * N_EXPERTS for k in range(5))
META_W = 5 * N_EXPERTS


def _local_rows(tm):
    return 2 * tm + N_EXPERTS * SUBLANES


def _local_pos(rt, loff_row):
    lanef = lax.broadcasted_iota(I32, rt.shape, 1).astype(F32)
    out = []
    for e_lane, r_lane in ((RT_E1, RT_RANK1), (RT_E2, RT_RANK2)):
        e = rt[:, e_lane:e_lane + 1]
        off = jnp.sum(jnp.where(lanef == e, loff_row, 0.0), axis=1, keepdims=True)
        out.append(off + rt[:, r_lane:r_lane + 1])
    return out


def _one_hot(hit):
    return jnp.where(hit, 1.0, 0.0).astype(BF16)


def _segment_starts(meta_ref, make_copy):
    for e in range(N_EXPERTS):
        dst = meta_ref[0, 0, META_DST + e]
        loff = meta_ref[0, 0, META_LOFF + e]
        n_big = meta_ref[0, 0, META_NBIG + e]
        n_small = meta_ref[0, 0, META_NSMALL + e]

        def big(k, c):
            o = k * SEG_BIG
            make_copy(pl.multiple_of(loff + o, SUBLANES), pl.multiple_of(dst + o, SUBLANES), SEG_BIG).start()
            return c

        lax.fori_loop(0, n_big, big, 0)

        def small(k, c):
            o = n_big * SEG_BIG + k * SUBLANES
            make_copy(pl.multiple_of(loff + o, SUBLANES), pl.multiple_of(dst + o, SUBLANES), SUBLANES).start()
            return c

        lax.fori_loop(0, n_small, small, 0)


def _segment_wait(meta_ref, make_copy, rows):
    total8 = meta_ref[0, 0, META_TOTAL8]
    for bit in range((rows // SUBLANES).bit_length()):
        @pl.when((total8 & (1 << bit)) != 0)
        def _():
            make_copy(0, 0, SUBLANES << bit).wait()


TAIL_PIECES = tuple(SUBLANES << b for b in range((MOE_BLK // SUBLANES).bit_length() - 1))


def _tail_copies(tail_ref, make_copy, action):
    def per_expert(e, c):
        start = tail_ref[0, 0, e]
        n8 = tail_ref[0, 0, N_EXPERTS + e]
        for rows in TAIL_PIECES:
            bit = rows // SUBLANES

            @pl.when((n8 & bit) != 0)
            def _():
                off = (n8 & ~(2 * bit - 1)) * SUBLANES
                action(make_copy(0, pl.multiple_of(start + off, SUBLANES), rows))
        return c

    lax.fori_loop(0, N_EXPERTS, per_expert, 0)

    def per_block(b, c):
        action(make_copy(0, pl.multiple_of(b * MOE_BLK, MOE_BLK), MOE_BLK))
        return c

    lax.fori_loop(tail_ref[0, 0, 2 * N_EXPERTS], tail_ref[0, 0, 2 * N_EXPERTS + 1], per_block, 0)


def _dispatch_body(meta_ref, prev_meta_ref, tail_ref, loff_ref, rt_ref, h_ref, xs_ref, buf_ref, sem):
    i = pl.program_id(0)
    last = pl.num_programs(0) - 1
    slot = i % 2
    tm = rt_ref.shape[0]
    rows = buf_ref.shape[1]
    rt = rt_ref[...]
    dot = functools.partial(jnp.dot, preferred_element_type=F32)
    lane = lax.broadcasted_iota(I32, (tm, LANES), 1)
    l1, l2 = _local_pos(rt, loff_ref[0])
    parts = []
    for pos in (l1, l2):
        hi = jnp.floor(pos * (1.0 / 32.0))
        parts += [hi, pos - 32.0 * hi]
    cols = jnp.zeros((tm, LANES), F32)
    for k, part in enumerate(parts):
        cols = jnp.where(lane == k, part, cols)
    eye = _one_hot(lax.broadcasted_iota(I32, (SUBLANES, LANES), 0) == lax.broadcasted_iota(I32, (SUBLANES, LANES), 1))
    lanes = _nt(eye, cols.astype(BF16))
    r1 = (32.0 * lanes[0:1] + lanes[1:2]).astype(I32)
    r2 = (32.0 * lanes[2:3] + lanes[3:4]).astype(I32)
    row = lax.broadcasted_iota(I32, (rows, tm), 0)
    p1, p2 = _one_hot(row == r1), _one_hot(row == r2)
    xl = dot(p1 + p2, h_ref[...])
    half = D_MODEL // 2
    buf_ref[slot, :, :half] = _pack_pair(xl[:, :half], xl[:, half:])

    def wcols(w):
        hi = w.astype(BF16).astype(F32)
        return jnp.where(lane == 0, hi, jnp.where(lane == 1, w - hi, 0.0)).astype(BF16)

    wl = dot(p1, wcols(rt[:, RT_W1:RT_W1 + 1])) + dot(p2, wcols(rt[:, RT_W2:RT_W2 + 1]))
    buf_ref[slot, :, half:] = lax.bitcast_convert_type(wl, U32)

    def copier(k):
        def make_copy(src_row, dst_row, n):
            return pltpu.make_async_copy(buf_ref.at[k, pl.ds(src_row, n)], xs_ref.at[pl.ds(dst_row, n)],
                                         sem.at[k])
        return make_copy

    _segment_starts(meta_ref, copier(slot))

    @pl.when(i > 0)
    def _():
        _segment_wait(prev_meta_ref, copier(1 - slot), rows)

    @pl.when(i == last)
    def _():
        _segment_wait(meta_ref, copier(slot), rows)
        buf_ref[slot, :MOE_BLK] = jnp.zeros((MOE_BLK, XS_W), U32)
        _tail_copies(tail_ref, copier(slot), lambda cp: cp.start())
        _tail_copies(tail_ref, copier(slot), lambda cp: cp.wait())


def _dispatch(meta, tail, loff, rt, h2, n_pad):
    T, D = h2.shape
    nt = meta.shape[0]
    tm = T // nt
    return pl.pallas_call(
        _dispatch_body,
        grid=(nt,),
        in_specs=[pl.BlockSpec((1, 1, meta.shape[2]), lambda i: (i, 0, 0), memory_space=pltpu.SMEM),
                  pl.BlockSpec((1, 1, meta.shape[2]), lambda i: (jnp.maximum(i - 1, 0), 0, 0),
                               memory_space=pltpu.SMEM),
                  pl.BlockSpec((1, 1, tail.shape[2]), lambda i: (0, 0, 0), memory_space=pltpu.SMEM),
                  pl.BlockSpec((1, 1, LANES), lambda i: (i, 0, 0)),
                  pl.BlockSpec((tm, LANES), lambda i: (i, 0)),
                  pl.BlockSpec((tm, D), lambda i: (i, 0))],
        out_specs=pl.BlockSpec(memory_space=pl.ANY),
        out_shape=jax.ShapeDtypeStruct((n_pad, XS_W), U32),
        scratch_shapes=[pltpu.VMEM((2, _local_rows(tm), XS_W), U32), pltpu.SemaphoreType.DMA((2,))],
        compiler_params=pltpu.CompilerParams(
            dimension_semantics=("arbitrary",), vmem_limit_bytes=VMEM_LIMIT),
        name="moe_dispatch",
    )(meta, meta, tail, loff, rt, h2)


def _expert_body(be_ref, nused_ref, xs_ref, wg_ref, wu_ref, wd_ref, ys_ref, wgb_ref, wub_ref, wdb_ref):
    i = pl.program_id(0)

    @pl.when((i == 0) | (be_ref[i] != be_ref[jnp.maximum(i - 1, 0)]))
    def _():
        wgb_ref[...] = wg_ref[0].astype(BF16)
        wub_ref[...] = wu_ref[0].astype(BF16)
        wdb_ref[...] = wd_ref[0].astype(BF16)

    @pl.when(i >= nused_ref[0])
    def _():
        ys_ref[...] = jnp.zeros_like(ys_ref)

    @pl.when(i < nused_ref[0])
    def _():
        dot = functools.partial(jnp.dot, preferred_element_type=F32)
        half = D_MODEL // 2
        ha, hb = _unpack_pair(xs_ref[:, :half])
        h = jnp.concatenate([ha, hb], axis=1).astype(BF16)
        wl = lax.bitcast_convert_type(xs_ref[:, half:], F32)
        w = wl[:, 0:1] + wl[:, 1:2]
        g = dot(h, wgb_ref[...])
        u = dot(h, wub_ref[...])
        y = dot((g * _sigmoid(g) * u).astype(BF16), wdb_ref[...]) * w
        ys_ref[...] = _pack_pair(y[:, :half], y[:, half:])


def _experts(block_expert, nused, xs, wg, wu, wd):
    n_pad = xs.shape[0]
    blk_in = pl.BlockSpec((MOE_BLK, XS_W), lambda i, be, nu: (jnp.minimum(i, nu[0] - 1), 0))
    blk_out = pl.BlockSpec((MOE_BLK, D_MODEL // 2), lambda i, be, nu: (i, 0))
    wspec = lambda a: pl.BlockSpec((1,) + a.shape[1:], lambda i, be, nu: (be[i], 0, 0))
    return pl.pallas_call(
        _expert_body,
        grid_spec=pltpu.PrefetchScalarGridSpec(
            num_scalar_prefetch=2,
            grid=(n_pad // MOE_BLK,),
            in_specs=[blk_in, wspec(wg), wspec(wu), wspec(wd)],
            out_specs=blk_out,
            scratch_shapes=[pltpu.VMEM(wg.shape[1:], BF16), pltpu.VMEM(wu.shape[1:], BF16),
                            pltpu.VMEM(wd.shape[1:], BF16)],
        ),
        out_shape=jax.ShapeDtypeStruct((n_pad, D_MODEL // 2), U32),
        compiler_params=pltpu.CompilerParams(
            dimension_semantics=("arbitrary",), vmem_limit_bytes=VMEM_LIMIT),
        name="moe_experts",
    )(block_expert, nused, xs, wg, wu, wd)


def _combine_body(meta_ref, next_meta_ref, loff_ref, rt_ref, x_ref, fg_ref, ys_ref, o_ref, buf_ref, sem,
                  *, final_norm):
    i = pl.program_id(0)
    slot = i % 2

    def copier(k):
        def make_copy(loc_row, glob_row, n):
            return pltpu.make_async_copy(ys_ref.at[pl.ds(glob_row, n)], buf_ref.at[k, pl.ds(loc_row, n)],
                                         sem.at[k])
        return make_copy

    rows = buf_ref.shape[1]

    @pl.when(i == 0)
    def _():
        buf_ref[...] = jnp.zeros_like(buf_ref)
        _segment_starts(meta_ref, copier(0))

    @pl.when(i + 1 < pl.num_programs(0))
    def _():
        _segment_starts(next_meta_ref, copier(1 - slot))

    l1, l2 = _local_pos(rt_ref[...], loff_ref[0])
    col = lax.broadcasted_iota(I32, (rt_ref.shape[0], rows), 1)
    pt = _one_hot(col == l1.astype(I32)) + _one_hot(col == l2.astype(I32))
    _segment_wait(meta_ref, copier(slot), rows)
    yl = jnp.concatenate(_unpack_pair(buf_ref[slot]), axis=1).astype(BF16)
    out = x_ref[...] + jnp.dot(pt, yl, preferred_element_type=F32)
    if final_norm:
        out = _rms(out) * fg_ref[...]
    o_ref[...] = out


def _combine(meta, loff, rt, x, fg, ys, final_norm):
    T, D = x.shape
    nt = meta.shape[0]
    tm = T // nt
    return pl.pallas_call(
        functools.partial(_combine_body, final_norm=final_norm),
        grid=(nt,),
        in_specs=[pl.BlockSpec((1, 1, meta.shape[2]), lambda i: (i, 0, 0), memory_space=pltpu.SMEM),
                  pl.BlockSpec((1, 1, meta.shape[2]), lambda i: (jnp.minimum(i + 1, nt - 1), 0, 0),
                               memory_space=pltpu.SMEM),
                  pl.BlockSpec((1, 1, LANES), lambda i: (i, 0, 0)),
                  pl.BlockSpec((tm, LANES), lambda i: (i, 0)),
                  pl.BlockSpec((tm, D), lambda i: (i, 0)),
                  pl.BlockSpec((1, D), lambda i: (0, 0)),
                  pl.BlockSpec(memory_space=pl.ANY)],
        out_specs=pl.BlockSpec((tm, D), lambda i: (i, 0)),
        out_shape=jax.ShapeDtypeStruct((T, D), F32),
        scratch_shapes=[pltpu.VMEM((2, _local_rows(tm), D // 2), U32), pltpu.SemaphoreType.DMA((2,))],
        compiler_params=pltpu.CompilerParams(
            dimension_semantics=("arbitrary",), vmem_limit_bytes=VMEM_LIMIT),
        name="moe_combine",
    )(meta, meta, loff, rt, x, fg, ys)


def _rope_tables(L):
    half = HEAD_DIM // 2
    inv = ROPE_THETA ** (-jnp.arange(half, dtype=F32) / half)
    ang = jnp.arange(L, dtype=F32)[:, None] * inv[None, :]
    lane = np.arange(LANES)
    idx = lane % half
    first = jnp.asarray((lane % HEAD_DIM) < half)
    cos = jnp.cos(ang)[:, idx]
    sin = jnp.sin(ang)[:, idx]
    return cos, jnp.where(first, -sin, 0.0), jnp.where(first, 0.0, sin)


def _moe_layout(cnt, T, layer):
    nt = cnt.shape[0] // SUBLANES
    c = cnt.reshape(nt, SUBLANES, LANES)[:, 0, :N_EXPERTS].astype(I32)
    c8 = (c + SUBLANES - 1) // SUBLANES * SUBLANES
    tile_off = jnp.cumsum(c8, axis=0) - c8
    total = jnp.sum(c8, axis=0)
    padded = (total + MOE_BLK - 1) // MOE_BLK * MOE_BLK
    pend = jnp.cumsum(padded)
    dst = (pend - padded)[None, :] + tile_off
    loff = jnp.cumsum(c8, axis=1) - c8
    n8 = c8 // SUBLANES
    per_big = SEG_BIG // SUBLANES
    total8 = jnp.broadcast_to(jnp.sum(n8, axis=1, keepdims=True), n8.shape)
    meta = jnp.concatenate([dst, loff, n8 // per_big, n8 % per_big, total8], axis=1).reshape(nt, 1, META_W)
    loff_f = jnp.pad(loff.astype(F32), ((0, 0), (0, LANES - N_EXPERTS))).reshape(nt, 1, LANES)
    n_pad = 2 * T + nt * N_EXPERTS * SUBLANES + N_EXPERTS * MOE_BLK
    n_pad = (n_pad + MOE_BLK - 1) // MOE_BLK * MOE_BLK
    starts = jnp.arange(n_pad // MOE_BLK, dtype=I32) * MOE_BLK
    block_expert = jnp.minimum(jnp.sum(starts[:, None] >= pend[None, :], axis=1), N_EXPERTS - 1)
    block_expert = block_expert.astype(I32) + layer * N_EXPERTS
    nused = (pend[-1:] // MOE_BLK).astype(I32)
    tail = jnp.concatenate([pend - padded + total, (padded - total) // SUBLANES, nused,
                            jnp.full((1,), n_pad // MOE_BLK, I32)]).reshape(1, 1, 2 * N_EXPERTS + 2)
    return meta, tail, loff_f, block_expert, nused, n_pad


def kernel(x, attn_norm_g, w_in, sink_logits, hgrn_lb_table, hgrn_norm_g, w_up_attn, w_up_hgrn, w_out,
           ffn_norm_g, w_router_group, b_router_group, w_router_expert, b_router_expert,
           w_expert_gate, w_expert_up, w_expert_down, final_norm_g):
    B, L, D = x.shape
    T = B * L
    depth = w_in.shape[0]
    sm = jax.nn.softmax(hgrn_lb_table.astype(F32), axis=0)
    lower_bounds = jnp.cumsum(sm, axis=0) - sm[:1]
    cos, s1, s2 = _rope_tables(L)
    tm_post = min(TM_POST, T)
    ltri = jnp.tril(jnp.ones((tm_post, tm_post), BF16), -1)
    fin_g = final_norm_g.reshape(1, D).astype(F32)
    w_in_bf16 = w_in.astype(BF16)
    all_layers = lambda w: w.reshape((depth * N_EXPERTS,) + w.shape[2:])
    wg_all, wu_all, wd_all = all_layers(w_expert_gate), all_layers(w_expert_up), all_layers(w_expert_down)

    for l in range(depth):
        qa, ka, va, qb, sf, sb, ib, og, ga, gb = _inproj(
            x, attn_norm_g[l].reshape(1, D), w_in_bf16, l, cos, s1, s2)
        ya = _attention(qa, ka, va, sink_logits[l].astype(F32))
        of, orv = _hgrn(qb, sf, sb, ib, lower_bounds[l])

        rw = jnp.concatenate(
            [w_router_group[l], jnp.transpose(w_router_expert[l], (1, 0, 2)).reshape(D, N_EXPERTS)], axis=1)
        rw = jnp.pad(rw.astype(F32), ((0, 0), (0, LANES - rw.shape[1])))
        rwh = rw.astype(BF16)
        rwl = (rw - rwh.astype(F32)).astype(BF16)
        rb = jnp.concatenate([b_router_group[l], b_router_expert[l].reshape(-1)]).astype(F32)
        rb = jnp.pad(rb, (0, LANES - rb.shape[0])).reshape(1, LANES)
        flat = lambda a: a.reshape(T, a.shape[-1])
        xn, h2, rt, cnt = _post(
            flat(x), flat(ya), flat(of), flat(orv), flat(og), flat(ga), flat(gb),
            w_up_attn[l].astype(BF16), w_up_hgrn[l].astype(BF16), w_out[l].astype(BF16),
            hgrn_norm_g[l].reshape(1, HGRN_W).astype(F32), ffn_norm_g[l].reshape(1, D).astype(F32),
            rwh, rwl, rb, ltri)

        meta, tail, loff, block_expert, nused, n_pad = _moe_layout(cnt, T, l)
        xs = _dispatch(meta, tail, loff, rt, h2, n_pad)
        ys = _experts(block_expert, nused, xs, wg_all, wu_all, wd_all)
        x = _combine(meta, loff, rt, xn, fin_g, ys, l == depth - 1).reshape(B, L, D)
    return x
```

```python
import functools

import numpy as np
import jax
import jax.numpy as jnp
from jax import lax
from jax.experimental import pallas as pl
from jax.experimental.pallas import tpu as pltpu

F32 = jnp.float32
BF16 = jnp.bfloat16
U32 = jnp.uint32
I32 = jnp.int32

D_MODEL = 1024
N_Q_HEADS = 8
N_KV_HEADS = 2
HEAD_DIM = 64
WINDOW = 128
ROPE_THETA = 10000.0
MASK_VALUE = -1e30
HGRN_HEADS = 4
HGRN_DIM = 128
LOGF_MIN = -4.0
F_MIN = float(np.exp(LOGF_MIN))
LOG2E = float(np.log2(np.e))
LB_FLOOR = 1e-30
N_GROUPS = 4
EXPERTS_PER_GROUP = 8
N_EXPERTS = N_GROUPS * EXPERTS_PER_GROUP
D_EXPERT = 512
NORM_EPS = 1e-6

ATTN_Q_W = N_Q_HEADS * HEAD_DIM
ATTN_KV_W = N_KV_HEADS * HEAD_DIM
HGRN_W = HGRN_HEADS * HGRN_DIM
OFF_QA = 0
OFF_KA = OFF_QA + ATTN_Q_W
OFF_VA = OFF_KA + ATTN_KV_W
OFF_QB = OFF_VA + ATTN_KV_W
OFF_ZF = OFF_QB + HGRN_W
OFF_ZB = OFF_ZF + HGRN_W
OFF_IB = OFF_ZB + HGRN_W
OFF_OG = OFF_IB + HGRN_W
OFF_GA = OFF_OG + HGRN_W
OFF_GB = OFF_GA + D_MODEL

LANES = 128
VMEM_LIMIT = 56 * 1024 * 1024

TM_PROJ = 512
TQ_ATTN = 512
TH_HGRN = 512
TM_POST = 512
MOE_BLK = 512


def _sigmoid(z):
    return 1.0 / (1.0 + jnp.exp(-z))


def _rms(x):
    return x * lax.rsqrt(jnp.mean(x * x, axis=-1, keepdims=True) + NORM_EPS)


def _nt(a, b):
    return lax.dot_general(a, b, (((1,), (1,)), ((), ())), preferred_element_type=F32)


def _pack_pair(a, b):
    ua = lax.bitcast_convert_type(a.astype(BF16).astype(F32), U32)
    ub = lax.bitcast_convert_type(b.astype(BF16).astype(F32), U32)
    return ua | (ub >> 16)


def _unpack_pair(u):
    hi = lax.bitcast_convert_type(u & jnp.uint32(0xFFFF0000), F32)
    lo = lax.bitcast_convert_type(u << 16, F32)
    return hi, lo


def _inproj_body(x_ref, g_ref, w_ref, cos_ref, s1_ref, s2_ref,
                 qa_ref, ka_ref, va_ref, qb_ref, sf_ref, sb_ref, ib_ref, og_ref, ga_ref, gb_ref):
    h = (_rms(x_ref[0]) * g_ref[...]).astype(BF16)

    def proj(off, width):
        return jnp.dot(h, w_ref[:, off:off + width], preferred_element_type=F32)

    cos, s1, s2 = cos_ref[...], s1_ref[...], s2_ref[...]

    def rope(t):
        return t * cos + pltpu.roll(t, 96, 1) * s1 + pltpu.roll(t, 32, 1) * s2

    q = proj(OFF_QA, ATTN_Q_W)
    for c in range(ATTN_Q_W // LANES):
        sl = slice(c * LANES, (c + 1) * LANES)
        qa_ref[0, :, sl] = (rope(q[:, sl]) * (HEAD_DIM ** -0.5)).astype(BF16)
    k = rope(proj(OFF_KA, ATTN_KV_W))
    ka_ref[0, :, :LANES] = k.astype(BF16)
    ka_ref[0, :, LANES:] = pltpu.roll(k, 64, 1).astype(BF16)
    v = proj(OFF_VA, ATTN_KV_W)
    va_ref[0, :, :LANES] = v.astype(BF16)
    va_ref[0, :, LANES:] = pltpu.roll(v, 64, 1).astype(BF16)
    t = proj(OFF_QB, HGRN_W)
    qb_ref[0] = (t * _sigmoid(t)).astype(BF16)
    sf_ref[0] = _sigmoid(proj(OFF_ZF, HGRN_W)).astype(BF16)
    sb_ref[0] = _sigmoid(proj(OFF_ZB, HGRN_W)).astype(BF16)
    ib_ref[0] = proj(OFF_IB, HGRN_W).astype(BF16)
    t = proj(OFF_OG, HGRN_W)
    og_ref[0] = (t * _sigmoid(t)).astype(BF16)
    ga_ref[0] = _sigmoid(proj(OFF_GA, D_MODEL)).astype(BF16)
    gb_ref[0] = _sigmoid(proj(OFF_GB, D_MODEL)).astype(BF16)


def _inproj(x, g, w_all, layer, cos, s1, s2):
    B, L, D = x.shape
    tm = min(TM_PROJ, L)
    widths = (ATTN_Q_W, 2 * ATTN_KV_W, 2 * ATTN_KV_W, HGRN_W, HGRN_W, HGRN_W, HGRN_W, HGRN_W,
              D_MODEL, D_MODEL)
    tab = pl.BlockSpec((tm, LANES), lambda b, i: (i, 0))
    return pl.pallas_call(
        _inproj_body,
        grid=(B, L // tm),
        in_specs=[
            pl.BlockSpec((1, tm, D), lambda b, i: (b, i, 0)),
            pl.BlockSpec((1, D), lambda b, i: (0, 0)),
            pl.BlockSpec((None,) + w_all.shape[1:], lambda b, i: (layer, 0, 0)),
            tab, tab, tab,
        ],
        out_specs=[pl.BlockSpec((1, tm, w), lambda b, i: (b, i, 0)) for w in widths],
        out_shape=[jax.ShapeDtypeStruct((B, L, w), BF16) for w in widths],
        compiler_params=pltpu.CompilerParams(
            dimension_semantics=("arbitrary", "arbitrary"), vmem_limit_bytes=VMEM_LIMIT),
        name="inproj",
    )(x, g, w_all, cos, s1, s2)


def _attn_body(sink_ref, q_ref, kp_ref, km_ref, kn_ref, vp_ref, vm_ref, vn_ref, o_ref, *, tq, seq):
    i = pl.program_id(1)
    kw = jnp.concatenate([kp_ref[0], km_ref[0], kn_ref[0]], axis=0)
    vw = jnp.concatenate([vp_ref[0], vm_ref[0], vn_ref[0]], axis=0)
    lo = lax.broadcasted_iota(I32, (1, LANES), 1) < HEAD_DIM
    zero = jnp.zeros((), BF16)

    def variants(t):
        a, b = t[:, :LANES], t[:, LANES:]
        return ((jnp.where(lo, a, zero), jnp.where(lo, zero, b)),
                (jnp.where(lo, b, zero), jnp.where(lo, zero, a)))

    kvar, vvar = variants(kw), variants(vw)
    row = lax.broadcasted_iota(I32, (WINDOW, 3 * WINDOW), 0)
    col = lax.broadcasted_iota(I32, (WINDOW, 3 * WINDOW), 1)
    band = (col >= row) & (col <= row + 2 * WINDOW)
    ncol = ATTN_Q_W // LANES
    group = lambda c: (2 * c) // (N_Q_HEADS // N_KV_HEADS)
    heads = [(c, hh) for c in range(ncol) for hh in range(2)]
    for j in range(tq // WINDOW):
        kpos = col + (i * tq + (j - 1) * WINDOW)
        bias = jnp.where(band & (kpos >= 0) & (kpos < seq), 0.0, MASK_VALUE)
        rows = slice(j * WINDOW, (j + 1) * WINDOW)
        win = slice(j * WINDOW, (j + 3) * WINDOW)
        scores = [_nt(q_ref[0, rows, c * LANES:(c + 1) * LANES], kvar[group(c)][hh][win]) for c, hh in heads]
        probs, denoms = [], []
        for (c, hh), s in zip(heads, scores):
            s = s + bias
            sk = sink_ref[2 * c + hh]
            m = jnp.maximum(jnp.max(s, axis=1, keepdims=True), sk)
            p = jnp.exp(s - m)
            denoms.append(jnp.sum(p, axis=1, keepdims=True) + jnp.exp(sk - m))
            probs.append(p.astype(BF16))
        for c in range(ncol):
            g = group(c)
            pv = jnp.dot(jnp.concatenate(probs[2 * c:2 * c + 2], axis=1),
                         jnp.concatenate([vvar[g][0][win], vvar[g][1][win]], axis=0),
                         preferred_element_type=F32)
            inv = jnp.where(lo, 1.0 / denoms[2 * c], 1.0 / denoms[2 * c + 1])
            o_ref[0, rows, c * LANES:(c + 1) * LANES] = (pv * inv).astype(BF16)


def _attention(qa, ka, va, sink):
    B, L, _ = qa.shape
    tq = min(TQ_ATTN, L)
    r = tq // WINDOW
    nb = L // WINDOW
    kvw = 2 * ATTN_KV_W
    main = pl.BlockSpec((1, tq, kvw), lambda b, i: (b, i, 0))
    prev = pl.BlockSpec((1, WINDOW, kvw), lambda b, i: (b, jnp.maximum(i * r - 1, 0), 0))
    nxt = pl.BlockSpec((1, WINDOW, kvw), lambda b, i: (b, jnp.minimum(i * r + r, nb - 1), 0))
    return pl.pallas_call(
        functools.partial(_attn_body, tq=tq, seq=L),
        grid=(B, L // tq),
        in_specs=[
            pl.BlockSpec(memory_space=pltpu.SMEM),
            pl.BlockSpec((1, tq, ATTN_Q_W), lambda b, i: (b, i, 0)),
            prev, main, nxt, prev, main, nxt,
        ],
        out_specs=pl.BlockSpec((1, tq, ATTN_Q_W), lambda b, i: (b, i, 0)),
        out_shape=jax.ShapeDtypeStruct((B, L, ATTN_Q_W), BF16),
        compiler_params=pltpu.CompilerParams(
            dimension_semantics=("arbitrary", "arbitrary"), vmem_limit_bytes=VMEM_LIMIT),
        name="win_attn",
    )(sink, qa, ka, ka, ka, va, va, va)


HB = 128
LEAF = 32
HGRN_LEVELS = tuple(LEAF << k for k in range((HB // LEAF).bit_length() - 1))
LEAF_SHIFT = 0.5 * LEAF * (-LOGF_MIN) * LOG2E


def _hgrn_masks(reverse):
    r = lax.broadcasted_iota(I32, (HB, HB), 0)
    c = lax.broadcasted_iota(I32, (HB, HB), 1)
    if reverse:
        r, c = c, r
    tri = jnp.where(c <= r, 1.0, 0.0).astype(BF16)
    same = lambda w: (r >> (w.bit_length() - 1)) == (c >> (w.bit_length() - 1))
    cross = lambda w: ((r & w) != 0) & ((c & w) == 0)
    lev = jnp.full((HB, HB), len(HGRN_LEVELS), I32)
    for k, w in reversed(list(enumerate(HGRN_LEVELS[:-1]))):
        lev = jnp.where(same(2 * w) & cross(w), k + 1, lev)
    lev = jnp.where(same(LEAF) & (c <= r), 0, lev)
    return tri, lev


def _hgrn_gates(s, lb, lbc, tri):
    f = lbc + (1.0 - lb) * s
    logf = jnp.maximum(jnp.log(f), LOGF_MIN)
    kk = 1.0 - jnp.maximum(f, F_MIN)
    hi = logf.astype(BF16)
    mid = (logf - hi.astype(F32)).astype(BF16)
    dot = functools.partial(jnp.dot, preferred_element_type=F32)
    return kk, (dot(tri, hi) + dot(tri, mid)) * LOG2E


def _hgrn_scores(q, kk, cum, v, st_ref, reverse):
    def ref_row(idx):
        return cum[idx:idx + 1]

    total = ref_row(0) if reverse else ref_row(HB - 1)
    q_blk = q * jnp.exp2(cum)
    k_blk = kk * jnp.exp2(total - cum)

    ql, kl = [], []
    for c in range(HB // LEAF):
        rows = slice(c * LEAF, (c + 1) * LEAF)
        edge = (c + 1) * LEAF if reverse else c * LEAF - 1
        loc = (cum[rows] - ref_row(edge) if 0 <= edge < HB else cum[rows]) + LEAF_SHIFT
        ql.append(q[rows] * jnp.exp2(loc))
        kl.append(kk[rows] * jnp.exp2(-loc))
    pairs = [(jnp.concatenate(ql, axis=0), jnp.concatenate(kl, axis=0))]
    for w in HGRN_LEVELS:
        qp, kp = [], []
        zeros = jnp.zeros((w, HB), F32)
        for s0 in range(0, HB, 2 * w):
            first, second = slice(s0, s0 + w), slice(s0 + w, s0 + 2 * w)
            if reverse:
                ref = ref_row(s0 + w)
                qp += [q[first] * jnp.exp2(cum[first] - ref), zeros]
                kp += [zeros, kk[second] * jnp.exp2(ref - cum[second])]
            else:
                ref = ref_row(s0 + w - 1)
                qp += [zeros, q[second] * jnp.exp2(cum[second] - ref)]
                kp += [kk[first] * jnp.exp2(ref - cum[first]), zeros]
        pairs.append((jnp.concatenate(qp, axis=0), jnp.concatenate(kp, axis=0)))
    scores = [_nt(qq.astype(BF16), kq.astype(BF16)) for qq, kq in pairs]
    st = st_ref[...]
    o_state = _nt(q_blk.astype(BF16), st.astype(BF16))
    st_ref[...] = st * jnp.exp2(total) + lax.dot_general(
        v, k_blk.astype(BF16), (((0,), (0,)), ((), ())), preferred_element_type=F32)
    return scores, o_state


def _hgrn_output(scores, o_state, v, lev):
    a = scores[-1]
    for k in reversed(range(len(scores) - 1)):
        a = jnp.where(lev == k, scores[k], a)
    return jnp.dot(a.astype(BF16), v, preferred_element_type=F32) + o_state


def _hgrn_body(lb_ref, qf_ref, sf_ref, vf_ref, qr_ref, sr_ref, vr_ref, of_ref, or_ref, st_ref, *, th):
    @pl.when(pl.program_id(1) == 0)
    def _():
        st_ref[...] = jnp.zeros_like(st_ref)

    lb = lb_ref[...]
    lbc = jnp.maximum(lb, LB_FLOOR)
    masks = (_hgrn_masks(False), _hgrn_masks(True))
    nblk = th // HB
    streams = ((qf_ref, sf_ref, vf_ref, of_ref), (qr_ref, sr_ref, vr_ref, or_ref))

    def body(t, carry):
        rows = [pl.multiple_of((nblk - 1 - t if d else t) * HB, HB) for d in range(2)]
        units = [(d, slice(h * HGRN_DIM, (h + 1) * HGRN_DIM)) for d in range(2) for h in range(HGRN_HEADS)]
        gates = [_hgrn_gates(streams[d][1][0, pl.ds(rows[d], HB), sl].astype(F32),
                             lb[d:d + 1, sl], lbc[d:d + 1, sl], masks[d][0]) for d, sl in units]
        mids = [_hgrn_scores(streams[d][0][0, pl.ds(rows[d], HB), sl].astype(F32), kk, cum,
                             streams[d][2][0, pl.ds(rows[d], HB), sl], st_ref.at[d, sl.start // HGRN_DIM],
                             bool(d)) for (d, sl), (kk, cum) in zip(units, gates)]
        for (d, sl), (scores, o_state) in zip(units, mids):
            o = _hgrn_output(scores, o_state, streams[d][2][0, pl.ds(rows[d], HB), sl], masks[d][1])
            streams[d][3][0, pl.ds(rows[d], HB), sl] = o.astype(BF16)
        return carry

    lax.fori_loop(0, nblk, body, 0)


def _hgrn(qb, sf, sb, ib, lb):
    B, L, _ = qb.shape
    th = min(TH_HGRN, L)
    n = L // th
    fwd = pl.BlockSpec((1, th, HGRN_W), lambda b, i: (b, i, 0))
    rev = pl.BlockSpec((1, th, HGRN_W), lambda b, i: (b, n - 1 - i, 0))
    return pl.pallas_call(
        functools.partial(_hgrn_body, th=th),
        grid=(B, n),
        in_specs=[pl.BlockSpec((2, HGRN_W), lambda b, i: (0, 0)), fwd, fwd, fwd, rev, rev, rev],
        out_specs=[fwd, rev],
        out_shape=[jax.ShapeDtypeStruct((B, L, HGRN_W), BF16)] * 2,
        scratch_shapes=[pltpu.VMEM((2, HGRN_HEADS, HGRN_DIM, HGRN_DIM), F32)],
        compiler_params=pltpu.CompilerParams(
            dimension_semantics=("arbitrary", "arbitrary"), vmem_limit_bytes=VMEM_LIMIT),
        name="hgrn2",
    )(lb, qb, sf, ib, qb, sb, ib)


POST_PARTS = 2


def _post_mix(rows, x_ref, ya_ref, of_ref, or_ref, og_ref, ga_ref, gb_ref, wua_ref, wuh_ref, wo_ref,
              hg_ref, fg_ref, rw_ref, rb_ref, xn_ref, h2_ref):
    dot = functools.partial(jnp.dot, preferred_element_type=F32)
    o = of_ref[rows].astype(F32) + or_ref[rows].astype(F32)
    yh = (_rms(o) * hg_ref[...] * og_ref[rows].astype(F32)).astype(BF16)
    merged = (ga_ref[rows].astype(F32) * dot(ya_ref[rows], wua_ref[...])
              + gb_ref[rows].astype(F32) * dot(yh, wuh_ref[...]))
    xn = x_ref[rows] + dot(merged.astype(BF16), wo_ref[...])
    xn_ref[rows] = xn
    h2 = _rms(xn) * fg_ref[...]
    hi = h2.astype(BF16)
    h2_ref[rows] = hi
    lo = (h2 - hi.astype(F32)).astype(BF16)
    a = dot(hi, rw_ref[...])
    return a[:, :LANES] + a[:, LANES:] + dot(lo, rw_ref[:, :LANES]) + rb_ref[...]


def _post_route(logits):
    lane = lax.broadcasted_iota(I32, logits.shape, 1)
    lanef = lane.astype(F32)
    big = jnp.float32(1e9)
    ninf = jnp.float32(-jnp.inf)
    red = dict(axis=1, keepdims=True)
    gmask = lane < N_GROUPS
    mg = jnp.max(jnp.where(gmask, logits, ninf), **red)
    p_top = 1.0 / jnp.sum(jnp.where(gmask, jnp.exp(logits - mg), 0.0), **red)
    gi = jnp.min(jnp.where(gmask & (logits == mg), lanef, big), **red)
    emask = ((lane >= N_GROUPS) & (lane < N_GROUPS + N_EXPERTS)
             & (((lane - N_GROUPS) >> 3).astype(F32) == gi))
    le = jnp.where(emask, logits, ninf)
    m1 = jnp.max(le, **red)
    i1 = jnp.min(jnp.where(le == m1, lanef, big), **red)
    le2 = jnp.where(lanef == i1, ninf, le)
    m2 = jnp.max(le2, **red)
    i2 = jnp.min(jnp.where(le2 == m2, lanef, big), **red)
    r = jnp.exp(m2 - m1)
    w1 = 1.0 / (1.0 + r)
    w2 = r * w1
    e1 = i1 - N_GROUPS
    e2 = i2 - N_GROUPS
    is1, is2 = lanef == e1, lanef == e2
    return (e1, e2, p_top * w1, p_top * w2), is1, is2, jnp.where(is1 | is2, 1.0, 0.0)


def _post_body(x_ref, ya_ref, of_ref, or_ref, og_ref, ga_ref, gb_ref, wua_ref, wuh_ref, wo_ref,
               hg_ref, fg_ref, rw_ref, rb_ref, ltri_ref,
               xn_ref, h2_ref, rt_ref, cnt_ref):
    tm = x_ref.shape[0]
    part = tm // POST_PARTS
    parts = [slice(k * part, (k + 1) * part) for k in range(POST_PARTS)]
    logits = [_post_mix(rows, x_ref, ya_ref, of_ref, or_ref, og_ref, ga_ref, gb_ref, wua_ref, wuh_ref,
                        wo_ref, hg_ref, fg_ref, rw_ref, rb_ref, xn_ref, h2_ref) for rows in parts]
    routed = [_post_route(lg) for lg in logits]
    before = jnp.zeros((1, LANES), F32)
    red = dict(axis=1, keepdims=True)
    lane = lax.broadcasted_iota(I32, (part, LANES), 1)
    for rows, ((e1, e2, w1, w2), is1, is2, onehot) in zip(parts, routed):
        pref = jnp.dot(ltri_ref[...], onehot.astype(BF16), preferred_element_type=F32) + before
        before = before + jnp.sum(onehot, axis=0, keepdims=True)
        rank1 = jnp.sum(jnp.where(is1, pref, 0.0), **red)
        rank2 = jnp.sum(jnp.where(is2, pref, 0.0), **red)
        rt = jnp.zeros((part, LANES), F32)
        for idx, val in enumerate((e1, e2, rank1, rank2, w1, w2)):
            rt = jnp.where(lane == idx, val, rt)
        rt_ref[rows] = rt
    cnt_ref[...] = jnp.broadcast_to(before, cnt_ref.shape)


RT_E1, RT_E2, RT_RANK1, RT_RANK2, RT_W1, RT_W2 = range(6)
SUBLANES = 8


def _post(x, ya, of, orv, og, ga, gb, wua, wuh, wo, hg, fg, rw, rb, ltri):
    T, D = x.shape
    tm = ltri.shape[0] * POST_PARTS
    row = lambda w: pl.BlockSpec((tm, w), lambda i: (i, 0))
    full = lambda a: pl.BlockSpec(a.shape, lambda i: (0,) * a.ndim)
    return pl.pallas_call(
        _post_body,
        grid=(T // tm,),
        in_specs=[row(D), row(ATTN_Q_W), row(HGRN_W), row(HGRN_W), row(HGRN_W), row(D), row(D),
                  full(wua), full(wuh), full(wo), full(hg), full(fg), full(rw), full(rb), full(ltri)],
        out_specs=[row(D), row(D), row(LANES), pl.BlockSpec((SUBLANES, LANES), lambda i: (i, 0))],
        out_shape=[jax.ShapeDtypeStruct((T, D), F32), jax.ShapeDtypeStruct((T, D), BF16),
                   jax.ShapeDtypeStruct((T, LANES), F32),
                   jax.ShapeDtypeStruct((T // tm * SUBLANES, LANES), F32)],
        compiler_params=pltpu.CompilerParams(
            dimension_semantics=("arbitrary",), vmem_limit_bytes=VMEM_LIMIT),
        name="post_mixer_router",
    )(x, ya, of, orv, og, ga, gb, wua, wuh, wo, hg, fg, rw, rb, ltri)


XS_W = D_MODEL // 2 + LANES
SEG_BIG = 32
META_DST, META_LOFF, META_NBIG, META_NSMALL, META_TOTAL8 = (k * N_EXPERTS for k in range(5))
META_W = 5 * N_EXPERTS


def _local_rows(tm):
    return 2 * tm + N_EXPERTS * SUBLANES


def _local_pos(rt, loff_row):
    lanef = lax.broadcasted_iota(I32, rt.shape, 1).astype(F32)
    out = []
    for e_lane, r_lane in ((RT_E1, RT_RANK1), (RT_E2, RT_RANK2)):
        e = rt[:, e_lane:e_lane + 1]
        off = jnp.sum(jnp.where(lanef == e, loff_row, 0.0), axis=1, keepdims=True)
        out.append(off + rt[:, r_lane:r_lane + 1])
    return out


def _one_hot(hit):
    return jnp.where(hit, 1.0, 0.0).astype(BF16)


def _segment_starts(meta_ref, make_copy):
    for e in range(N_EXPERTS):
        dst = meta_ref[0, 0, META_DST + e]
        loff = meta_ref[0, 0, META_LOFF + e]
        n_big = meta_ref[0, 0, META_NBIG + e]
        n_small = meta_ref[0, 0, META_NSMALL + e]

        def big(k, c):
            o = k * SEG_BIG
            make_copy(pl.multiple_of(loff + o, SUBLANES), pl.multiple_of(dst + o, SUBLANES), SEG_BIG).start()
            return c

        lax.fori_loop(0, n_big, big, 0)

        def small(k, c):
            o = n_big * SEG_BIG + k * SUBLANES
            make_copy(pl.multiple_of(loff + o, SUBLANES), pl.multiple_of(dst + o, SUBLANES), SUBLANES).start()
            return c

        lax.fori_loop(0, n_small, small, 0)


def _segment_wait(meta_ref, make_copy, rows):
    total8 = meta_ref[0, 0, META_TOTAL8]
    for bit in range((rows // SUBLANES).bit_length()):
        @pl.when((total8 & (1 << bit)) != 0)
        def _():
            make_copy(0, 0, SUBLANES << bit).wait()


TAIL_PIECES = tuple(SUBLANES << b for b in range((MOE_BLK // SUBLANES).bit_length() - 1))


def _tail_copies(tail_ref, make_copy, action):
    def per_expert(e, c):
        start = tail_ref[0, 0, e]
        n8 = tail_ref[0, 0, N_EXPERTS + e]
        for rows in TAIL_PIECES:
            bit = rows // SUBLANES

            @pl.when((n8 & bit) != 0)
            def _():
                off = (n8 & ~(2 * bit - 1)) * SUBLANES
                action(make_copy(0, pl.multiple_of(start + off, SUBLANES), rows))
        return c

    lax.fori_loop(0, N_EXPERTS, per_expert, 0)

    def per_block(b, c):
        action(make_copy(0, pl.multiple_of(b * MOE_BLK, MOE_BLK), MOE_BLK))
        return c

    lax.fori_loop(tail_ref[0, 0, 2 * N_EXPERTS], tail_ref[0, 0, 2 * N_EXPERTS + 1], per_block, 0)


def _dispatch_body(meta_ref, prev_meta_ref, tail_ref, loff_ref, rt_ref, h_ref, xs_ref, buf_ref, sem):
    i = pl.program_id(0)
    last = pl.num_programs(0) - 1
    slot = i % 2
    tm = rt_ref.shape[0]
    rows = buf_ref.shape[1]
    rt = rt_ref[...]
    dot = functools.partial(jnp.dot, preferred_element_type=F32)
    lane = lax.broadcasted_iota(I32, (tm, LANES), 1)
    l1, l2 = _local_pos(rt, loff_ref[0])
    parts = []
    for pos in (l1, l2):
        hi = jnp.floor(pos * (1.0 / 32.0))
        parts += [hi, pos - 32.0 * hi]
    cols = jnp.zeros((tm, LANES), F32)
    for k, part in enumerate(parts):
        cols = jnp.where(lane == k, part, cols)
    eye = _one_hot(lax.broadcasted_iota(I32, (SUBLANES, LANES), 0) == lax.broadcasted_iota(I32, (SUBLANES, LANES), 1))
    lanes = _nt(eye, cols.astype(BF16))
    r1 = (32.0 * lanes[0:1] + lanes[1:2]).astype(I32)
    r2 = (32.0 * lanes[2:3] + lanes[3:4]).astype(I32)
    row = lax.broadcasted_iota(I32, (rows, tm), 0)
    p1, p2 = _one_hot(row == r1), _one_hot(row == r2)
    xl = dot(p1 + p2, h_ref[...])
    half = D_MODEL // 2
    buf_ref[slot, :, :half] = _pack_pair(xl[:, :half], xl[:, half:])

    def wcols(w):
        hi = w.astype(BF16).astype(F32)
        return jnp.where(lane == 0, hi, jnp.where(lane == 1, w - hi, 0.0)).astype(BF16)

    wl = dot(p1, wcols(rt[:, RT_W1:RT_W1 + 1])) + dot(p2, wcols(rt[:, RT_W2:RT_W2 + 1]))
    buf_ref[slot, :, half:] = lax.bitcast_convert_type(wl, U32)

    def copier(k):
        def make_copy(src_row, dst_row, n):
            return pltpu.make_async_copy(buf_ref.at[k, pl.ds(src_row, n)], xs_ref.at[pl.ds(dst_row, n)],
                                         sem.at[k])
        return make_copy

    _segment_starts(meta_ref, copier(slot))

    @pl.when(i > 0)
    def _():
        _segment_wait(prev_meta_ref, copier(1 - slot), rows)

    @pl.when(i == last)
    def _():
        _segment_wait(meta_ref, copier(slot), rows)
        buf_ref[slot, :MOE_BLK] = jnp.zeros((MOE_BLK, XS_W), U32)
        _tail_copies(tail_ref, copier(slot), lambda cp: cp.start())
        _tail_copies(tail_ref, copier(slot), lambda cp: cp.wait())


def _dispatch(meta, tail, loff, rt, h2, n_pad):
    T, D = h2.shape
    nt = meta.shape[0]
    tm = T // nt
    return pl.pallas_call(
        _dispatch_body,
        grid=(nt,),
        in_specs=[pl.BlockSpec((1, 1, meta.shape[2]), lambda i: (i, 0, 0), memory_space=pltpu.SMEM),
                  pl.BlockSpec((1, 1, meta.shape[2]), lambda i: (jnp.maximum(i - 1, 0), 0, 0),
                               memory_space=pltpu.SMEM),
                  pl.BlockSpec((1, 1, tail.shape[2]), lambda i: (0, 0, 0), memory_space=pltpu.SMEM),
                  pl.BlockSpec((1, 1, LANES), lambda i: (i, 0, 0)),
                  pl.BlockSpec((tm, LANES), lambda i: (i, 0)),
                  pl.BlockSpec((tm, D), lambda i: (i, 0))],
        out_specs=pl.BlockSpec(memory_space=pl.ANY),
        out_shape=jax.ShapeDtypeStruct((n_pad, XS_W), U32),
        scratch_shapes=[pltpu.VMEM((2, _local_rows(tm), XS_W), U32), pltpu.SemaphoreType.DMA((2,))],
        compiler_params=pltpu.CompilerParams(
            dimension_semantics=("arbitrary",), vmem_limit_bytes=VMEM_LIMIT),
        name="moe_dispatch",
    )(meta, meta, tail, loff, rt, h2)


def _expert_body(be_ref, nused_ref, xs_ref, wg_ref, wu_ref, wd_ref, ys_ref, wgb_ref, wub_ref, wdb_ref):
    i = pl.program_id(0)

    @pl.when((i == 0) | (be_ref[i] != be_ref[jnp.maximum(i - 1, 0)]))
    def _():
        wgb_ref[...] = wg_ref[0].astype(BF16)
        wub_ref[...] = wu_ref[0].astype(BF16)
        wdb_ref[...] = wd_ref[0].astype(BF16)

    @pl.when(i >= nused_ref[0])
    def _():
        ys_ref[...] = jnp.zeros_like(ys_ref)

    @pl.when(i < nused_ref[0])
    def _():
        dot = functools.partial(jnp.dot, preferred_element_type=F32)
        half = D_MODEL // 2
        ha, hb = _unpack_pair(xs_ref[:, :half])
        h = jnp.concatenate([ha, hb], axis=1).astype(BF16)
        wl = lax.bitcast_convert_type(xs_ref[:, half:], F32)
        w = wl[:, 0:1] + wl[:, 1:2]
        g = dot(h, wgb_ref[...])
        u = dot(h, wub_ref[...])
        y = dot((g * _sigmoid(g) * u).astype(BF16), wdb_ref[...]) * w
        ys_ref[...] = _pack_pair(y[:, :half], y[:, half:])


def _experts(block_expert, nused, xs, wg, wu, wd):
    n_pad = xs.shape[0]
    blk_in = pl.BlockSpec((MOE_BLK, XS_W), lambda i, be, nu: (jnp.minimum(i, nu[0] - 1), 0))
    blk_out = pl.BlockSpec((MOE_BLK, D_MODEL // 2), lambda i, be, nu: (i, 0))
    wspec = lambda a: pl.BlockSpec((1,) + a.shape[1:], lambda i, be, nu: (be[i], 0, 0))
    return pl.pallas_call(
        _expert_body,
        grid_spec=pltpu.PrefetchScalarGridSpec(
            num_scalar_prefetch=2,
            grid=(n_pad // MOE_BLK,),
            in_specs=[blk_in, wspec(wg), wspec(wu), wspec(wd)],
            out_specs=blk_out,
            scratch_shapes=[pltpu.VMEM(wg.shape[1:], BF16), pltpu.VMEM(wu.shape[1:], BF16),
                            pltpu.VMEM(wd.shape[1:], BF16)],
        ),
        out_shape=jax.ShapeDtypeStruct((n_pad, D_MODEL // 2), U32),
        compiler_params=pltpu.CompilerParams(
            dimension_semantics=("arbitrary",), vmem_limit_bytes=VMEM_LIMIT),
        name="moe_experts",
    )(block_expert, nused, xs, wg, wu, wd)


def _combine_body(meta_ref, next_meta_ref, loff_ref, rt_ref, x_ref, fg_ref, ys_ref, o_ref, buf_ref, sem,
                  *, final_norm):
    i = pl.program_id(0)
    slot = i % 2

    def copier(k):
        def make_copy(loc_row, glob_row, n):
            return pltpu.make_async_copy(ys_ref.at[pl.ds(glob_row, n)], buf_ref.at[k, pl.ds(loc_row, n)],
                                         sem.at[k])
        return make_copy

    rows = buf_ref.shape[1]

    @pl.when(i == 0)
    def _():
        buf_ref[...] = jnp.zeros_like(buf_ref)
        _segment_starts(meta_ref, copier(0))

    @pl.when(i + 1 < pl.num_programs(0))
    def _():
        _segment_starts(next_meta_ref, copier(1 - slot))

    l1, l2 = _local_pos(rt_ref[...], loff_ref[0])
    col = lax.broadcasted_iota(I32, (rt_ref.shape[0], rows), 1)
    pt = _one_hot(col == l1.astype(I32)) + _one_hot(col == l2.astype(I32))
    _segment_wait(meta_ref, copier(slot), rows)
    yl = jnp.concatenate(_unpack_pair(buf_ref[slot]), axis=1).astype(BF16)
    out = x_ref[...] + jnp.dot(pt, yl, preferred_element_type=F32)
    if final_norm:
        out = _rms(out) * fg_ref[...]
    o_ref[...] = out


def _combine(meta, loff, rt, x, fg, ys, final_norm):
    T, D = x.shape
    nt = meta.shape[0]
    tm = T // nt
    return pl.pallas_call(
        functools.partial(_combine_body, final_norm=final_norm),
        grid=(nt,),
        in_specs=[pl.BlockSpec((1, 1, meta.shape[2]), lambda i: (i, 0, 0), memory_space=pltpu.SMEM),
                  pl.BlockSpec((1, 1, meta.shape[2]), lambda i: (jnp.minimum(i + 1, nt - 1), 0, 0),
                               memory_space=pltpu.SMEM),
                  pl.BlockSpec((1, 1, LANES), lambda i: (i, 0, 0)),
                  pl.BlockSpec((tm, LANES), lambda i: (i, 0)),
                  pl.BlockSpec((tm, D), lambda i: (i, 0)),
                  pl.BlockSpec((1, D), lambda i: (0, 0)),
                  pl.BlockSpec(memory_space=pl.ANY)],
        out_specs=pl.BlockSpec((tm, D), lambda i: (i, 0)),
        out_shape=jax.ShapeDtypeStruct((T, D), F32),
        scratch_shapes=[pltpu.VMEM((2, _local_rows(tm), D // 2), U32), pltpu.SemaphoreType.DMA((2,))],
        compiler_params=pltpu.CompilerParams(
            dimension_semantics=("arbitrary",), vmem_limit_bytes=VMEM_LIMIT),
        name="moe_combine",
    )(meta, meta, loff, rt, x, fg, ys)


def _rope_tables(L):
    half = HEAD_DIM // 2
    inv = ROPE_THETA ** (-jnp.arange(half, dtype=F32) / half)
    ang = jnp.arange(L, dtype=F32)[:, None] * inv[None, :]
    lane = np.arange(LANES)
    idx = lane % half
    first = jnp.asarray((lane % HEAD_DIM) < half)
    cos = jnp.cos(ang)[:, idx]
    sin = jnp.sin(ang)[:, idx]
    return cos, jnp.where(first, -sin, 0.0), jnp.where(first, 0.0, sin)


def _moe_layout(cnt, T, layer):
    nt = cnt.shape[0] // SUBLANES
    c = cnt.reshape(nt, SUBLANES, LANES)[:, 0, :N_EXPERTS].astype(I32)
    c8 = (c + SUBLANES - 1) // SUBLANES * SUBLANES
    tile_off = jnp.cumsum(c8, axis=0) - c8
    total = jnp.sum(c8, axis=0)
    padded = (total + MOE_BLK - 1) // MOE_BLK * MOE_BLK
    pend = jnp.cumsum(padded)
    dst = (pend - padded)[None, :] + tile_off
    loff = jnp.cumsum(c8, axis=1) - c8
    n8 = c8 // SUBLANES
    per_big = SEG_BIG // SUBLANES
    total8 = jnp.broadcast_to(jnp.sum(n8, axis=1, keepdims=True), n8.shape)
    meta = jnp.concatenate([dst, loff, n8 // per_big, n8 % per_big, total8], axis=1).reshape(nt, 1, META_W)
    loff_f = jnp.pad(loff.astype(F32), ((0, 0), (0, LANES - N_EXPERTS))).reshape(nt, 1, LANES)
    n_pad = 2 * T + nt * N_EXPERTS * SUBLANES + N_EXPERTS * MOE_BLK
    n_pad = (n_pad + MOE_BLK - 1) // MOE_BLK * MOE_BLK
    starts = jnp.arange(n_pad // MOE_BLK, dtype=I32) * MOE_BLK
    block_expert = jnp.minimum(jnp.sum(starts[:, None] >= pend[None, :], axis=1), N_EXPERTS - 1)
    block_expert = block_expert.astype(I32) + layer * N_EXPERTS
    nused = (pend[-1:] // MOE_BLK).astype(I32)
    tail = jnp.concatenate([pend - padded + total, (padded - total) // SUBLANES, nused,
                            jnp.full((1,), n_pad // MOE_BLK, I32)]).reshape(1, 1, 2 * N_EXPERTS + 2)
    return meta, tail, loff_f, block_expert, nused, n_pad


def kernel(x, attn_norm_g, w_in, sink_logits, hgrn_lb_table, hgrn_norm_g, w_up_attn, w_up_hgrn, w_out,
           ffn_norm_g, w_router_group, b_router_group, w_router_expert, b_router_expert,
           w_expert_gate, w_expert_up, w_expert_down, final_norm_g):
    B, L, D = x.shape
    T = B * L
    depth = w_in.shape[0]
    sm = jax.nn.softmax(hgrn_lb_table.astype(F32), axis=0)
    lower_bounds = jnp.cumsum(sm, axis=0) - sm[:1]
    cos, s1, s2 = _rope_tables(L)
    tm_post = min(TM_POST, T)
    ltri = jnp.tril(jnp.ones((tm_post // POST_PARTS,) * 2, BF16), -1)
    fin_g = final_norm_g.reshape(1, D).astype(F32)
    w_in_bf16 = w_in.astype(BF16)
    all_layers = lambda w: w.reshape((depth * N_EXPERTS,) + w.shape[2:])
    wg_all, wu_all, wd_all = all_layers(w_expert_gate), all_layers(w_expert_up), all_layers(w_expert_down)

    for l in range(depth):
        qa, ka, va, qb, sf, sb, ib, og, ga, gb = _inproj(
            x, attn_norm_g[l].reshape(1, D), w_in_bf16, l, cos, s1, s2)
        ya = _attention(qa, ka, va, sink_logits[l].astype(F32))
        of, orv = _hgrn(qb, sf, sb, ib, lower_bounds[l])

        rw = jnp.concatenate(
            [w_router_group[l], jnp.transpose(w_router_expert[l], (1, 0, 2)).reshape(D, N_EXPERTS)], axis=1)
        rw = jnp.pad(rw.astype(F32), ((0, 0), (0, LANES - rw.shape[1])))
        rwh = rw.astype(BF16)
        rw2 = jnp.concatenate([rwh, (rw - rwh.astype(F32)).astype(BF16)], axis=1)
        rb = jnp.concatenate([b_router_group[l], b_router_expert[l].reshape(-1)]).astype(F32)
        rb = jnp.pad(rb, (0, LANES - rb.shape[0])).reshape(1, LANES)
        flat = lambda a: a.reshape(T, a.shape[-1])
        xn, h2, rt, cnt = _post(
            flat(x), flat(ya), flat(of), flat(orv), flat(og), flat(ga), flat(gb),
            w_up_attn[l].astype(BF16), w_up_hgrn[l].astype(BF16), w_out[l].astype(BF16),
            hgrn_norm_g[l].reshape(1, HGRN_W).astype(F32), ffn_norm_g[l].reshape(1, D).astype(F32),
            rw2, rb, ltri)

        meta, tail, loff, block_expert, nused, n_pad = _moe_layout(cnt, T, l)
        xs = _dispatch(meta, tail, loff, rt, h2, n_pad)
        ys = _experts(block_expert, nused, xs, wg_all, wu_all, wd_all)
        x = _combine(meta, loff, rt, xn, fin_g, ys, l == depth - 1).reshape(B, L, D)
    return x
```

```python
import functools

import numpy as np
import jax
import jax.numpy as jnp
from jax import lax
from jax.experimental import pallas as pl
from jax.experimental.pallas import tpu as pltpu

F32 = jnp.float32
BF16 = jnp.bfloat16
U32 = jnp.uint32
I32 = jnp.int32

D_MODEL = 1024
N_Q_HEADS = 8
N_KV_HEADS = 2
HEAD_DIM = 64
WINDOW = 128
ROPE_THETA = 10000.0
MASK_VALUE = -1e30
HGRN_HEADS = 4
HGRN_DIM = 128
LOGF_MIN = -4.0
F_MIN = float(np.exp(LOGF_MIN))
LOG2E = float(np.log2(np.e))
Q_SCALE = HEAD_DIM ** -0.5 * LOG2E
LB_FLOOR = 1e-30
N_GROUPS = 4
EXPERTS_PER_GROUP = 8
N_EXPERTS = N_GROUPS * EXPERTS_PER_GROUP
D_EXPERT = 512
NORM_EPS = 1e-6

ATTN_Q_W = N_Q_HEADS * HEAD_DIM
ATTN_KV_W = N_KV_HEADS * HEAD_DIM
HGRN_W = HGRN_HEADS * HGRN_DIM
OFF_QA = 0
OFF_KA = OFF_QA + ATTN_Q_W
OFF_VA = OFF_KA + ATTN_KV_W
OFF_QB = OFF_VA + ATTN_KV_W
OFF_ZF = OFF_QB + HGRN_W
OFF_ZB = OFF_ZF + HGRN_W
OFF_IB = OFF_ZB + HGRN_W
OFF_OG = OFF_IB + HGRN_W
OFF_GA = OFF_OG + HGRN_W
OFF_GB = OFF_GA + D_MODEL

LANES = 128
VMEM_LIMIT = 56 * 1024 * 1024

TM_PROJ = 512
TQ_ATTN = 1024
TH_HGRN = 1024
TM_POST = 512
MOE_BLK = 512


def _sigmoid(z):
    return 1.0 / (1.0 + jnp.exp(-z))


def _rms(x):
    return x * lax.rsqrt(jnp.mean(x * x, axis=-1, keepdims=True) + NORM_EPS)


def _nt(a, b):
    return lax.dot_general(a, b, (((1,), (1,)), ((), ())), preferred_element_type=F32)


def _pack_pair(a, b):
    ua = lax.bitcast_convert_type(a.astype(BF16).astype(F32), U32)
    ub = lax.bitcast_convert_type(b.astype(BF16).astype(F32), U32)
    return ua | (ub >> 16)


def _unpack_pair(u):
    hi = lax.bitcast_convert_type(u & jnp.uint32(0xFFFF0000), F32)
    lo = lax.bitcast_convert_type(u << 16, F32)
    return hi, lo


def _inproj_body(x_ref, g_ref, w_ref, cos_ref, s1_ref, s2_ref,
                 qa_ref, ka_ref, va_ref, qb_ref, sf_ref, sb_ref, ib_ref, og_ref, ga_ref, gb_ref):
    h = (_rms(x_ref[0]) * g_ref[...]).astype(BF16)

    def proj(off, width):
        return jnp.dot(h, w_ref[:, off:off + width], preferred_element_type=F32)

    cos, s1, s2 = cos_ref[...], s1_ref[...], s2_ref[...]

    def rope(t):
        return t * cos + pltpu.roll(t, 96, 1) * s1 + pltpu.roll(t, 32, 1) * s2

    q = proj(OFF_QA, ATTN_Q_W)
    for c in range(ATTN_Q_W // LANES):
        sl = slice(c * LANES, (c + 1) * LANES)
        qa_ref[0, :, sl] = (rope(q[:, sl]) * Q_SCALE).astype(BF16)
    k = rope(proj(OFF_KA, ATTN_KV_W))
    ka_ref[0, :, :LANES] = k.astype(BF16)
    ka_ref[0, :, LANES:] = pltpu.roll(k, 64, 1).astype(BF16)
    v = proj(OFF_VA, ATTN_KV_W)
    va_ref[0, :, :LANES] = v.astype(BF16)
    va_ref[0, :, LANES:] = pltpu.roll(v, 64, 1).astype(BF16)
    t = proj(OFF_QB, HGRN_W)
    qb_ref[0] = (t * _sigmoid(t)).astype(BF16)
    sf_ref[0] = _sigmoid(proj(OFF_ZF, HGRN_W)).astype(BF16)
    sb_ref[0] = _sigmoid(proj(OFF_ZB, HGRN_W)).astype(BF16)
    ib_ref[0] = proj(OFF_IB, HGRN_W).astype(BF16)
    t = proj(OFF_OG, HGRN_W)
    og_ref[0] = (t * _sigmoid(t)).astype(BF16)
    ga_ref[0] = _sigmoid(proj(OFF_GA, D_MODEL)).astype(BF16)
    gb_ref[0] = _sigmoid(proj(OFF_GB, D_MODEL)).astype(BF16)


def _inproj(x, g, w_all, layer, cos, s1, s2):
    B, L, D = x.shape
    tm = min(TM_PROJ, L)
    widths = (ATTN_Q_W, 2 * ATTN_KV_W, 2 * ATTN_KV_W, HGRN_W, HGRN_W, HGRN_W, HGRN_W, HGRN_W,
              D_MODEL, D_MODEL)
    tab = pl.BlockSpec((tm, LANES), lambda b, i: (i, 0))
    return pl.pallas_call(
        _inproj_body,
        grid=(B, L // tm),
        in_specs=[
            pl.BlockSpec((1, tm, D), lambda b, i: (b, i, 0)),
            pl.BlockSpec((1, D), lambda b, i: (0, 0)),
            pl.BlockSpec((None,) + w_all.shape[1:], lambda b, i: (layer, 0, 0)),
            tab, tab, tab,
        ],
        out_specs=[pl.BlockSpec((1, tm, w), lambda b, i: (b, i, 0)) for w in widths],
        out_shape=[jax.ShapeDtypeStruct((B, L, w), BF16) for w in widths],
        compiler_params=pltpu.CompilerParams(
            dimension_semantics=("arbitrary", "arbitrary"), vmem_limit_bytes=VMEM_LIMIT),
        name="inproj",
    )(x, g, w_all, cos, s1, s2)


def _attn_body(sink_ref, q_ref, kp_ref, km_ref, kn_ref, vp_ref, vm_ref, vn_ref, o_ref, *, tq, seq):
    i = pl.program_id(1)
    kw = jnp.concatenate([kp_ref[0], km_ref[0], kn_ref[0]], axis=0)
    vw = jnp.concatenate([vp_ref[0], vm_ref[0], vn_ref[0]], axis=0)
    lo = lax.broadcasted_iota(I32, (1, LANES), 1) < HEAD_DIM
    zero = jnp.zeros((), BF16)

    def variants(t):
        a, b = t[:, :LANES], t[:, LANES:]
        return ((jnp.where(lo, a, zero), jnp.where(lo, zero, b)),
                (jnp.where(lo, b, zero), jnp.where(lo, zero, a)))

    kvar, vvar = variants(kw), variants(vw)
    row = lax.broadcasted_iota(I32, (WINDOW, 3 * WINDOW), 0)
    col = lax.broadcasted_iota(I32, (WINDOW, 3 * WINDOW), 1)
    band = (col >= row) & (col <= row + 2 * WINDOW)
    ncol = ATTN_Q_W // LANES
    group = lambda c: (2 * c) // (N_Q_HEADS // N_KV_HEADS)
    heads = [(c, hh) for c in range(ncol) for hh in range(2)]
    for j in range(tq // WINDOW):
        kpos = col + (i * tq + (j - 1) * WINDOW)
        bias = jnp.where(band & (kpos >= 0) & (kpos < seq), 0.0, MASK_VALUE)
        rows = slice(j * WINDOW, (j + 1) * WINDOW)
        win = slice(j * WINDOW, (j + 3) * WINDOW)
        scores = [_nt(q_ref[0, rows, c * LANES:(c + 1) * LANES], kvar[group(c)][hh][win]) for c, hh in heads]
        probs, denoms = [], []
        for (c, hh), s in zip(heads, scores):
            s = jnp.concatenate([s[:, :WINDOW] + bias[:, :WINDOW], s[:, WINDOW:2 * WINDOW],
                                 s[:, 2 * WINDOW:] + bias[:, 2 * WINDOW:]], axis=1)
            sk = sink_ref[2 * c + hh]
            m = jnp.maximum(jnp.max(s, axis=1, keepdims=True), sk)
            p = jnp.exp2(s - m)
            denoms.append(jnp.sum(p, axis=1, keepdims=True) + jnp.exp2(sk - m))
            probs.append(p.astype(BF16))
        for c in range(ncol):
            g = group(c)
            pv = jnp.dot(jnp.concatenate(probs[2 * c:2 * c + 2], axis=1),
                         jnp.concatenate([vvar[g][0][win], vvar[g][1][win]], axis=0),
                         preferred_element_type=F32)
            inv = jnp.where(lo, 1.0 / denoms[2 * c], 1.0 / denoms[2 * c + 1])
            o_ref[0, rows, c * LANES:(c + 1) * LANES] = (pv * inv).astype(BF16)


def _attention(qa, ka, va, sink):
    B, L, _ = qa.shape
    tq = min(TQ_ATTN, L)
    r = tq // WINDOW
    nb = L // WINDOW
    kvw = 2 * ATTN_KV_W
    main = pl.BlockSpec((1, tq, kvw), lambda b, i: (b, i, 0))
    prev = pl.BlockSpec((1, WINDOW, kvw), lambda b, i: (b, jnp.maximum(i * r - 1, 0), 0))
    nxt = pl.BlockSpec((1, WINDOW, kvw), lambda b, i: (b, jnp.minimum(i * r + r, nb - 1), 0))
    return pl.pallas_call(
        functools.partial(_attn_body, tq=tq, seq=L),
        grid=(B, L // tq),
        in_specs=[
            pl.BlockSpec(memory_space=pltpu.SMEM),
            pl.BlockSpec((1, tq, ATTN_Q_W), lambda b, i: (b, i, 0)),
            prev, main, nxt, prev, main, nxt,
        ],
        out_specs=pl.BlockSpec((1, tq, ATTN_Q_W), lambda b, i: (b, i, 0)),
        out_shape=jax.ShapeDtypeStruct((B, L, ATTN_Q_W), BF16),
        compiler_params=pltpu.CompilerParams(
            dimension_semantics=("arbitrary", "arbitrary"), vmem_limit_bytes=VMEM_LIMIT),
        name="win_attn",
    )(sink, qa, ka, ka, ka, va, va, va)


HB = 128
LEAF = 32
HGRN_LEVELS = tuple(LEAF << k for k in range((HB // LEAF).bit_length() - 1))
LEAF_SHIFT = 0.5 * LEAF * (-LOGF_MIN) * LOG2E


def _hgrn_masks(reverse):
    r = lax.broadcasted_iota(I32, (HB, HB), 0)
    c = lax.broadcasted_iota(I32, (HB, HB), 1)
    if reverse:
        r, c = c, r
    tri = jnp.where(c <= r, 1.0, 0.0).astype(BF16)
    same = lambda w: (r >> (w.bit_length() - 1)) == (c >> (w.bit_length() - 1))
    cross = lambda w: ((r & w) != 0) & ((c & w) == 0)
    lev = jnp.full((HB, HB), len(HGRN_LEVELS), I32)
    for k, w in reversed(list(enumerate(HGRN_LEVELS[:-1]))):
        lev = jnp.where(same(2 * w) & cross(w), k + 1, lev)
    lev = jnp.where(same(LEAF) & (c <= r), 0, lev)
    return tri, lev


def _hgrn_gates(s, lb, lbc, tri):
    f = lbc + (1.0 - lb) * s
    logf = jnp.maximum(jnp.log(f), LOGF_MIN)
    kk = 1.0 - jnp.maximum(f, F_MIN)
    hi = logf.astype(BF16)
    mid = (logf - hi.astype(F32)).astype(BF16)
    dot = functools.partial(jnp.dot, preferred_element_type=F32)
    return kk, (dot(tri, hi) + dot(tri, mid)) * LOG2E


def _hgrn_scores(q, kk, cum, v, st_ref, reverse):
    def ref_row(idx):
        return cum[idx:idx + 1]

    total = ref_row(0) if reverse else ref_row(HB - 1)
    q_blk = q * jnp.exp2(cum)
    k_blk = kk * jnp.exp2(total - cum)

    ql, kl = [], []
    for c in range(HB // LEAF):
        rows = slice(c * LEAF, (c + 1) * LEAF)
        edge = (c + 1) * LEAF if reverse else c * LEAF - 1
        loc = (cum[rows] - ref_row(edge) if 0 <= edge < HB else cum[rows]) + LEAF_SHIFT
        ql.append(q[rows] * jnp.exp2(loc))
        kl.append(kk[rows] * jnp.exp2(-loc))
    pairs = [(jnp.concatenate(ql, axis=0), jnp.concatenate(kl, axis=0))]
    for w in HGRN_LEVELS:
        qp, kp = [], []
        zeros = jnp.zeros((w, HB), F32)
        for s0 in range(0, HB, 2 * w):
            first, second = slice(s0, s0 + w), slice(s0 + w, s0 + 2 * w)
            if reverse:
                ref = ref_row(s0 + w)
                qp += [q[first] * jnp.exp2(cum[first] - ref), zeros]
                kp += [zeros, kk[second] * jnp.exp2(ref - cum[second])]
            else:
                ref = ref_row(s0 + w - 1)
                qp += [zeros, q[second] * jnp.exp2(cum[second] - ref)]
                kp += [kk[first] * jnp.exp2(ref - cum[first]), zeros]
        pairs.append((jnp.concatenate(qp, axis=0), jnp.concatenate(kp, axis=0)))
    scores = [_nt(qq.astype(BF16), kq.astype(BF16)) for qq, kq in pairs]
    st = st_ref[...]
    o_state = _nt(q_blk.astype(BF16), st.astype(BF16))
    st_ref[...] = st * jnp.exp2(total) + lax.dot_general(
        v, k_blk.astype(BF16), (((0,), (0,)), ((), ())), preferred_element_type=F32)
    return scores, o_state


def _hgrn_output(scores, o_state, v, lev):
    a = scores[-1]
    for k in reversed(range(len(scores) - 1)):
        a = jnp.where(lev == k, scores[k], a)
    return jnp.dot(a.astype(BF16), v, preferred_element_type=F32) + o_state


def _hgrn_body(lb_ref, qf_ref, sf_ref, vf_ref, qr_ref, sr_ref, vr_ref, of_ref, or_ref, st_ref, *, th):
    @pl.when(pl.program_id(1) == 0)
    def _():
        st_ref[...] = jnp.zeros_like(st_ref)

    lb = lb_ref[...]
    lbc = jnp.maximum(lb, LB_FLOOR)
    masks = (_hgrn_masks(False), _hgrn_masks(True))
    nblk = th // HB
    streams = ((qf_ref, sf_ref, vf_ref, of_ref), (qr_ref, sr_ref, vr_ref, or_ref))

    def body(t, carry):
        rows = [pl.multiple_of((nblk - 1 - t if d else t) * HB, HB) for d in range(2)]
        units = [(d, h * HGRN_DIM) for d in range(2) for h in range(HGRN_HEADS)]

        def load(d, k, lane0):
            return streams[d][k][0, pl.ds(rows[d], HB), lane0:lane0 + HGRN_DIM]

        gates = [_hgrn_gates(load(d, 1, c0).astype(F32), lb[d:d + 1, c0:c0 + HGRN_DIM],
                             lbc[d:d + 1, c0:c0 + HGRN_DIM], masks[d][0]) for d, c0 in units]
        mids = [_hgrn_scores(load(d, 0, c0).astype(F32), kk, cum, load(d, 2, c0),
                             st_ref.at[d, c0 // HGRN_DIM], bool(d))
                for (d, c0), (kk, cum) in zip(units, gates)]
        for (d, c0), (scores, o_state) in zip(units, mids):
            o = _hgrn_output(scores, o_state, load(d, 2, c0), masks[d][1])
            streams[d][3][0, pl.ds(rows[d], HB), c0:c0 + HGRN_DIM] = o.astype(BF16)
        return carry

    lax.fori_loop(0, nblk, body, 0)


def _hgrn(qb, sf, sb, ib, lb):
    B, L, _ = qb.shape
    th = min(TH_HGRN, L)
    n = L // th
    fwd = pl.BlockSpec((1, th, HGRN_W), lambda b, i: (b, i, 0))
    rev = pl.BlockSpec((1, th, HGRN_W), lambda b, i: (b, n - 1 - i, 0))
    return pl.pallas_call(
        functools.partial(_hgrn_body, th=th),
        grid=(B, n),
        in_specs=[pl.BlockSpec((2, HGRN_W), lambda b, i: (0, 0)), fwd, fwd, fwd, rev, rev, rev],
        out_specs=[fwd, rev],
        out_shape=[jax.ShapeDtypeStruct((B, L, HGRN_W), BF16)] * 2,
        scratch_shapes=[pltpu.VMEM((2, HGRN_HEADS, HGRN_DIM, HGRN_DIM), F32)],
        compiler_params=pltpu.CompilerParams(
            dimension_semantics=("arbitrary", "arbitrary"), vmem_limit_bytes=VMEM_LIMIT),
        name="hgrn2",
    )(lb, qb, sf, ib, qb, sb, ib)


POST_PARTS = 2


def _post_mix(rows, x_ref, ya_ref, of_ref, or_ref, og_ref, ga_ref, gb_ref, wua_ref, wuh_ref, wo_ref,
              hg_ref, fg_ref, rw_ref, rb_ref, xn_ref, h2_ref):
    dot = functools.partial(jnp.dot, preferred_element_type=F32)
    o = of_ref[rows].astype(F32) + or_ref[rows].astype(F32)
    yh = (_rms(o) * hg_ref[...] * og_ref[rows].astype(F32)).astype(BF16)
    merged = (ga_ref[rows].astype(F32) * dot(ya_ref[rows], wua_ref[...])
              + gb_ref[rows].astype(F32) * dot(yh, wuh_ref[...]))
    xn = x_ref[rows] + dot(merged.astype(BF16), wo_ref[...])
    xn_ref[rows] = xn
    h2 = _rms(xn) * fg_ref[...]
    hi = h2.astype(BF16)
    h2_ref[rows] = hi
    lo = (h2 - hi.astype(F32)).astype(BF16)
    a = dot(hi, rw_ref[...])
    return a[:, :LANES] + a[:, LANES:] + dot(lo, rw_ref[:, :LANES]) + rb_ref[...]


def _post_route(logits):
    lane = lax.broadcasted_iota(I32, logits.shape, 1)
    lanef = lane.astype(F32)
    big = jnp.float32(1e9)
    ninf = jnp.float32(-jnp.inf)
    red = dict(axis=1, keepdims=True)
    gmask = lane < N_GROUPS
    mg = jnp.max(jnp.where(gmask, logits, ninf), **red)
    p_top = 1.0 / jnp.sum(jnp.where(gmask, jnp.exp(logits - mg), 0.0), **red)
    gi = jnp.min(jnp.where(gmask & (logits == mg), lanef, big), **red)
    emask = ((lane >= N_GROUPS) & (lane < N_GROUPS + N_EXPERTS)
             & (((lane - N_GROUPS) >> 3).astype(F32) == gi))
    le = jnp.where(emask, logits, ninf)
    m1 = jnp.max(le, **red)
    i1 = jnp.min(jnp.where(le == m1, lanef, big), **red)
    le2 = jnp.where(lanef == i1, ninf, le)
    m2 = jnp.max(le2, **red)
    i2 = jnp.min(jnp.where(le2 == m2, lanef, big), **red)
    r = jnp.exp(m2 - m1)
    w1 = 1.0 / (1.0 + r)
    w2 = r * w1
    e1 = i1 - N_GROUPS
    e2 = i2 - N_GROUPS
    is1, is2 = lanef == e1, lanef == e2
    return (e1, e2, p_top * w1, p_top * w2), is1, is2, jnp.where(is1 | is2, 1.0, 0.0)


def _post_body(x_ref, ya_ref, of_ref, or_ref, og_ref, ga_ref, gb_ref, wua_ref, wuh_ref, wo_ref,
               hg_ref, fg_ref, rw_ref, rb_ref, ltri_ref,
               xn_ref, h2_ref, rt_ref, cnt_ref):
    tm = x_ref.shape[0]
    part = tm // POST_PARTS
    parts = [slice(k * part, (k + 1) * part) for k in range(POST_PARTS)]
    logits = [_post_mix(rows, x_ref, ya_ref, of_ref, or_ref, og_ref, ga_ref, gb_ref, wua_ref, wuh_ref,
                        wo_ref, hg_ref, fg_ref, rw_ref, rb_ref, xn_ref, h2_ref) for rows in parts]
    routed = [_post_route(lg) for lg in logits]
    before = jnp.zeros((1, LANES), F32)
    red = dict(axis=1, keepdims=True)
    lane = lax.broadcasted_iota(I32, (part, LANES), 1)
    for rows, ((e1, e2, w1, w2), is1, is2, onehot) in zip(parts, routed):
        pref = jnp.dot(ltri_ref[...], onehot.astype(BF16), preferred_element_type=F32) + before
        before = before + jnp.sum(onehot, axis=0, keepdims=True)
        rank1 = jnp.sum(jnp.where(is1, pref, 0.0), **red)
        rank2 = jnp.sum(jnp.where(is2, pref, 0.0), **red)
        rt = jnp.zeros((part, LANES), F32)
        for idx, val in enumerate((e1, e2, rank1, rank2, w1, w2)):
            rt = jnp.where(lane == idx, val, rt)
        rt_ref[rows] = rt
    cnt_ref[...] = jnp.broadcast_to(before, cnt_ref.shape)


RT_E1, RT_E2, RT_RANK1, RT_RANK2, RT_W1, RT_W2 = range(6)
SUBLANES = 8


def _post(x, ya, of, orv, og, ga, gb, wua, wuh, wo, hg, fg, rw, rb, ltri):
    T, D = x.shape
    tm = ltri.shape[0] * POST_PARTS
    row = lambda w: pl.BlockSpec((tm, w), lambda i: (i, 0))
    full = lambda a: pl.BlockSpec(a.shape, lambda i: (0,) * a.ndim)
    return pl.pallas_call(
        _post_body,
        grid=(T // tm,),
        in_specs=[row(D), row(ATTN_Q_W), row(HGRN_W), row(HGRN_W), row(HGRN_W), row(D), row(D),
                  full(wua), full(wuh), full(wo), full(hg), full(fg), full(rw), full(rb), full(ltri)],
        out_specs=[row(D), row(D), row(LANES), pl.BlockSpec((SUBLANES, LANES), lambda i: (i, 0))],
        out_shape=[jax.ShapeDtypeStruct((T, D), F32), jax.ShapeDtypeStruct((T, D), BF16),
                   jax.ShapeDtypeStruct((T, LANES), F32),
                   jax.ShapeDtypeStruct((T // tm * SUBLANES, LANES), F32)],
        compiler_params=pltpu.CompilerParams(
            dimension_semantics=("arbitrary",), vmem_limit_bytes=VMEM_LIMIT),
        name="post_mixer_router",
    )(x, ya, of, orv, og, ga, gb, wua, wuh, wo, hg, fg, rw, rb, ltri)


XS_W = D_MODEL // 2 + LANES
SEG_BIG = 32
META_DST, META_LOFF, META_NBIG, META_NSMALL, META_TOTAL8 = (k * N_EXPERTS for k in range(5))
META_W = 5 * N_EXPERTS


def _local_rows(tm):
    return 2 * tm + N_EXPERTS * SUBLANES


def _local_pos(rt, loff_row):
    lanef = lax.broadcasted_iota(I32, rt.shape, 1).astype(F32)
    out = []
    for e_lane, r_lane in ((RT_E1, RT_RANK1), (RT_E2, RT_RANK2)):
        e = rt[:, e_lane:e_lane + 1]
        off = jnp.sum(jnp.where(lanef == e, loff_row, 0.0), axis=1, keepdims=True)
        out.append(off + rt[:, r_lane:r_lane + 1])
    return out


def _one_hot(hit):
    return jnp.where(hit, 1.0, 0.0).astype(BF16)


def _segment_starts(meta_ref, make_copy):
    for e in range(N_EXPERTS):
        dst = meta_ref[0, 0, META_DST + e]
        loff = meta_ref[0, 0, META_LOFF + e]
        n_big = meta_ref[0, 0, META_NBIG + e]
        n_small = meta_ref[0, 0, META_NSMALL + e]

        def big(k, c):
            o = k * SEG_BIG
            make_copy(pl.multiple_of(loff + o, SUBLANES), pl.multiple_of(dst + o, SUBLANES), SEG_BIG).start()
            return c

        lax.fori_loop(0, n_big, big, 0)

        def small(k, c):
            o = n_big * SEG_BIG + k * SUBLANES
            make_copy(pl.multiple_of(loff + o, SUBLANES), pl.multiple_of(dst + o, SUBLANES), SUBLANES).start()
            return c

        lax.fori_loop(0, n_small, small, 0)


def _segment_wait(meta_ref, make_copy, rows):
    total8 = meta_ref[0, 0, META_TOTAL8]
    for bit in range((rows // SUBLANES).bit_length()):
        @pl.when((total8 & (1 << bit)) != 0)
        def _():
            make_copy(0, 0, SUBLANES << bit).wait()


TAIL_PIECES = tuple(SUBLANES << b for b in range((MOE_BLK // SUBLANES).bit_length() - 1))


def _tail_copies(tail_ref, make_copy, action):
    def per_expert(e, c):
        start = tail_ref[0, 0, e]
        n8 = tail_ref[0, 0, N_EXPERTS + e]
        for rows in TAIL_PIECES:
            bit = rows // SUBLANES

            @pl.when((n8 & bit) != 0)
            def _():
                off = (n8 & ~(2 * bit - 1)) * SUBLANES
                action(make_copy(0, pl.multiple_of(start + off, SUBLANES), rows))
        return c

    lax.fori_loop(0, N_EXPERTS, per_expert, 0)

    def per_block(b, c):
        action(make_copy(0, pl.multiple_of(b * MOE_BLK, MOE_BLK), MOE_BLK))
        return c

    lax.fori_loop(tail_ref[0, 0, 2 * N_EXPERTS], tail_ref[0, 0, 2 * N_EXPERTS + 1], per_block, 0)


def _dispatch_body(meta_ref, prev_meta_ref, tail_ref, loff_ref, rt_ref, h_ref, xs_ref, buf_ref, sem):
    i = pl.program_id(0)
    last = pl.num_programs(0) - 1
    slot = i % 2
    tm = rt_ref.shape[0]
    rows = buf_ref.shape[1]
    rt = rt_ref[...]
    dot = functools.partial(jnp.dot, preferred_element_type=F32)
    lane = lax.broadcasted_iota(I32, (tm, LANES), 1)
    l1, l2 = _local_pos(rt, loff_ref[0])
    parts = []
    for pos in (l1, l2):
        hi = jnp.floor(pos * (1.0 / 32.0))
        parts += [hi, pos - 32.0 * hi]
    cols = jnp.zeros((tm, LANES), F32)
    for k, part in enumerate(parts):
        cols = jnp.where(lane == k, part, cols)
    eye = _one_hot(lax.broadcasted_iota(I32, (SUBLANES, LANES), 0) == lax.broadcasted_iota(I32, (SUBLANES, LANES), 1))
    lanes = _nt(eye, cols.astype(BF16))
    r1 = (32.0 * lanes[0:1] + lanes[1:2]).astype(I32)
    r2 = (32.0 * lanes[2:3] + lanes[3:4]).astype(I32)
    row = lax.broadcasted_iota(I32, (rows, tm), 0)
    p1, p2 = _one_hot(row == r1), _one_hot(row == r2)
    xl = dot(p1 + p2, h_ref[...])
    half = D_MODEL // 2
    buf_ref[slot, :, :half] = _pack_pair(xl[:, :half], xl[:, half:])

    def wcols(w):
        hi = w.astype(BF16).astype(F32)
        return jnp.where(lane == 0, hi, jnp.where(lane == 1, w - hi, 0.0)).astype(BF16)

    wl = dot(p1, wcols(rt[:, RT_W1:RT_W1 + 1])) + dot(p2, wcols(rt[:, RT_W2:RT_W2 + 1]))
    buf_ref[slot, :, half:] = lax.bitcast_convert_type(wl, U32)

    def copier(k):
        def make_copy(src_row, dst_row, n):
            return pltpu.make_async_copy(buf_ref.at[k, pl.ds(src_row, n)], xs_ref.at[pl.ds(dst_row, n)],
                                         sem.at[k])
        return make_copy

    _segment_starts(meta_ref, copier(slot))

    @pl.when(i > 0)
    def _():
        _segment_wait(prev_meta_ref, copier(1 - slot), rows)

    @pl.when(i == last)
    def _():
        _segment_wait(meta_ref, copier(slot), rows)
        buf_ref[slot, :MOE_BLK] = jnp.zeros((MOE_BLK, XS_W), U32)
        _tail_copies(tail_ref, copier(slot), lambda cp: cp.start())
        _tail_copies(tail_ref, copier(slot), lambda cp: cp.wait())


def _dispatch(meta, tail, loff, rt, h2, n_pad):
    T, D = h2.shape
    nt = meta.shape[0]
    tm = T // nt
    return pl.pallas_call(
        _dispatch_body,
        grid=(nt,),
        in_specs=[pl.BlockSpec((1, 1, meta.shape[2]), lambda i: (i, 0, 0), memory_space=pltpu.SMEM),
                  pl.BlockSpec((1, 1, meta.shape[2]), lambda i: (jnp.maximum(i - 1, 0), 0, 0),
                               memory_space=pltpu.SMEM),
                  pl.BlockSpec((1, 1, tail.shape[2]), lambda i: (0, 0, 0), memory_space=pltpu.SMEM),
                  pl.BlockSpec((1, 1, LANES), lambda i: (i, 0, 0)),
                  pl.BlockSpec((tm, LANES), lambda i: (i, 0)),
                  pl.BlockSpec((tm, D), lambda i: (i, 0))],
        out_specs=pl.BlockSpec(memory_space=pl.ANY),
        out_shape=jax.ShapeDtypeStruct((n_pad, XS_W), U32),
        scratch_shapes=[pltpu.VMEM((2, _local_rows(tm), XS_W), U32), pltpu.SemaphoreType.DMA((2,))],
        compiler_params=pltpu.CompilerParams(
            dimension_semantics=("arbitrary",), vmem_limit_bytes=VMEM_LIMIT),
        name="moe_dispatch",
    )(meta, meta, tail, loff, rt, h2)


EXPERT_PARTS = 2


def _expert_body(be_ref, nused_ref, used_ref, xs_ref, wg_ref, wu_ref, wd_ref, ys_ref,
                 wgb_ref, wub_ref, wdb_ref):
    del nused_ref
    i = pl.program_id(0)

    @pl.when((i == 0) | (be_ref[i] != be_ref[jnp.maximum(i - 1, 0)]))
    def _():
        wgb_ref[...] = wg_ref[0].astype(BF16)
        wub_ref[...] = wu_ref[0].astype(BF16)
        wdb_ref[...] = wd_ref[0].astype(BF16)

    part = MOE_BLK // EXPERT_PARTS
    half = D_MODEL // 2
    for k in range(EXPERT_PARTS):
        rows = slice(k * part, (k + 1) * part)
        live = used_ref[i] > k * part

        @pl.when(live)
        def _():
            dot = functools.partial(jnp.dot, preferred_element_type=F32)
            ha, hb = _unpack_pair(xs_ref[rows, :half])
            h = jnp.concatenate([ha, hb], axis=1).astype(BF16)
            wl = lax.bitcast_convert_type(xs_ref[rows, half:], F32)
            w = wl[:, 0:1] + wl[:, 1:2]
            g = dot(h, wgb_ref[...])
            u = dot(h, wub_ref[...])
            y = dot((g * _sigmoid(g) * u).astype(BF16), wdb_ref[...]) * w
            ys_ref[rows] = _pack_pair(y[:, :half], y[:, half:])

        @pl.when(jnp.logical_not(live))
        def _():
            ys_ref[rows] = jnp.zeros((part, half), U32)


def _experts(block_expert, nused, used, xs, wg, wu, wd):
    n_pad = xs.shape[0]
    blk_in = pl.BlockSpec((MOE_BLK, XS_W), lambda i, be, nu, us: (jnp.minimum(i, nu[0] - 1), 0))
    blk_out = pl.BlockSpec((MOE_BLK, D_MODEL // 2), lambda i, be, nu, us: (i, 0))
    wspec = lambda a: pl.BlockSpec((1,) + a.shape[1:], lambda i, be, nu, us: (be[i], 0, 0))
    return pl.pallas_call(
        _expert_body,
        grid_spec=pltpu.PrefetchScalarGridSpec(
            num_scalar_prefetch=3,
            grid=(n_pad // MOE_BLK,),
            in_specs=[blk_in, wspec(wg), wspec(wu), wspec(wd)],
            out_specs=blk_out,
            scratch_shapes=[pltpu.VMEM(wg.shape[1:], BF16), pltpu.VMEM(wu.shape[1:], BF16),
                            pltpu.VMEM(wd.shape[1:], BF16)],
        ),
        out_shape=jax.ShapeDtypeStruct((n_pad, D_MODEL // 2), U32),
        compiler_params=pltpu.CompilerParams(
            dimension_semantics=("arbitrary",), vmem_limit_bytes=VMEM_LIMIT),
        name="moe_experts",
    )(block_expert, nused, used, xs, wg, wu, wd)


def _combine_body(meta_ref, next_meta_ref, loff_ref, rt_ref, x_ref, fg_ref, ys_ref, o_ref, buf_ref, sem,
                  *, final_norm):
    i = pl.program_id(0)
    slot = i % 2

    def copier(k):
        def make_copy(loc_row, glob_row, n):
            return pltpu.make_async_copy(ys_ref.at[pl.ds(glob_row, n)], buf_ref.at[k, pl.ds(loc_row, n)],
                                         sem.at[k])
        return make_copy

    rows = buf_ref.shape[1]

    @pl.when(i == 0)
    def _():
        buf_ref[...] = jnp.zeros_like(buf_ref)
        _segment_starts(meta_ref, copier(0))

    @pl.when(i + 1 < pl.num_programs(0))
    def _():
        _segment_starts(next_meta_ref, copier(1 - slot))

    l1, l2 = _local_pos(rt_ref[...], loff_ref[0])
    col = lax.broadcasted_iota(I32, (rt_ref.shape[0], rows), 1)
    pt = _one_hot(col == l1.astype(I32)) + _one_hot(col == l2.astype(I32))
    _segment_wait(meta_ref, copier(slot), rows)
    yl = jnp.concatenate(_unpack_pair(buf_ref[slot]), axis=1).astype(BF16)
    out = x_ref[...] + jnp.dot(pt, yl, preferred_element_type=F32)
    if final_norm:
        out = _rms(out) * fg_ref[...]
    o_ref[...] = out


def _combine(meta, loff, rt, x, fg, ys, final_norm):
    T, D = x.shape
    nt = meta.shape[0]
    tm = T // nt
    return pl.pallas_call(
        functools.partial(_combine_body, final_norm=final_norm),
        grid=(nt,),
        in_specs=[pl.BlockSpec((1, 1, meta.shape[2]), lambda i: (i, 0, 0), memory_space=pltpu.SMEM),
                  pl.BlockSpec((1, 1, meta.shape[2]), lambda i: (jnp.minimum(i + 1, nt - 1), 0, 0),
                               memory_space=pltpu.SMEM),
                  pl.BlockSpec((1, 1, LANES), lambda i: (i, 0, 0)),
                  pl.BlockSpec((tm, LANES), lambda i: (i, 0)),
                  pl.BlockSpec((tm, D), lambda i: (i, 0)),
                  pl.BlockSpec((1, D), lambda i: (0, 0)),
                  pl.BlockSpec(memory_space=pl.ANY)],
        out_specs=pl.BlockSpec((tm, D), lambda i: (i, 0)),
        out_shape=jax.ShapeDtypeStruct((T, D), F32),
        scratch_shapes=[pltpu.VMEM((2, _local_rows(tm), D // 2), U32), pltpu.SemaphoreType.DMA((2,))],
        compiler_params=pltpu.CompilerParams(
            dimension_semantics=("arbitrary",), vmem_limit_bytes=VMEM_LIMIT),
        name="moe_combine",
    )(meta, meta, loff, rt, x, fg, ys)


def _rope_tables(L):
    half = HEAD_DIM // 2
    inv = ROPE_THETA ** (-jnp.arange(half, dtype=F32) / half)
    ang = jnp.arange(L, dtype=F32)[:, None] * inv[None, :]
    lane = np.arange(LANES)
    idx = lane % half
    first = jnp.asarray((lane % HEAD_DIM) < half)
    cos = jnp.cos(ang)[:, idx]
    sin = jnp.sin(ang)[:, idx]
    return cos, jnp.where(first, -sin, 0.0), jnp.where(first, 0.0, sin)


def _moe_layout(cnt, T, layer):
    nt = cnt.shape[0] // SUBLANES
    c = cnt.reshape(nt, SUBLANES, LANES)[:, 0, :N_EXPERTS].astype(I32)
    c8 = (c + SUBLANES - 1) // SUBLANES * SUBLANES
    tile_off = jnp.cumsum(c8, axis=0) - c8
    total = jnp.sum(c8, axis=0)
    padded = (total + MOE_BLK - 1) // MOE_BLK * MOE_BLK
    pend = jnp.cumsum(padded)
    dst = (pend - padded)[None, :] + tile_off
    loff = jnp.cumsum(c8, axis=1) - c8
    n8 = c8 // SUBLANES
    per_big = SEG_BIG // SUBLANES
    total8 = jnp.broadcast_to(jnp.sum(n8, axis=1, keepdims=True), n8.shape)
    meta = jnp.concatenate([dst, loff, n8 // per_big, n8 % per_big, total8], axis=1).reshape(nt, 1, META_W)
    loff_f = jnp.pad(loff.astype(F32), ((0, 0), (0, LANES - N_EXPERTS))).reshape(nt, 1, LANES)
    n_pad = 2 * T + nt * N_EXPERTS * SUBLANES + N_EXPERTS * MOE_BLK
    n_pad = (n_pad + MOE_BLK - 1) // MOE_BLK * MOE_BLK
    starts = jnp.arange(n_pad // MOE_BLK, dtype=I32) * MOE_BLK
    block_expert = jnp.minimum(jnp.sum(starts[:, None] >= pend[None, :], axis=1), N_EXPERTS - 1)
    used = jnp.clip((pend - padded + total)[block_expert] - starts, 0, MOE_BLK).astype(I32)
    block_expert = block_expert.astype(I32) + layer * N_EXPERTS
    nused = (pend[-1:] // MOE_BLK).astype(I32)
    tail = jnp.concatenate([pend - padded + total, (padded - total) // SUBLANES, nused,
                            jnp.full((1,), n_pad // MOE_BLK, I32)]).reshape(1, 1, 2 * N_EXPERTS + 2)
    return meta, tail, loff_f, block_expert, nused, used, n_pad


def kernel(x, attn_norm_g, w_in, sink_logits, hgrn_lb_table, hgrn_norm_g, w_up_attn, w_up_hgrn, w_out,
           ffn_norm_g, w_router_group, b_router_group, w_router_expert, b_router_expert,
           w_expert_gate, w_expert_up, w_expert_down, final_norm_g):
    B, L, D = x.shape
    T = B * L
    depth = w_in.shape[0]
    sm = jax.nn.softmax(hgrn_lb_table.astype(F32), axis=0)
    lower_bounds = jnp.cumsum(sm, axis=0) - sm[:1]
    cos, s1, s2 = _rope_tables(L)
    tm_post = min(TM_POST, T)
    ltri = jnp.tril(jnp.ones((tm_post // POST_PARTS,) * 2, BF16), -1)
    fin_g = final_norm_g.reshape(1, D).astype(F32)
    w_in_bf16 = w_in.astype(BF16)
    all_layers = lambda w: w.reshape((depth * N_EXPERTS,) + w.shape[2:])
    wg_all, wu_all, wd_all = all_layers(w_expert_gate), all_layers(w_expert_up), all_layers(w_expert_down)

    for l in range(depth):
        qa, ka, va, qb, sf, sb, ib, og, ga, gb = _inproj(
            x, attn_norm_g[l].reshape(1, D), w_in_bf16, l, cos, s1, s2)
        ya = _attention(qa, ka, va, sink_logits[l].astype(F32) * LOG2E)
        of, orv = _hgrn(qb, sf, sb, ib, lower_bounds[l])

        rw = jnp.concatenate(
            [w_router_group[l], jnp.transpose(w_router_expert[l], (1, 0, 2)).reshape(D, N_EXPERTS)], axis=1)
        rw = jnp.pad(rw.astype(F32), ((0, 0), (0, LANES - rw.shape[1])))
        rwh = rw.astype(BF16)
        rw2 = jnp.concatenate([rwh, (rw - rwh.astype(F32)).astype(BF16)], axis=1)
        rb = jnp.concatenate([b_router_group[l], b_router_expert[l].reshape(-1)]).astype(F32)
        rb = jnp.pad(rb, (0, LANES - rb.shape[0])).reshape(1, LANES)
        flat = lambda a: a.reshape(T, a.shape[-1])
        xn, h2, rt, cnt = _post(
            flat(x), flat(ya), flat(of), flat(orv), flat(og), flat(ga), flat(gb),
            w_up_attn[l].astype(BF16), w_up_hgrn[l].astype(BF16), w_out[l].astype(BF16),
            hgrn_norm_g[l].reshape(1, HGRN_W).astype(F32), ffn_norm_g[l].reshape(1, D).astype(F32),
            rw2, rb, ltri)

        meta, tail, loff, block_expert, nused, used, n_pad = _moe_layout(cnt, T, l)
        xs = _dispatch(meta, tail, loff, rt, h2, n_pad)
        ys = _experts(block_expert, nused, used, xs, wg_all, wu_all, wd_all)
        x = _combine(meta, loff, rt, xn, fin_g, ys, l == depth - 1).reshape(B, L, D)
    return x
```

```python
import functools

import numpy as np
import jax
import jax.numpy as jnp
from jax import lax
from jax.experimental import pallas as pl
from jax.experimental.pallas import tpu as pltpu

F32 = jnp.float32
BF16 = jnp.bfloat16
U32 = jnp.uint32
I32 = jnp.int32

D_MODEL = 1024
N_Q_HEADS = 8
N_KV_HEADS = 2
HEAD_DIM = 64
WINDOW = 128
ROPE_THETA = 10000.0
MASK_VALUE = -1e30
HGRN_HEADS = 4
HGRN_DIM = 128
LOGF_MIN = -4.0
F_MIN = float(np.exp(LOGF_MIN))
LOG2E = float(np.log2(np.e))
Q_SCALE = HEAD_DIM ** -0.5 * LOG2E
LB_FLOOR = 1e-30
N_GROUPS = 4
EXPERTS_PER_GROUP = 8
N_EXPERTS = N_GROUPS * EXPERTS_PER_GROUP
D_EXPERT = 512
NORM_EPS = 1e-6

ATTN_Q_W = N_Q_HEADS * HEAD_DIM
ATTN_KV_W = N_KV_HEADS * HEAD_DIM
HGRN_W = HGRN_HEADS * HGRN_DIM
OFF_QA = 0
OFF_KA = OFF_QA + ATTN_Q_W
OFF_VA = OFF_KA + ATTN_KV_W
OFF_QB = OFF_VA + ATTN_KV_W
OFF_ZF = OFF_QB + HGRN_W
OFF_ZB = OFF_ZF + HGRN_W
OFF_IB = OFF_ZB + HGRN_W
OFF_OG = OFF_IB + HGRN_W
OFF_GA = OFF_OG + HGRN_W
OFF_GB = OFF_GA + D_MODEL

LANES = 128
VMEM_LIMIT = 56 * 1024 * 1024

TM_PROJ = 512
TQ_ATTN = 1024
TH_HGRN = 1024
TM_POST = 512
MOE_BLK = 512


def _sigmoid(z):
    return 1.0 / (1.0 + jnp.exp(-z))


def _rms(x):
    return x * lax.rsqrt(jnp.mean(x * x, axis=-1, keepdims=True) + NORM_EPS)


def _nt(a, b):
    return lax.dot_general(a, b, (((1,), (1,)), ((), ())), preferred_element_type=F32)


def _pack_pair(a, b):
    ua = lax.bitcast_convert_type(a.astype(BF16).astype(F32), U32)
    ub = lax.bitcast_convert_type(b.astype(BF16).astype(F32), U32)
    return ua | (ub >> 16)


def _unpack_pair(u):
    hi = lax.bitcast_convert_type(u & jnp.uint32(0xFFFF0000), F32)
    lo = lax.bitcast_convert_type(u << 16, F32)
    return hi, lo


def _inproj_body(x_ref, g_ref, w_ref, cos_ref, s1_ref, s2_ref,
                 qa_ref, ka_ref, va_ref, qb_ref, sf_ref, sb_ref, ib_ref, og_ref, ga_ref, gb_ref):
    h = (_rms(x_ref[0]) * g_ref[...]).astype(BF16)

    def proj(off, width):
        return jnp.dot(h, w_ref[:, off:off + width], preferred_element_type=F32)

    cos, s1, s2 = cos_ref[...], s1_ref[...], s2_ref[...]

    def rope(t):
        return t * cos + pltpu.roll(t, 96, 1) * s1 + pltpu.roll(t, 32, 1) * s2

    q = proj(OFF_QA, ATTN_Q_W)
    for c in range(ATTN_Q_W // LANES):
        sl = slice(c * LANES, (c + 1) * LANES)
        qa_ref[0, :, sl] = (rope(q[:, sl]) * Q_SCALE).astype(BF16)
    k = rope(proj(OFF_KA, ATTN_KV_W))
    ka_ref[0, :, :LANES] = k.astype(BF16)
    ka_ref[0, :, LANES:] = pltpu.roll(k, 64, 1).astype(BF16)
    v = proj(OFF_VA, ATTN_KV_W)
    va_ref[0, :, :LANES] = v.astype(BF16)
    va_ref[0, :, LANES:] = pltpu.roll(v, 64, 1).astype(BF16)
    t = proj(OFF_QB, HGRN_W)
    qb_ref[0] = (t * _sigmoid(t)).astype(BF16)
    sf_ref[0] = _sigmoid(proj(OFF_ZF, HGRN_W)).astype(BF16)
    sb_ref[0] = _sigmoid(proj(OFF_ZB, HGRN_W)).astype(BF16)
    ib_ref[0] = proj(OFF_IB, HGRN_W).astype(BF16)
    t = proj(OFF_OG, HGRN_W)
    og_ref[0] = (t * _sigmoid(t)).astype(BF16)
    ga_ref[0] = _sigmoid(proj(OFF_GA, D_MODEL)).astype(BF16)
    gb_ref[0] = _sigmoid(proj(OFF_GB, D_MODEL)).astype(BF16)


def _inproj(x, g, w_all, layer, cos, s1, s2):
    B, L, D = x.shape
    tm = min(TM_PROJ, L)
    widths = (ATTN_Q_W, 2 * ATTN_KV_W, 2 * ATTN_KV_W, HGRN_W, HGRN_W, HGRN_W, HGRN_W, HGRN_W,
              D_MODEL, D_MODEL)
    tab = pl.BlockSpec((tm, LANES), lambda b, i: (i, 0))
    return pl.pallas_call(
        _inproj_body,
        grid=(B, L // tm),
        in_specs=[
            pl.BlockSpec((1, tm, D), lambda b, i: (b, i, 0)),
            pl.BlockSpec((1, D), lambda b, i: (0, 0)),
            pl.BlockSpec((None,) + w_all.shape[1:], lambda b, i: (layer, 0, 0)),
            tab, tab, tab,
        ],
        out_specs=[pl.BlockSpec((1, tm, w), lambda b, i: (b, i, 0)) for w in widths],
        out_shape=[jax.ShapeDtypeStruct((B, L, w), BF16) for w in widths],
        compiler_params=pltpu.CompilerParams(
            dimension_semantics=("arbitrary", "arbitrary"), vmem_limit_bytes=VMEM_LIMIT),
        name="inproj",
    )(x, g, w_all, cos, s1, s2)


def _attn_body(sink_ref, q_ref, kp_ref, km_ref, kn_ref, vp_ref, vm_ref, vn_ref, o_ref, *, tq, seq):
    i = pl.program_id(1)
    kw = jnp.concatenate([kp_ref[0], km_ref[0], kn_ref[0]], axis=0)
    vw = jnp.concatenate([vp_ref[0], vm_ref[0], vn_ref[0]], axis=0)
    lo = lax.broadcasted_iota(I32, (1, LANES), 1) < HEAD_DIM
    zero = jnp.zeros((), BF16)

    def variants(t):
        a, b = t[:, :LANES], t[:, LANES:]
        return ((jnp.where(lo, a, zero), jnp.where(lo, zero, b)),
                (jnp.where(lo, b, zero), jnp.where(lo, zero, a)))

    kvar, vvar = variants(kw), variants(vw)
    row = lax.broadcasted_iota(I32, (WINDOW, 3 * WINDOW), 0)
    col = lax.broadcasted_iota(I32, (WINDOW, 3 * WINDOW), 1)
    band = (col >= row) & (col <= row + 2 * WINDOW)
    ncol = ATTN_Q_W // LANES
    group = lambda c: (2 * c) // (N_Q_HEADS // N_KV_HEADS)
    heads = [(c, hh) for c in range(ncol) for hh in range(2)]
    for j in range(tq // WINDOW):
        kpos = col + (i * tq + (j - 1) * WINDOW)
        bias = jnp.where(band & (kpos >= 0) & (kpos < seq), 0.0, MASK_VALUE)
        rows = slice(j * WINDOW, (j + 1) * WINDOW)
        win = slice(j * WINDOW, (j + 3) * WINDOW)
        scores = [_nt(q_ref[0, rows, c * LANES:(c + 1) * LANES], kvar[group(c)][hh][win]) for c, hh in heads]
        probs, denoms = [], []
        for (c, hh), s in zip(heads, scores):
            s = jnp.concatenate([s[:, :WINDOW] + bias[:, :WINDOW], s[:, WINDOW:2 * WINDOW],
                                 s[:, 2 * WINDOW:] + bias[:, 2 * WINDOW:]], axis=1)
            sk = sink_ref[2 * c + hh]
            m = jnp.maximum(jnp.max(s, axis=1, keepdims=True), sk)
            p = jnp.exp2(s - m)
            denoms.append(jnp.sum(p, axis=1, keepdims=True) + jnp.exp2(sk - m))
            probs.append(p.astype(BF16))
        for c in range(ncol):
            g = group(c)
            pv = jnp.dot(jnp.concatenate(probs[2 * c:2 * c + 2], axis=1),
                         jnp.concatenate([vvar[g][0][win], vvar[g][1][win]], axis=0),
                         preferred_element_type=F32)
            inv = jnp.where(lo, 1.0 / denoms[2 * c], 1.0 / denoms[2 * c + 1])
            o_ref[0, rows, c * LANES:(c + 1) * LANES] = (pv * inv).astype(BF16)


def _attention(qa, ka, va, sink):
    B, L, _ = qa.shape
    tq = min(TQ_ATTN, L)
    r = tq // WINDOW
    nb = L // WINDOW
    kvw = 2 * ATTN_KV_W
    main = pl.BlockSpec((1, tq, kvw), lambda b, i: (b, i, 0))
    prev = pl.BlockSpec((1, WINDOW, kvw), lambda b, i: (b, jnp.maximum(i * r - 1, 0), 0))
    nxt = pl.BlockSpec((1, WINDOW, kvw), lambda b, i: (b, jnp.minimum(i * r + r, nb - 1), 0))
    return pl.pallas_call(
        functools.partial(_attn_body, tq=tq, seq=L),
        grid=(B, L // tq),
        in_specs=[
            pl.BlockSpec(memory_space=pltpu.SMEM),
            pl.BlockSpec((1, tq, ATTN_Q_W), lambda b, i: (b, i, 0)),
            prev, main, nxt, prev, main, nxt,
        ],
        out_specs=pl.BlockSpec((1, tq, ATTN_Q_W), lambda b, i: (b, i, 0)),
        out_shape=jax.ShapeDtypeStruct((B, L, ATTN_Q_W), BF16),
        compiler_params=pltpu.CompilerParams(
            dimension_semantics=("arbitrary", "arbitrary"), vmem_limit_bytes=VMEM_LIMIT),
        name="win_attn",
    )(sink, qa, ka, ka, ka, va, va, va)


HB = 128
LEAF = 32
HGRN_LEVELS = tuple(LEAF << k for k in range((HB // LEAF).bit_length() - 1))
LEAF_SHIFT = 0.5 * LEAF * (-LOGF_MIN) * LOG2E


def _hgrn_masks(reverse):
    r = lax.broadcasted_iota(I32, (HB, HB), 0)
    c = lax.broadcasted_iota(I32, (HB, HB), 1)
    if reverse:
        r, c = c, r
    tri = jnp.where(c <= r, 1.0, 0.0).astype(BF16)
    same = lambda w: (r >> (w.bit_length() - 1)) == (c >> (w.bit_length() - 1))
    cross = lambda w: ((r & w) != 0) & ((c & w) == 0)
    lev = jnp.full((HB, HB), len(HGRN_LEVELS), I32)
    for k, w in reversed(list(enumerate(HGRN_LEVELS[:-1]))):
        lev = jnp.where(same(2 * w) & cross(w), k + 1, lev)
    lev = jnp.where(same(LEAF) & (c <= r), 0, lev)
    return tri, lev


def _hgrn_gates(s, lb, lbc, tri):
    f = lbc + (1.0 - lb) * s
    logf = jnp.maximum(jnp.log(f), LOGF_MIN)
    kk = 1.0 - jnp.maximum(f, F_MIN)
    hi = logf.astype(BF16)
    mid = (logf - hi.astype(F32)).astype(BF16)
    dot = functools.partial(jnp.dot, preferred_element_type=F32)
    return kk, (dot(tri, hi) + dot(tri, mid)) * LOG2E


def _hgrn_scores(q, kk, cum, v, st_ref, reverse):
    def ref_row(idx):
        return cum[idx:idx + 1]

    total = ref_row(0) if reverse else ref_row(HB - 1)
    q_blk = q * jnp.exp2(cum)
    k_blk = kk * jnp.exp2(total - cum)

    ql, kl = [], []
    for c in range(HB // LEAF):
        rows = slice(c * LEAF, (c + 1) * LEAF)
        edge = (c + 1) * LEAF if reverse else c * LEAF - 1
        loc = (cum[rows] - ref_row(edge) if 0 <= edge < HB else cum[rows]) + LEAF_SHIFT
        ql.append(q[rows] * jnp.exp2(loc))
        kl.append(kk[rows] * jnp.exp2(-loc))
    pairs = [(jnp.concatenate(ql, axis=0), jnp.concatenate(kl, axis=0))]
    for w in HGRN_LEVELS:
        qp, kp = [], []
        zeros = jnp.zeros((w, HB), F32)
        for s0 in range(0, HB, 2 * w):
            first, second = slice(s0, s0 + w), slice(s0 + w, s0 + 2 * w)
            if reverse:
                ref = ref_row(s0 + w)
                qp += [q[first] * jnp.exp2(cum[first] - ref), zeros]
                kp += [zeros, kk[second] * jnp.exp2(ref - cum[second])]
            else:
                ref = ref_row(s0 + w - 1)
                qp += [zeros, q[second] * jnp.exp2(cum[second] - ref)]
                kp += [kk[first] * jnp.exp2(ref - cum[first]), zeros]
        pairs.append((jnp.concatenate(qp, axis=0), jnp.concatenate(kp, axis=0)))
    scores = [_nt(qq.astype(BF16), kq.astype(BF16)) for qq, kq in pairs]
    st = st_ref[...]
    o_state = _nt(q_blk.astype(BF16), st.astype(BF16))
    st_ref[...] = st * jnp.exp2(total) + lax.dot_general(
        v, k_blk.astype(BF16), (((0,), (0,)), ((), ())), preferred_element_type=F32)
    return scores, o_state


def _hgrn_output(scores, o_state, v, lev):
    a = scores[-1]
    for k in reversed(range(len(scores) - 1)):
        a = jnp.where(lev == k, scores[k], a)
    return jnp.dot(a.astype(BF16), v, preferred_element_type=F32) + o_state


def _hgrn_body(lb_ref, qf_ref, sf_ref, vf_ref, qr_ref, sr_ref, vr_ref, of_ref, or_ref, st_ref, *, th):
    @pl.when(pl.program_id(1) == 0)
    def _():
        st_ref[...] = jnp.zeros_like(st_ref)

    lb = lb_ref[...]
    lbc = jnp.maximum(lb, LB_FLOOR)
    masks = (_hgrn_masks(False), _hgrn_masks(True))
    nblk = th // HB
    streams = ((qf_ref, sf_ref, vf_ref, of_ref), (qr_ref, sr_ref, vr_ref, or_ref))

    def body(t, carry):
        rows = [pl.multiple_of((nblk - 1 - t if d else t) * HB, HB) for d in range(2)]
        units = [(d, h * HGRN_DIM) for d in range(2) for h in range(HGRN_HEADS)]

        def load(d, k, lane0):
            return streams[d][k][0, pl.ds(rows[d], HB), lane0:lane0 + HGRN_DIM]

        gates = [_hgrn_gates(load(d, 1, c0).astype(F32), lb[d:d + 1, c0:c0 + HGRN_DIM],
                             lbc[d:d + 1, c0:c0 + HGRN_DIM], masks[d][0]) for d, c0 in units]
        mids = [_hgrn_scores(load(d, 0, c0).astype(F32), kk, cum, load(d, 2, c0),
                             st_ref.at[d, c0 // HGRN_DIM], bool(d))
                for (d, c0), (kk, cum) in zip(units, gates)]
        for (d, c0), (scores, o_state) in zip(units, mids):
            o = _hgrn_output(scores, o_state, load(d, 2, c0), masks[d][1])
            streams[d][3][0, pl.ds(rows[d], HB), c0:c0 + HGRN_DIM] = o.astype(BF16)
        return carry

    lax.fori_loop(0, nblk, body, 0)


def _hgrn(qb, sf, sb, ib, lb):
    B, L, _ = qb.shape
    th = min(TH_HGRN, L)
    n = L // th
    fwd = pl.BlockSpec((1, th, HGRN_W), lambda b, i: (b, i, 0))
    rev = pl.BlockSpec((1, th, HGRN_W), lambda b, i: (b, n - 1 - i, 0))
    return pl.pallas_call(
        functools.partial(_hgrn_body, th=th),
        grid=(B, n),
        in_specs=[pl.BlockSpec((2, HGRN_W), lambda b, i: (0, 0)), fwd, fwd, fwd, rev, rev, rev],
        out_specs=[fwd, rev],
        out_shape=[jax.ShapeDtypeStruct((B, L, HGRN_W), BF16)] * 2,
        scratch_shapes=[pltpu.VMEM((2, HGRN_HEADS, HGRN_DIM, HGRN_DIM), F32)],
        compiler_params=pltpu.CompilerParams(
            dimension_semantics=("arbitrary", "arbitrary"), vmem_limit_bytes=VMEM_LIMIT),
        name="hgrn2",
    )(lb, qb, sf, ib, qb, sb, ib)


POST_PARTS = 2


def _post_mix(rows, x_ref, ya_ref, of_ref, or_ref, og_ref, ga_ref, gb_ref, wua_ref, wuh_ref, wo_ref,
              hg_ref, fg_ref, rw_ref, rb_ref, xn_ref, h2_ref):
    dot = functools.partial(jnp.dot, preferred_element_type=F32)
    o = of_ref[rows].astype(F32) + or_ref[rows].astype(F32)
    yh = (_rms(o) * hg_ref[...] * og_ref[rows].astype(F32)).astype(BF16)
    merged = (ga_ref[rows].astype(F32) * dot(ya_ref[rows], wua_ref[...])
              + gb_ref[rows].astype(F32) * dot(yh, wuh_ref[...]))
    xn = x_ref[rows] + dot(merged.astype(BF16), wo_ref[...])
    xn_ref[rows] = xn
    h2 = _rms(xn) * fg_ref[...]
    hi = h2.astype(BF16)
    h2_ref[rows] = hi
    lo = (h2 - hi.astype(F32)).astype(BF16)
    a = dot(hi, rw_ref[...])
    return a[:, :LANES] + a[:, LANES:] + dot(lo, rw_ref[:, :LANES]) + rb_ref[...]


def _post_route(logits):
    lane = lax.broadcasted_iota(I32, logits.shape, 1)
    lanef = lane.astype(F32)
    big = jnp.float32(1e9)
    ninf = jnp.float32(-jnp.inf)
    red = dict(axis=1, keepdims=True)
    gmask = lane < N_GROUPS
    mg = jnp.max(jnp.where(gmask, logits, ninf), **red)
    p_top = 1.0 / jnp.sum(jnp.where(gmask, jnp.exp(logits - mg), 0.0), **red)
    gi = jnp.min(jnp.where(gmask & (logits == mg), lanef, big), **red)
    emask = ((lane >= N_GROUPS) & (lane < N_GROUPS + N_EXPERTS)
             & (((lane - N_GROUPS) >> 3).astype(F32) == gi))
    le = jnp.where(emask, logits, ninf)
    m1 = jnp.max(le, **red)
    i1 = jnp.min(jnp.where(le == m1, lanef, big), **red)
    le2 = jnp.where(lanef == i1, ninf, le)
    m2 = jnp.max(le2, **red)
    i2 = jnp.min(jnp.where(le2 == m2, lanef, big), **red)
    r = jnp.exp(m2 - m1)
    w1 = 1.0 / (1.0 + r)
    w2 = r * w1
    e1 = i1 - N_GROUPS
    e2 = i2 - N_GROUPS
    is1, is2 = lanef == e1, lanef == e2
    return (e1, e2, p_top * w1, p_top * w2), is1, is2, jnp.where(is1 | is2, 1.0, 0.0)


def _post_body(x_ref, ya_ref, of_ref, or_ref, og_ref, ga_ref, gb_ref, wua_ref, wuh_ref, wo_ref,
               hg_ref, fg_ref, rw_ref, rb_ref, ltri_ref,
               xn_ref, h2_ref, rt_ref, cnt_ref):
    tm = x_ref.shape[0]
    part = tm // POST_PARTS
    parts = [slice(k * part, (k + 1) * part) for k in range(POST_PARTS)]
    logits = [_post_mix(rows, x_ref, ya_ref, of_ref, or_ref, og_ref, ga_ref, gb_ref, wua_ref, wuh_ref,
                        wo_ref, hg_ref, fg_ref, rw_ref, rb_ref, xn_ref, h2_ref) for rows in parts]
    routed = [_post_route(lg) for lg in logits]
    before = jnp.zeros((1, LANES), F32)
    red = dict(axis=1, keepdims=True)
    lane = lax.broadcasted_iota(I32, (part, LANES), 1)
    for rows, ((e1, e2, w1, w2), is1, is2, onehot) in zip(parts, routed):
        pref = jnp.dot(ltri_ref[...], onehot.astype(BF16), preferred_element_type=F32) + before
        before = before + jnp.sum(onehot, axis=0, keepdims=True)
        rank1 = jnp.sum(jnp.where(is1, pref, 0.0), **red)
        rank2 = jnp.sum(jnp.where(is2, pref, 0.0), **red)
        rt = jnp.zeros((part, LANES), F32)
        for idx, val in enumerate((e1, e2, rank1, rank2, w1, w2)):
            rt = jnp.where(lane == idx, val, rt)
        rt_ref[rows] = rt
    cnt_ref[...] = jnp.broadcast_to(before, cnt_ref.shape)


RT_E1, RT_E2, RT_RANK1, RT_RANK2, RT_W1, RT_W2 = range(6)
SUBLANES = 8


def _post(x, ya, of, orv, og, ga, gb, wua, wuh, wo, hg, fg, rw, rb, ltri):
    T, D = x.shape
    tm = ltri.shape[0] * POST_PARTS
    row = lambda w: pl.BlockSpec((tm, w), lambda i: (i, 0))
    full = lambda a: pl.BlockSpec(a.shape, lambda i: (0,) * a.ndim)
    return pl.pallas_call(
        _post_body,
        grid=(T // tm,),
        in_specs=[row(D), row(ATTN_Q_W), row(HGRN_W), row(HGRN_W), row(HGRN_W), row(D), row(D),
                  full(wua), full(wuh), full(wo), full(hg), full(fg), full(rw), full(rb), full(ltri)],
        out_specs=[row(D), row(D), row(LANES), pl.BlockSpec((SUBLANES, LANES), lambda i: (i, 0))],
        out_shape=[jax.ShapeDtypeStruct((T, D), F32), jax.ShapeDtypeStruct((T, D), BF16),
                   jax.ShapeDtypeStruct((T, LANES), F32),
                   jax.ShapeDtypeStruct((T // tm * SUBLANES, LANES), F32)],
        compiler_params=pltpu.CompilerParams(
            dimension_semantics=("arbitrary",), vmem_limit_bytes=VMEM_LIMIT),
        name="post_mixer_router",
    )(x, ya, of, orv, og, ga, gb, wua, wuh, wo, hg, fg, rw, rb, ltri)


XS_W = D_MODEL // 2 + LANES
SLOT_W1, SLOT_W2, SLOT_E1, SLOT_E2 = 0, 2, 4, 5
SEG_BIG = 32
META_DST, META_LOFF, META_NBIG, META_NSMALL, META_TOTAL8 = (k * N_EXPERTS for k in range(5))
META_W = 5 * N_EXPERTS


def _local_rows(tm):
    return 2 * tm + N_EXPERTS * SUBLANES


def _local_pos(rt, loff_row):
    lanef = lax.broadcasted_iota(I32, rt.shape, 1).astype(F32)
    out = []
    for e_lane, r_lane in ((RT_E1, RT_RANK1), (RT_E2, RT_RANK2)):
        e = rt[:, e_lane:e_lane + 1]
        off = jnp.sum(jnp.where(lanef == e, loff_row, 0.0), axis=1, keepdims=True)
        out.append(off + rt[:, r_lane:r_lane + 1])
    return out


def _one_hot(hit):
    return jnp.where(hit, 1.0, 0.0).astype(BF16)


def _segment_starts(meta_ref, make_copy):
    for e in range(N_EXPERTS):
        dst = meta_ref[0, 0, META_DST + e]
        loff = meta_ref[0, 0, META_LOFF + e]
        n_big = meta_ref[0, 0, META_NBIG + e]
        n_small = meta_ref[0, 0, META_NSMALL + e]

        def big(k, c):
            o = k * SEG_BIG
            make_copy(pl.multiple_of(loff + o, SUBLANES), pl.multiple_of(dst + o, SUBLANES), SEG_BIG).start()
            return c

        lax.fori_loop(0, n_big, big, 0)

        def small(k, c):
            o = n_big * SEG_BIG + k * SUBLANES
            make_copy(pl.multiple_of(loff + o, SUBLANES), pl.multiple_of(dst + o, SUBLANES), SUBLANES).start()
            return c

        lax.fori_loop(0, n_small, small, 0)


def _segment_wait(meta_ref, make_copy, rows):
    total8 = meta_ref[0, 0, META_TOTAL8]
    for bit in range((rows // SUBLANES).bit_length()):
        @pl.when((total8 & (1 << bit)) != 0)
        def _():
            make_copy(0, 0, SUBLANES << bit).wait()


TAIL_PIECES = tuple(SUBLANES << b for b in range((MOE_BLK // SUBLANES).bit_length() - 1))


def _tail_copies(tail_ref, make_copy, action):
    def per_expert(e, c):
        start = tail_ref[0, 0, e]
        n8 = tail_ref[0, 0, N_EXPERTS + e]
        for rows in TAIL_PIECES:
            bit = rows // SUBLANES

            @pl.when((n8 & bit) != 0)
            def _():
                off = (n8 & ~(2 * bit - 1)) * SUBLANES
                action(make_copy(0, pl.multiple_of(start + off, SUBLANES), rows))
        return c

    lax.fori_loop(0, N_EXPERTS, per_expert, 0)

    def per_block(b, c):
        action(make_copy(0, pl.multiple_of(b * MOE_BLK, MOE_BLK), MOE_BLK))
        return c

    lax.fori_loop(tail_ref[0, 0, 2 * N_EXPERTS], tail_ref[0, 0, 2 * N_EXPERTS + 1], per_block, 0)


def _dispatch_body(meta_ref, prev_meta_ref, tail_ref, loff_ref, rt_ref, h_ref, xs_ref, buf_ref, sem):
    i = pl.program_id(0)
    last = pl.num_programs(0) - 1
    slot = i % 2
    tm = rt_ref.shape[0]
    rows = buf_ref.shape[1]
    rt = rt_ref[...]
    dot = functools.partial(jnp.dot, preferred_element_type=F32)
    lane = lax.broadcasted_iota(I32, (tm, LANES), 1)
    l1, l2 = _local_pos(rt, loff_ref[0])
    parts = []
    for pos in (l1, l2):
        hi = jnp.floor(pos * (1.0 / 32.0))
        parts += [hi, pos - 32.0 * hi]
    cols = jnp.zeros((tm, LANES), F32)
    for k, part in enumerate(parts):
        cols = jnp.where(lane == k, part, cols)
    eye = _one_hot(lax.broadcasted_iota(I32, (SUBLANES, LANES), 0) == lax.broadcasted_iota(I32, (SUBLANES, LANES), 1))
    lanes = _nt(eye, cols.astype(BF16))
    r1 = (32.0 * lanes[0:1] + lanes[1:2]).astype(I32)
    r2 = (32.0 * lanes[2:3] + lanes[3:4]).astype(I32)
    row = lax.broadcasted_iota(I32, (rows, tm), 0)
    perm = _one_hot((row == r1) | (row == r2))
    side = jnp.zeros((tm, LANES), F32)
    for w_lane, e_lane, base in ((RT_W1, RT_E1, SLOT_W1), (RT_W2, RT_E2, SLOT_W2)):
        w = rt[:, w_lane:w_lane + 1]
        hi = w.astype(BF16).astype(F32)
        side = jnp.where(lane == base, hi, jnp.where(lane == base + 1, w - hi, side))
        side = jnp.where(lane == SLOT_E1 + (base - SLOT_W1) // 2, rt[:, e_lane:e_lane + 1], side)
    xl = dot(perm, jnp.concatenate([h_ref[...], side.astype(BF16)], axis=1))
    half = D_MODEL // 2
    buf_ref[slot, :, :half] = _pack_pair(xl[:, :half], xl[:, half:D_MODEL])
    buf_ref[slot, :, half:] = lax.bitcast_convert_type(xl[:, D_MODEL:], U32)

    def copier(k):
        def make_copy(src_row, dst_row, n):
            return pltpu.make_async_copy(buf_ref.at[k, pl.ds(src_row, n)], xs_ref.at[pl.ds(dst_row, n)],
                                         sem.at[k])
        return make_copy

    _segment_starts(meta_ref, copier(slot))

    @pl.when(i > 0)
    def _():
        _segment_wait(prev_meta_ref, copier(1 - slot), rows)

    @pl.when(i == last)
    def _():
        _segment_wait(meta_ref, copier(slot), rows)
        buf_ref[slot, :MOE_BLK] = jnp.zeros((MOE_BLK, XS_W), U32)
        _tail_copies(tail_ref, copier(slot), lambda cp: cp.start())
        _tail_copies(tail_ref, copier(slot), lambda cp: cp.wait())


def _dispatch(meta, tail, loff, rt, h2, n_pad):
    T, D = h2.shape
    nt = meta.shape[0]
    tm = T // nt
    return pl.pallas_call(
        _dispatch_body,
        grid=(nt,),
        in_specs=[pl.BlockSpec((1, 1, meta.shape[2]), lambda i: (i, 0, 0), memory_space=pltpu.SMEM),
                  pl.BlockSpec((1, 1, meta.shape[2]), lambda i: (jnp.maximum(i - 1, 0), 0, 0),
                               memory_space=pltpu.SMEM),
                  pl.BlockSpec((1, 1, tail.shape[2]), lambda i: (0, 0, 0), memory_space=pltpu.SMEM),
                  pl.BlockSpec((1, 1, LANES), lambda i: (i, 0, 0)),
                  pl.BlockSpec((tm, LANES), lambda i: (i, 0)),
                  pl.BlockSpec((tm, D), lambda i: (i, 0))],
        out_specs=pl.BlockSpec(memory_space=pl.ANY),
        out_shape=jax.ShapeDtypeStruct((n_pad, XS_W), U32),
        scratch_shapes=[pltpu.VMEM((2, _local_rows(tm), XS_W), U32), pltpu.SemaphoreType.DMA((2,))],
        compiler_params=pltpu.CompilerParams(
            dimension_semantics=("arbitrary",), vmem_limit_bytes=VMEM_LIMIT),
        name="moe_dispatch",
    )(meta, meta, tail, loff, rt, h2)


EXPERT_PARTS = 2


def _expert_body(be_ref, nused_ref, xs_ref, wg_ref, wu_ref, wd_ref, ys_ref, wgb_ref, wub_ref, wdb_ref):
    i = pl.program_id(0)

    @pl.when((i == 0) | (be_ref[i] != be_ref[jnp.maximum(i - 1, 0)]))
    def _():
        wgb_ref[...] = wg_ref[0].astype(BF16)
        wub_ref[...] = wu_ref[0].astype(BF16)
        wdb_ref[...] = wd_ref[0].astype(BF16)

    live = i < nused_ref[0]

    @pl.when(jnp.logical_not(live))
    def _():
        ys_ref[...] = jnp.zeros_like(ys_ref)

    @pl.when(live)
    def _():
        dot = functools.partial(jnp.dot, preferred_element_type=F32)
        part = MOE_BLK // EXPERT_PARTS
        half = D_MODEL // 2
        parts = [slice(k * part, (k + 1) * part) for k in range(EXPERT_PARTS)]
        hs = []
        for rows in parts:
            ha, hb = _unpack_pair(xs_ref[rows, :half])
            hs.append(jnp.concatenate([ha, hb], axis=1).astype(BF16))
        gs = [dot(h, wgb_ref[...]) for h in hs]
        us = [dot(h, wub_ref[...]) for h in hs]
        acts = [(g * _sigmoid(g) * u).astype(BF16) for g, u in zip(gs, us)]
        expert = (be_ref[i] % N_EXPERTS).astype(F32)
        for rows, act in zip(parts, acts):
            rec = lax.bitcast_convert_type(xs_ref[rows, half:], F32)
            w = jnp.where(rec[:, SLOT_E1:SLOT_E1 + 1] == expert,
                          rec[:, SLOT_W1:SLOT_W1 + 1] + rec[:, SLOT_W1 + 1:SLOT_W1 + 2],
                          rec[:, SLOT_W2:SLOT_W2 + 1] + rec[:, SLOT_W2 + 1:SLOT_W2 + 2])
            y = dot(act, wdb_ref[...]) * w
            ys_ref[rows] = _pack_pair(y[:, :half], y[:, half:])


def _experts(block_expert, nused, xs, wg, wu, wd):
    n_pad = xs.shape[0]
    blk_in = pl.BlockSpec((MOE_BLK, XS_W), lambda i, be, nu: (jnp.minimum(i, nu[0] - 1), 0))
    blk_out = pl.BlockSpec((MOE_BLK, D_MODEL // 2), lambda i, be, nu: (i, 0))
    wspec = lambda a: pl.BlockSpec((1,) + a.shape[1:], lambda i, be, nu: (be[i], 0, 0))
    return pl.pallas_call(
        _expert_body,
        grid_spec=pltpu.PrefetchScalarGridSpec(
            num_scalar_prefetch=2,
            grid=(n_pad // MOE_BLK,),
            in_specs=[blk_in, wspec(wg), wspec(wu), wspec(wd)],
            out_specs=blk_out,
            scratch_shapes=[pltpu.VMEM(wg.shape[1:], BF16), pltpu.VMEM(wu.shape[1:], BF16),
                            pltpu.VMEM(wd.shape[1:], BF16)],
        ),
        out_shape=jax.ShapeDtypeStruct((n_pad, D_MODEL // 2), U32),
        compiler_params=pltpu.CompilerParams(
            dimension_semantics=("arbitrary",), vmem_limit_bytes=VMEM_LIMIT),
        name="moe_experts",
    )(block_expert, nused, xs, wg, wu, wd)


def _combine_body(meta_ref, next_meta_ref, loff_ref, rt_ref, x_ref, fg_ref, ys_ref, o_ref, buf_ref, sem,
                  *, final_norm):
    i = pl.program_id(0)
    slot = i % 2

    def copier(k):
        def make_copy(loc_row, glob_row, n):
            return pltpu.make_async_copy(ys_ref.at[pl.ds(glob_row, n)], buf_ref.at[k, pl.ds(loc_row, n)],
                                         sem.at[k])
        return make_copy

    rows = buf_ref.shape[1]

    @pl.when(i == 0)
    def _():
        buf_ref[...] = jnp.zeros_like(buf_ref)
        _segment_starts(meta_ref, copier(0))

    @pl.when(i + 1 < pl.num_programs(0))
    def _():
        _segment_starts(next_meta_ref, copier(1 - slot))

    l1, l2 = _local_pos(rt_ref[...], loff_ref[0])
    col = lax.broadcasted_iota(I32, (rt_ref.shape[0], rows), 1)
    pt = _one_hot(col == l1.astype(I32)) + _one_hot(col == l2.astype(I32))
    _segment_wait(meta_ref, copier(slot), rows)
    yl = jnp.concatenate(_unpack_pair(buf_ref[slot]), axis=1).astype(BF16)
    out = x_ref[...] + jnp.dot(pt, yl, preferred_element_type=F32)
    if final_norm:
        out = _rms(out) * fg_ref[...]
    o_ref[...] = out


def _combine(meta, loff, rt, x, fg, ys, final_norm):
    T, D = x.shape
    nt = meta.shape[0]
    tm = T // nt
    return pl.pallas_call(
        functools.partial(_combine_body, final_norm=final_norm),
        grid=(nt,),
        in_specs=[pl.BlockSpec((1, 1, meta.shape[2]), lambda i: (i, 0, 0), memory_space=pltpu.SMEM),
                  pl.BlockSpec((1, 1, meta.shape[2]), lambda i: (jnp.minimum(i + 1, nt - 1), 0, 0),
                               memory_space=pltpu.SMEM),
                  pl.BlockSpec((1, 1, LANES), lambda i: (i, 0, 0)),
                  pl.BlockSpec((tm, LANES), lambda i: (i, 0)),
                  pl.BlockSpec((tm, D), lambda i: (i, 0)),
                  pl.BlockSpec((1, D), lambda i: (0, 0)),
                  pl.BlockSpec(memory_space=pl.ANY)],
        out_specs=pl.BlockSpec((tm, D), lambda i: (i, 0)),
        out_shape=jax.ShapeDtypeStruct((T, D), F32),
        scratch_shapes=[pltpu.VMEM((2, _local_rows(tm), D // 2), U32), pltpu.SemaphoreType.DMA((2,))],
        compiler_params=pltpu.CompilerParams(
            dimension_semantics=("arbitrary",), vmem_limit_bytes=VMEM_LIMIT),
        name="moe_combine",
    )(meta, meta, loff, rt, x, fg, ys)


def _rope_tables(L):
    half = HEAD_DIM // 2
    inv = ROPE_THETA ** (-jnp.arange(half, dtype=F32) / half)
    ang = jnp.arange(L, dtype=F32)[:, None] * inv[None, :]
    lane = np.arange(LANES)
    idx = lane % half
    first = jnp.asarray((lane % HEAD_DIM) < half)
    cos = jnp.cos(ang)[:, idx]
    sin = jnp.sin(ang)[:, idx]
    return cos, jnp.where(first, -sin, 0.0), jnp.where(first, 0.0, sin)


def _moe_layout(cnt, T, layer):
    nt = cnt.shape[0] // SUBLANES
    c = cnt.reshape(nt, SUBLANES, LANES)[:, 0, :N_EXPERTS].astype(I32)
    c8 = (c + SUBLANES - 1) // SUBLANES * SUBLANES
    tile_off = jnp.cumsum(c8, axis=0) - c8
    total = jnp.sum(c8, axis=0)
    padded = (total + MOE_BLK - 1) // MOE_BLK * MOE_BLK
    pend = jnp.cumsum(padded)
    dst = (pend - padded)[None, :] + tile_off
    loff = jnp.cumsum(c8, axis=1) - c8
    n8 = c8 // SUBLANES
    per_big = SEG_BIG // SUBLANES
    total8 = jnp.broadcast_to(jnp.sum(n8, axis=1, keepdims=True), n8.shape)
    meta = jnp.concatenate([dst, loff, n8 // per_big, n8 % per_big, total8], axis=1).reshape(nt, 1, META_W)
    loff_f = jnp.pad(loff.astype(F32), ((0, 0), (0, LANES - N_EXPERTS))).reshape(nt, 1, LANES)
    n_pad = 2 * T + nt * N_EXPERTS * SUBLANES + N_EXPERTS * MOE_BLK
    n_pad = (n_pad + MOE_BLK - 1) // MOE_BLK * MOE_BLK
    starts = jnp.arange(n_pad // MOE_BLK, dtype=I32) * MOE_BLK
    block_expert = jnp.minimum(jnp.sum(starts[:, None] >= pend[None, :], axis=1), N_EXPERTS - 1)
    block_expert = block_expert.astype(I32) + layer * N_EXPERTS
    nused = (pend[-1:] // MOE_BLK).astype(I32)
    tail = jnp.concatenate([pend - padded + total, (padded - total) // SUBLANES, nused,
                            jnp.full((1,), n_pad // MOE_BLK, I32)]).reshape(1, 1, 2 * N_EXPERTS + 2)
    return meta, tail, loff_f, block_expert, nused, n_pad


def kernel(x, attn_norm_g, w_in, sink_logits, hgrn_lb_table, hgrn_norm_g, w_up_attn, w_up_hgrn, w_out,
           ffn_norm_g, w_router_group, b_router_group, w_router_expert, b_router_expert,
           w_expert_gate, w_expert_up, w_expert_down, final_norm_g):
    B, L, D = x.shape
    T = B * L
    depth = w_in.shape[0]
    sm = jax.nn.softmax(hgrn_lb_table.astype(F32), axis=0)
    lower_bounds = jnp.cumsum(sm, axis=0) - sm[:1]
    cos, s1, s2 = _rope_tables(L)
    tm_post = min(TM_POST, T)
    ltri = jnp.tril(jnp.ones((tm_post // POST_PARTS,) * 2, BF16), -1)
    fin_g = final_norm_g.reshape(1, D).astype(F32)
    w_in_bf16 = w_in.astype(BF16)
    all_layers = lambda w: w.reshape((depth * N_EXPERTS,) + w.shape[2:])
    wg_all, wu_all, wd_all = all_layers(w_expert_gate), all_layers(w_expert_up), all_layers(w_expert_down)

    for l in range(depth):
        qa, ka, va, qb, sf, sb, ib, og, ga, gb = _inproj(
            x, attn_norm_g[l].reshape(1, D), w_in_bf16, l, cos, s1, s2)
        ya = _attention(qa, ka, va, sink_logits[l].astype(F32) * LOG2E)
        of, orv = _hgrn(qb, sf, sb, ib, lower_bounds[l])

        rw = jnp.concatenate(
            [w_router_group[l], jnp.transpose(w_router_expert[l], (1, 0, 2)).reshape(D, N_EXPERTS)], axis=1)
        rw = jnp.pad(rw.astype(F32), ((0, 0), (0, LANES - rw.shape[1])))
        rwh = rw.astype(BF16)
        rw2 = jnp.concatenate([rwh, (rw - rwh.astype(F32)).astype(BF16)], axis=1)
        rb = jnp.concatenate([b_router_group[l], b_router_expert[l].reshape(-1)]).astype(F32)
        rb = jnp.pad(rb, (0, LANES - rb.shape[0])).reshape(1, LANES)
        flat = lambda a: a.reshape(T, a.shape[-1])
        xn, h2, rt, cnt = _post(
            flat(x), flat(ya), flat(of), flat(orv), flat(og), flat(ga), flat(gb),
            w_up_attn[l].astype(BF16), w_up_hgrn[l].astype(BF16), w_out[l].astype(BF16),
            hgrn_norm_g[l].reshape(1, HGRN_W).astype(F32), ffn_norm_g[l].reshape(1, D).astype(F32),
            rw2, rb, ltri)

        meta, tail, loff, block_expert, nused, n_pad = _moe_layout(cnt, T, l)
        xs = _dispatch(meta, tail, loff, rt, h2, n_pad)
        ys = _experts(block_expert, nused, xs, wg_all, wu_all, wd_all)
        x = _combine(meta, loff, rt, xn, fin_g, ys, l == depth - 1).reshape(B, L, D)
    return x
```

```python
import functools

import numpy as np
import jax
import jax.numpy as jnp
from jax import lax
from jax.experimental import pallas as pl
from jax.experimental.pallas import tpu as pltpu

F32 = jnp.float32
BF16 = jnp.bfloat16
U32 = jnp.uint32
I32 = jnp.int32

D_MODEL = 1024
N_Q_HEADS = 8
N_KV_HEADS = 2
HEAD_DIM = 64
WINDOW = 128
ROPE_THETA = 10000.0
MASK_VALUE = -1e30
HGRN_HEADS = 4
HGRN_DIM = 128
LOGF_MIN = -4.0
F_MIN = float(np.exp(LOGF_MIN))
LOG2E = float(np.log2(np.e))
Q_SCALE = HEAD_DIM ** -0.5 * LOG2E
LB_FLOOR = 1e-30
N_GROUPS = 4
EXPERTS_PER_GROUP = 8
N_EXPERTS = N_GROUPS * EXPERTS_PER_GROUP
D_EXPERT = 512
NORM_EPS = 1e-6

ATTN_Q_W = N_Q_HEADS * HEAD_DIM
ATTN_KV_W = N_KV_HEADS * HEAD_DIM
HGRN_W = HGRN_HEADS * HGRN_DIM
OFF_QA = 0
OFF_KA = OFF_QA + ATTN_Q_W
OFF_VA = OFF_KA + ATTN_KV_W
OFF_QB = OFF_VA + ATTN_KV_W
OFF_ZF = OFF_QB + HGRN_W
OFF_ZB = OFF_ZF + HGRN_W
OFF_IB = OFF_ZB + HGRN_W
OFF_OG = OFF_IB + HGRN_W
OFF_GA = OFF_OG + HGRN_W
OFF_GB = OFF_GA + D_MODEL

LANES = 128
VMEM_LIMIT = 56 * 1024 * 1024

TM_PROJ = 512
TQ_ATTN = 1024
TH_HGRN = 1024
TM_POST = 512
MOE_BLK = 512


def _sigmoid(z):
    return 1.0 / (1.0 + jnp.exp(-z))


def _rms(x):
    return x * lax.rsqrt(jnp.mean(x * x, axis=-1, keepdims=True) + NORM_EPS)


def _nt(a, b):
    return lax.dot_general(a, b, (((1,), (1,)), ((), ())), preferred_element_type=F32)


def _pack_pair(a, b):
    ua = lax.bitcast_convert_type(a.astype(BF16).astype(F32), U32)
    ub = lax.bitcast_convert_type(b.astype(BF16).astype(F32), U32)
    return ua | (ub >> 16)


def _unpack_pair(u):
    hi = lax.bitcast_convert_type(u & jnp.uint32(0xFFFF0000), F32)
    lo = lax.bitcast_convert_type(u << 16, F32)
    return hi, lo


def _inproj_body(x_ref, g_ref, w_ref, cos_ref, s1_ref, s2_ref,
                 qa_ref, ka_ref, va_ref, qb_ref, sf_ref, sb_ref, ib_ref, og_ref, ga_ref, gb_ref):
    h = (_rms(x_ref[0]) * g_ref[...]).astype(BF16)

    def proj(off, width):
        return jnp.dot(h, w_ref[:, off:off + width], preferred_element_type=F32)

    cos, s1, s2 = cos_ref[...], s1_ref[...], s2_ref[...]

    def rope(t):
        return t * cos + pltpu.roll(t, 96, 1) * s1 + pltpu.roll(t, 32, 1) * s2

    ga_ref[0] = _sigmoid(proj(OFF_GA, D_MODEL)).astype(BF16)
    gb_ref[0] = _sigmoid(proj(OFF_GB, D_MODEL)).astype(BF16)
    t = proj(OFF_OG, HGRN_W)
    og_ref[0] = (t * _sigmoid(t)).astype(BF16)
    t = proj(OFF_QB, HGRN_W)
    qb_ref[0] = (t * _sigmoid(t)).astype(BF16)
    sf_ref[0] = _sigmoid(proj(OFF_ZF, HGRN_W)).astype(BF16)
    sb_ref[0] = _sigmoid(proj(OFF_ZB, HGRN_W)).astype(BF16)
    q = proj(OFF_QA, ATTN_Q_W)
    for c in range(ATTN_Q_W // LANES):
        sl = slice(c * LANES, (c + 1) * LANES)
        qa_ref[0, :, sl] = (rope(q[:, sl]) * Q_SCALE).astype(BF16)
    k = rope(proj(OFF_KA, ATTN_KV_W))
    ka_ref[0, :, :LANES] = k.astype(BF16)
    ka_ref[0, :, LANES:] = pltpu.roll(k, 64, 1).astype(BF16)
    v = proj(OFF_VA, ATTN_KV_W)
    va_ref[0, :, :LANES] = v.astype(BF16)
    va_ref[0, :, LANES:] = pltpu.roll(v, 64, 1).astype(BF16)
    ib_ref[0] = proj(OFF_IB, HGRN_W).astype(BF16)


def _inproj(x, g, w_all, layer, cos, s1, s2):
    B, L, D = x.shape
    tm = min(TM_PROJ, L)
    widths = (ATTN_Q_W, 2 * ATTN_KV_W, 2 * ATTN_KV_W, HGRN_W, HGRN_W, HGRN_W, HGRN_W, HGRN_W,
              D_MODEL, D_MODEL)
    tab = pl.BlockSpec((tm, LANES), lambda b, i: (i, 0))
    return pl.pallas_call(
        _inproj_body,
        grid=(B, L // tm),
        in_specs=[
            pl.BlockSpec((1, tm, D), lambda b, i: (b, i, 0)),
            pl.BlockSpec((1, D), lambda b, i: (0, 0)),
            pl.BlockSpec((None,) + w_all.shape[1:], lambda b, i: (layer, 0, 0)),
            tab, tab, tab,
        ],
        out_specs=[pl.BlockSpec((1, tm, w), lambda b, i: (b, i, 0)) for w in widths],
        out_shape=[jax.ShapeDtypeStruct((B, L, w), BF16) for w in widths],
        compiler_params=pltpu.CompilerParams(
            dimension_semantics=("arbitrary", "arbitrary"), vmem_limit_bytes=VMEM_LIMIT),
        name="inproj",
    )(x, g, w_all, cos, s1, s2)


def _attn_body(sink_ref, q_ref, kp_ref, km_ref, kn_ref, vp_ref, vm_ref, vn_ref, o_ref, *, tq, seq):
    i = pl.program_id(1)
    kw = jnp.concatenate([kp_ref[0], km_ref[0], kn_ref[0]], axis=0)
    vw = jnp.concatenate([vp_ref[0], vm_ref[0], vn_ref[0]], axis=0)
    lo = lax.broadcasted_iota(I32, (1, LANES), 1) < HEAD_DIM
    zero = jnp.zeros((), BF16)

    def variants(t):
        a, b = t[:, :LANES], t[:, LANES:]
        return ((jnp.where(lo, a, zero), jnp.where(lo, zero, b)),
                (jnp.where(lo, b, zero), jnp.where(lo, zero, a)))

    kvar, vvar = variants(kw), variants(vw)
    row = lax.broadcasted_iota(I32, (WINDOW, 3 * WINDOW), 0)
    col = lax.broadcasted_iota(I32, (WINDOW, 3 * WINDOW), 1)
    band = (col >= row) & (col <= row + 2 * WINDOW)
    ncol = ATTN_Q_W // LANES
    group = lambda c: (2 * c) // (N_Q_HEADS // N_KV_HEADS)
    heads = [(c, hh) for c in range(ncol) for hh in range(2)]
    for j in range(tq // WINDOW):
        kpos = col + (i * tq + (j - 1) * WINDOW)
        bias = jnp.where(band & (kpos >= 0) & (kpos < seq), 0.0, MASK_VALUE)
        rows = slice(j * WINDOW, (j + 1) * WINDOW)
        win = slice(j * WINDOW, (j + 3) * WINDOW)
        scores = [_nt(q_ref[0, rows, c * LANES:(c + 1) * LANES], kvar[group(c)][hh][win]) for c, hh in heads]
        probs, denoms = [], []
        for (c, hh), s in zip(heads, scores):
            s = jnp.concatenate([s[:, :WINDOW] + bias[:, :WINDOW], s[:, WINDOW:2 * WINDOW],
                                 s[:, 2 * WINDOW:] + bias[:, 2 * WINDOW:]], axis=1)
            sk = sink_ref[2 * c + hh]
            m = jnp.maximum(jnp.max(s, axis=1, keepdims=True), sk)
            p = jnp.exp2(s - m)
            denoms.append(jnp.sum(p, axis=1, keepdims=True) + jnp.exp2(sk - m))
            probs.append(p.astype(BF16))
        for c in range(ncol):
            g = group(c)
            pv = jnp.dot(jnp.concatenate(probs[2 * c:2 * c + 2], axis=1),
                         jnp.concatenate([vvar[g][0][win], vvar[g][1][win]], axis=0),
                         preferred_element_type=F32)
            inv = jnp.where(lo, 1.0 / denoms[2 * c], 1.0 / denoms[2 * c + 1])
            o_ref[0, rows, c * LANES:(c + 1) * LANES] = (pv * inv).astype(BF16)


def _attention(qa, ka, va, sink):
    B, L, _ = qa.shape
    tq = min(TQ_ATTN, L)
    r = tq // WINDOW
    nb = L // WINDOW
    kvw = 2 * ATTN_KV_W
    main = pl.BlockSpec((1, tq, kvw), lambda b, i: (b, i, 0))
    prev = pl.BlockSpec((1, WINDOW, kvw), lambda b, i: (b, jnp.maximum(i * r - 1, 0), 0))
    nxt = pl.BlockSpec((1, WINDOW, kvw), lambda b, i: (b, jnp.minimum(i * r + r, nb - 1), 0))
    return pl.pallas_call(
        functools.partial(_attn_body, tq=tq, seq=L),
        grid=(B, L // tq),
        in_specs=[
            pl.BlockSpec(memory_space=pltpu.SMEM),
            pl.BlockSpec((1, tq, ATTN_Q_W), lambda b, i: (b, i, 0)),
            prev, main, nxt, prev, main, nxt,
        ],
        out_specs=pl.BlockSpec((1, tq, ATTN_Q_W), lambda b, i: (b, i, 0)),
        out_shape=jax.ShapeDtypeStruct((B, L, ATTN_Q_W), BF16),
        compiler_params=pltpu.CompilerParams(
            dimension_semantics=("arbitrary", "arbitrary"), vmem_limit_bytes=VMEM_LIMIT),
        name="win_attn",
    )(sink, qa, ka, ka, ka, va, va, va)


HB = 128
LEAF = 32
HGRN_LEVELS = tuple(LEAF << k for k in range((HB // LEAF).bit_length() - 1))
LEAF_SHIFT = 0.5 * LEAF * (-LOGF_MIN) * LOG2E


def _hgrn_masks(reverse):
    r = lax.broadcasted_iota(I32, (HB, HB), 0)
    c = lax.broadcasted_iota(I32, (HB, HB), 1)
    if reverse:
        r, c = c, r
    tri = jnp.where(c <= r, 1.0, 0.0).astype(BF16)
    same = lambda w: (r >> (w.bit_length() - 1)) == (c >> (w.bit_length() - 1))
    cross = lambda w: ((r & w) != 0) & ((c & w) == 0)
    lev = jnp.full((HB, HB), len(HGRN_LEVELS), I32)
    for k, w in reversed(list(enumerate(HGRN_LEVELS[:-1]))):
        lev = jnp.where(same(2 * w) & cross(w), k + 1, lev)
    lev = jnp.where(same(LEAF) & (c <= r), 0, lev)
    return tri, lev


def _hgrn_gates(s, lb, lbc, tri):
    f = lbc + (1.0 - lb) * s
    logf = jnp.maximum(jnp.log(f), LOGF_MIN)
    kk = 1.0 - jnp.maximum(f, F_MIN)
    hi = logf.astype(BF16)
    mid = (logf - hi.astype(F32)).astype(BF16)
    dot = functools.partial(jnp.dot, preferred_element_type=F32)
    return kk, (dot(tri, hi) + dot(tri, mid)) * LOG2E


def _hgrn_scores(q, kk, cum, v, st_ref, reverse):
    def ref_row(idx):
        return cum[idx:idx + 1]

    total = ref_row(0) if reverse else ref_row(HB - 1)
    q_blk = q * jnp.exp2(cum)
    k_blk = kk * jnp.exp2(total - cum)

    ql, kl = [], []
    for c in range(HB // LEAF):
        rows = slice(c * LEAF, (c + 1) * LEAF)
        edge = (c + 1) * LEAF if reverse else c * LEAF - 1
        loc = (cum[rows] - ref_row(edge) if 0 <= edge < HB else cum[rows]) + LEAF_SHIFT
        ql.append(q[rows] * jnp.exp2(loc))
        kl.append(kk[rows] * jnp.exp2(-loc))
    pairs = [(jnp.concatenate(ql, axis=0), jnp.concatenate(kl, axis=0))]
    for w in HGRN_LEVELS:
        qp, kp = [], []
        zeros = jnp.zeros((w, HB), F32)
        for s0 in range(0, HB, 2 * w):
            first, second = slice(s0, s0 + w), slice(s0 + w, s0 + 2 * w)
            if reverse:
                ref = ref_row(s0 + w)
                qp += [q[first] * jnp.exp2(cum[first] - ref), zeros]
                kp += [zeros, kk[second] * jnp.exp2(ref - cum[second])]
            else:
                ref = ref_row(s0 + w - 1)
                qp += [zeros, q[second] * jnp.exp2(cum[second] - ref)]
                kp += [kk[first] * jnp.exp2(ref - cum[first]), zeros]
        pairs.append((jnp.concatenate(qp, axis=0), jnp.concatenate(kp, axis=0)))
    scores = [_nt(qq.astype(BF16), kq.astype(BF16)) for qq, kq in pairs]
    st = st_ref[...]
    o_state = _nt(q_blk.astype(BF16), st.astype(BF16))
    st_ref[...] = st * jnp.exp2(total) + lax.dot_general(
        v, k_blk.astype(BF16), (((0,), (0,)), ((), ())), preferred_element_type=F32)
    return scores, o_state


def _hgrn_output(scores, o_state, v, lev):
    a = scores[-1]
    for k in reversed(range(len(scores) - 1)):
        a = jnp.where(lev == k, scores[k], a)
    return jnp.dot(a.astype(BF16), v, preferred_element_type=F32) + o_state


def _hgrn_body(lb_ref, qf_ref, sf_ref, vf_ref, qr_ref, sr_ref, vr_ref, of_ref, or_ref, st_ref, *, th):
    @pl.when(pl.program_id(1) == 0)
    def _():
        st_ref[...] = jnp.zeros_like(st_ref)

    lb = lb_ref[...]
    lbc = jnp.maximum(lb, LB_FLOOR)
    masks = (_hgrn_masks(False), _hgrn_masks(True))
    nblk = th // HB
    streams = ((qf_ref, sf_ref, vf_ref, of_ref), (qr_ref, sr_ref, vr_ref, or_ref))

    def body(t, carry):
        rows = [pl.multiple_of((nblk - 1 - t if d else t) * HB, HB) for d in range(2)]
        units = [(d, h * HGRN_DIM) for d in range(2) for h in range(HGRN_HEADS)]

        def load(d, k, lane0):
            return streams[d][k][0, pl.ds(rows[d], HB), lane0:lane0 + HGRN_DIM]

        gates = [_hgrn_gates(load(d, 1, c0).astype(F32), lb[d:d + 1, c0:c0 + HGRN_DIM],
                             lbc[d:d + 1, c0:c0 + HGRN_DIM], masks[d][0]) for d, c0 in units]
        mids = [_hgrn_scores(load(d, 0, c0).astype(F32), kk, cum, load(d, 2, c0),
                             st_ref.at[d, c0 // HGRN_DIM], bool(d))
                for (d, c0), (kk, cum) in zip(units, gates)]
        for (d, c0), (scores, o_state) in zip(units, mids):
            o = _hgrn_output(scores, o_state, load(d, 2, c0), masks[d][1])
            streams[d][3][0, pl.ds(rows[d], HB), c0:c0 + HGRN_DIM] = o.astype(BF16)
        return carry

    lax.fori_loop(0, nblk, body, 0)


def _hgrn(qb, sf, sb, ib, lb):
    B, L, _ = qb.shape
    th = min(TH_HGRN, L)
    n = L // th
    fwd = pl.BlockSpec((1, th, HGRN_W), lambda b, i: (b, i, 0))
    rev = pl.BlockSpec((1, th, HGRN_W), lambda b, i: (b, n - 1 - i, 0))
    return pl.pallas_call(
        functools.partial(_hgrn_body, th=th),
        grid=(B, n),
        in_specs=[pl.BlockSpec((2, HGRN_W), lambda b, i: (0, 0)), fwd, fwd, fwd, rev, rev, rev],
        out_specs=[fwd, rev],
        out_shape=[jax.ShapeDtypeStruct((B, L, HGRN_W), BF16)] * 2,
        scratch_shapes=[pltpu.VMEM((2, HGRN_HEADS, HGRN_DIM, HGRN_DIM), F32)],
        compiler_params=pltpu.CompilerParams(
            dimension_semantics=("arbitrary", "arbitrary"), vmem_limit_bytes=VMEM_LIMIT),
        name="hgrn2",
    )(lb, qb, sf, ib, qb, sb, ib)


POST_PARTS = 2


def _post_mix(rows, x_ref, ya_ref, of_ref, or_ref, og_ref, ga_ref, gb_ref, wua_ref, wuh_ref, wo_ref,
              hg_ref, fg_ref, rw_ref, rb_ref, xn_ref, h2_ref):
    dot = functools.partial(jnp.dot, preferred_element_type=F32)
    o = of_ref[rows].astype(F32) + or_ref[rows].astype(F32)
    yh = (_rms(o) * hg_ref[...] * og_ref[rows].astype(F32)).astype(BF16)
    merged = (ga_ref[rows].astype(F32) * dot(ya_ref[rows], wua_ref[...])
              + gb_ref[rows].astype(F32) * dot(yh, wuh_ref[...]))
    xn = x_ref[rows] + dot(merged.astype(BF16), wo_ref[...])
    xn_ref[rows] = xn
    h2 = _rms(xn) * fg_ref[...]
    hi = h2.astype(BF16)
    h2_ref[rows] = hi
    lo = (h2 - hi.astype(F32)).astype(BF16)
    a = dot(hi, rw_ref[...])
    return a[:, :LANES] + a[:, LANES:] + dot(lo, rw_ref[:, :LANES]) + rb_ref[...]


def _post_route(logits):
    lane = lax.broadcasted_iota(I32, logits.shape, 1)
    lanef = lane.astype(F32)
    big = jnp.float32(1e9)
    ninf = jnp.float32(-jnp.inf)
    red = dict(axis=1, keepdims=True)
    gmask = lane < N_GROUPS
    mg = jnp.max(jnp.where(gmask, logits, ninf), **red)
    p_top = 1.0 / jnp.sum(jnp.where(gmask, jnp.exp(logits - mg), 0.0), **red)
    gi = jnp.min(jnp.where(gmask & (logits == mg), lanef, big), **red)
    emask = ((lane >= N_GROUPS) & (lane < N_GROUPS + N_EXPERTS)
             & (((lane - N_GROUPS) >> 3).astype(F32) == gi))
    le = jnp.where(emask, logits, ninf)
    m1 = jnp.max(le, **red)
    i1 = jnp.min(jnp.where(le == m1, lanef, big), **red)
    le2 = jnp.where(lanef == i1, ninf, le)
    m2 = jnp.max(le2, **red)
    i2 = jnp.min(jnp.where(le2 == m2, lanef, big), **red)
    r = jnp.exp(m2 - m1)
    w1 = 1.0 / (1.0 + r)
    w2 = r * w1
    e1 = i1 - N_GROUPS
    e2 = i2 - N_GROUPS
    is1, is2 = lanef == e1, lanef == e2
    return (e1, e2, p_top * w1, p_top * w2), is1, is2, jnp.where(is1 | is2, 1.0, 0.0)


def _post_body(x_ref, ya_ref, of_ref, or_ref, og_ref, ga_ref, gb_ref, wua_ref, wuh_ref, wo_ref,
               hg_ref, fg_ref, rw_ref, rb_ref, ltri_ref,
               xn_ref, h2_ref, rt_ref, cnt_ref):
    tm = x_ref.shape[0]
    part = tm // POST_PARTS
    parts = [slice(k * part, (k + 1) * part) for k in range(POST_PARTS)]
    logits = [_post_mix(rows, x_ref, ya_ref, of_ref, or_ref, og_ref, ga_ref, gb_ref, wua_ref, wuh_ref,
                        wo_ref, hg_ref, fg_ref, rw_ref, rb_ref, xn_ref, h2_ref) for rows in parts]
    routed = [_post_route(lg) for lg in logits]
    before = jnp.zeros((1, LANES), F32)
    red = dict(axis=1, keepdims=True)
    lane = lax.broadcasted_iota(I32, (part, LANES), 1)
    for rows, ((e1, e2, w1, w2), is1, is2, onehot) in zip(parts, routed):
        pref = jnp.dot(ltri_ref[...], onehot.astype(BF16), preferred_element_type=F32) + before
        before = before + jnp.sum(onehot, axis=0, keepdims=True)
        rank1 = jnp.sum(jnp.where(is1, pref, 0.0), **red)
        rank2 = jnp.sum(jnp.where(is2, pref, 0.0), **red)
        rt = jnp.zeros((part, LANES), F32)
        for idx, val in enumerate((e1, e2, rank1, rank2, w1, w2)):
            rt = jnp.where(lane == idx, val, rt)
        rt_ref[rows] = rt
    cnt_ref[...] = jnp.broadcast_to(before, cnt_ref.shape)


RT_E1, RT_E2, RT_RANK1, RT_RANK2, RT_W1, RT_W2 = range(6)
SUBLANES = 8


def _post(x, ya, of, orv, og, ga, gb, wua, wuh, wo, hg, fg, rw, rb, ltri):
    T, D = x.shape
    tm = ltri.shape[0] * POST_PARTS
    row = lambda w: pl.BlockSpec((tm, w), lambda i: (i, 0))
    full = lambda a: pl.BlockSpec(a.shape, lambda i: (0,) * a.ndim)
    return pl.pallas_call(
        _post_body,
        grid=(T // tm,),
        in_specs=[row(D), row(ATTN_Q_W), row(HGRN_W), row(HGRN_W), row(HGRN_W), row(D), row(D),
                  full(wua), full(wuh), full(wo), full(hg), full(fg), full(rw), full(rb), full(ltri)],
        out_specs=[row(D), row(D), row(LANES), pl.BlockSpec((SUBLANES, LANES), lambda i: (i, 0))],
        out_shape=[jax.ShapeDtypeStruct((T, D), F32), jax.ShapeDtypeStruct((T, D), BF16),
                   jax.ShapeDtypeStruct((T, LANES), F32),
                   jax.ShapeDtypeStruct((T // tm * SUBLANES, LANES), F32)],
        compiler_params=pltpu.CompilerParams(
            dimension_semantics=("arbitrary",), vmem_limit_bytes=VMEM_LIMIT),
        name="post_mixer_router",
    )(x, ya, of, orv, og, ga, gb, wua, wuh, wo, hg, fg, rw, rb, ltri)


XS_W = D_MODEL // 2 + LANES
SLOT_W1, SLOT_W2, SLOT_E1, SLOT_E2 = 0, 2, 4, 5
SEG_BIG = 32
SEG_SMALL_MAX = SEG_BIG // SUBLANES - 1
SEG_SLOT_ROWS = SEG_BIG + SEG_SMALL_MAX * SUBLANES
DUMP_ROWS = N_EXPERTS * SEG_SLOT_ROWS
DUMP_BLOCKS = -(-2 * DUMP_ROWS // MOE_BLK)
META_DST, META_LOFF, META_NBIG, META_NSMALL, META_EXTRA = (k * N_EXPERTS for k in range(5))
META_W = 5 * N_EXPERTS


def _local_rows(tm):
    return 2 * tm + N_EXPERTS * SUBLANES


def _local_pos(rt, loff_row):
    lanef = lax.broadcasted_iota(I32, rt.shape, 1).astype(F32)
    out = []
    for e_lane, r_lane in ((RT_E1, RT_RANK1), (RT_E2, RT_RANK2)):
        e = rt[:, e_lane:e_lane + 1]
        off = jnp.sum(jnp.where(lanef == e, loff_row, 0.0), axis=1, keepdims=True)
        out.append(off + rt[:, r_lane:r_lane + 1])
    return out


def _one_hot(hit):
    return jnp.where(hit, 1.0, 0.0).astype(BF16)


def _segment_starts(meta_ref, make_copy, dummy):
    def segment(e):
        return (meta_ref[0, 0, META_DST + e], meta_ref[0, 0, META_LOFF + e],
                meta_ref[0, 0, META_NBIG + e], meta_ref[0, 0, META_NSMALL + e])

    for e in range(N_EXPERTS):
        dst, loff, n_big, n_small = segment(e)

        def piece(live, off, slot_row, n):
            d_loc, d_glob = dummy(e * SEG_SLOT_ROWS + slot_row)
            loc = jnp.where(live, loff + off, d_loc)
            glob = jnp.where(live, dst + off, d_glob)
            make_copy(pl.multiple_of(loc, SUBLANES), pl.multiple_of(glob, SUBLANES), n).start()

        piece(n_big > 0, 0, 0, SEG_BIG)
        for k in range(SEG_SMALL_MAX):
            piece(k < n_small, n_big * SEG_BIG + k * SUBLANES, SEG_BIG + k * SUBLANES, SUBLANES)

    @pl.when(meta_ref[0, 0, META_EXTRA] > 0)
    def _():
        for e in range(N_EXPERTS):
            dst, loff, n_big, _ = segment(e)

            def more(k, c):
                o = k * SEG_BIG
                make_copy(pl.multiple_of(loff + o, SUBLANES), pl.multiple_of(dst + o, SUBLANES), SEG_BIG).start()
                return c

            lax.fori_loop(1, n_big, more, 0)


def _segment_wait(meta_ref, make_copy):
    for bit in range(DUMP_ROWS.bit_length()):
        if DUMP_ROWS & (1 << bit):
            make_copy(0, 0, 1 << bit).wait()

    def more(k, c):
        make_copy(0, 0, SEG_BIG).wait()
        return c

    lax.fori_loop(0, meta_ref[0, 0, META_EXTRA], more, 0)


TAIL_PIECES = tuple(SUBLANES << b for b in range((MOE_BLK // SUBLANES).bit_length() - 1))


def _tail_copies(tail_ref, make_copy, action):
    def per_expert(e, c):
        start = tail_ref[0, 0, e]
        n8 = tail_ref[0, 0, N_EXPERTS + e]
        for rows in TAIL_PIECES:
            bit = rows // SUBLANES

            @pl.when((n8 & bit) != 0)
            def _():
                off = (n8 & ~(2 * bit - 1)) * SUBLANES
                action(make_copy(0, pl.multiple_of(start + off, SUBLANES), rows))
        return c

    lax.fori_loop(0, N_EXPERTS, per_expert, 0)

    def per_block(b, c):
        action(make_copy(0, pl.multiple_of(b * MOE_BLK, MOE_BLK), MOE_BLK))
        return c

    lax.fori_loop(tail_ref[0, 0, 2 * N_EXPERTS], tail_ref[0, 0, 2 * N_EXPERTS + 1], per_block, 0)


def _dispatch_body(meta_ref, prev_meta_ref, tail_ref, loff_ref, rt_ref, h_ref, xs_ref, buf_ref, sem):
    i = pl.program_id(0)
    last = pl.num_programs(0) - 1
    slot = i % 2
    tm = rt_ref.shape[0]
    rows = buf_ref.shape[1]
    rt = rt_ref[...]
    dot = functools.partial(jnp.dot, preferred_element_type=F32)
    lane = lax.broadcasted_iota(I32, (tm, LANES), 1)
    l1, l2 = _local_pos(rt, loff_ref[0])
    parts = []
    for pos in (l1, l2):
        hi = jnp.floor(pos * (1.0 / 32.0))
        parts += [hi, pos - 32.0 * hi]
    cols = jnp.zeros((tm, LANES), F32)
    for k, part in enumerate(parts):
        cols = jnp.where(lane == k, part, cols)
    eye = _one_hot(lax.broadcasted_iota(I32, (SUBLANES, LANES), 0) == lax.broadcasted_iota(I32, (SUBLANES, LANES), 1))
    lanes = _nt(eye, cols.astype(BF16))
    r1 = (32.0 * lanes[0:1] + lanes[1:2]).astype(I32)
    r2 = (32.0 * lanes[2:3] + lanes[3:4]).astype(I32)
    row = lax.broadcasted_iota(I32, (rows, tm), 0)
    perm = _one_hot((row == r1) | (row == r2))
    side = jnp.zeros((tm, LANES), F32)
    for w_lane, e_lane, base in ((RT_W1, RT_E1, SLOT_W1), (RT_W2, RT_E2, SLOT_W2)):
        w = rt[:, w_lane:w_lane + 1]
        hi = w.astype(BF16).astype(F32)
        side = jnp.where(lane == base, hi, jnp.where(lane == base + 1, w - hi, side))
        side = jnp.where(lane == SLOT_E1 + (base - SLOT_W1) // 2, rt[:, e_lane:e_lane + 1], side)
    xl = dot(perm, jnp.concatenate([h_ref[...], side.astype(BF16)], axis=1))
    half = D_MODEL // 2
    buf_ref[slot, :, :half] = _pack_pair(xl[:, :half], xl[:, half:D_MODEL])
    buf_ref[slot, :, half:] = lax.bitcast_convert_type(xl[:, D_MODEL:], U32)

    def copier(k):
        def make_copy(src_row, dst_row, n):
            return pltpu.make_async_copy(buf_ref.at[k, pl.ds(src_row, n)], xs_ref.at[pl.ds(dst_row, n)],
                                         sem.at[k])
        return make_copy

    dump = xs_ref.shape[0] - DUMP_BLOCKS * MOE_BLK + slot * DUMP_ROWS
    _segment_starts(meta_ref, copier(slot), lambda k: (0, dump + k))

    @pl.when(i > 0)
    def _():
        _segment_wait(prev_meta_ref, copier(1 - slot))

    @pl.when(i == last)
    def _():
        _segment_wait(meta_ref, copier(slot))
        buf_ref[slot, :MOE_BLK] = jnp.zeros((MOE_BLK, XS_W), U32)
        _tail_copies(tail_ref, copier(slot), lambda cp: cp.start())
        _tail_copies(tail_ref, copier(slot), lambda cp: cp.wait())


def _dispatch(meta, tail, loff, rt, h2, n_pad):
    T, D = h2.shape
    nt = meta.shape[0]
    tm = T // nt
    return pl.pallas_call(
        _dispatch_body,
        grid=(nt,),
        in_specs=[pl.BlockSpec((1, 1, meta.shape[2]), lambda i: (i, 0, 0), memory_space=pltpu.SMEM),
                  pl.BlockSpec((1, 1, meta.shape[2]), lambda i: (jnp.maximum(i - 1, 0), 0, 0),
                               memory_space=pltpu.SMEM),
                  pl.BlockSpec((1, 1, tail.shape[2]), lambda i: (0, 0, 0), memory_space=pltpu.SMEM),
                  pl.BlockSpec((1, 1, LANES), lambda i: (i, 0, 0)),
                  pl.BlockSpec((tm, LANES), lambda i: (i, 0)),
                  pl.BlockSpec((tm, D), lambda i: (i, 0))],
        out_specs=pl.BlockSpec(memory_space=pl.ANY),
        out_shape=jax.ShapeDtypeStruct((n_pad + DUMP_BLOCKS * MOE_BLK, XS_W), U32),
        scratch_shapes=[pltpu.VMEM((2, _local_rows(tm), XS_W), U32), pltpu.SemaphoreType.DMA((2,))],
        compiler_params=pltpu.CompilerParams(
            dimension_semantics=("arbitrary",), vmem_limit_bytes=VMEM_LIMIT),
        name="moe_dispatch",
    )(meta, meta, tail, loff, rt, h2)


EXPERT_PARTS = 2


def _expert_body(be_ref, nused_ref, xs_ref, wg_ref, wu_ref, wd_ref, ys_ref, wgb_ref, wub_ref, wdb_ref):
    i = pl.program_id(0)

    @pl.when((i == 0) | (be_ref[i] != be_ref[jnp.maximum(i - 1, 0)]))
    def _():
        wgb_ref[...] = wg_ref[0].astype(BF16)
        wub_ref[...] = wu_ref[0].astype(BF16)
        wdb_ref[...] = wd_ref[0].astype(BF16)

    live = i < nused_ref[0]

    @pl.when(jnp.logical_not(live))
    def _():
        ys_ref[...] = jnp.zeros_like(ys_ref)

    @pl.when(live)
    def _():
        dot = functools.partial(jnp.dot, preferred_element_type=F32)
        part = MOE_BLK // EXPERT_PARTS
        half = D_MODEL // 2
        parts = [slice(k * part, (k + 1) * part) for k in range(EXPERT_PARTS)]
        hs = []
        for rows in parts:
            ha, hb = _unpack_pair(xs_ref[rows, :half])
            hs.append(jnp.concatenate([ha, hb], axis=1).astype(BF16))
        gs = [dot(h, wgb_ref[...]) for h in hs]
        us = [dot(h, wub_ref[...]) for h in hs]
        acts = [(g * _sigmoid(g) * u).astype(BF16) for g, u in zip(gs, us)]
        expert = (be_ref[i] % N_EXPERTS).astype(F32)
        for rows, act in zip(parts, acts):
            rec = lax.bitcast_convert_type(xs_ref[rows, half:], F32)
            w = jnp.where(rec[:, SLOT_E1:SLOT_E1 + 1] == expert,
                          rec[:, SLOT_W1:SLOT_W1 + 1] + rec[:, SLOT_W1 + 1:SLOT_W1 + 2],
                          rec[:, SLOT_W2:SLOT_W2 + 1] + rec[:, SLOT_W2 + 1:SLOT_W2 + 2])
            y = dot(act, wdb_ref[...]) * w
            ys_ref[rows] = _pack_pair(y[:, :half], y[:, half:])


def _experts(block_expert, nused, xs, wg, wu, wd):
    n_pad = xs.shape[0] - DUMP_BLOCKS * MOE_BLK
    blk_in = pl.BlockSpec((MOE_BLK, XS_W), lambda i, be, nu: (jnp.minimum(i, nu[0] - 1), 0))
    blk_out = pl.BlockSpec((MOE_BLK, D_MODEL // 2), lambda i, be, nu: (i, 0))
    wspec = lambda a: pl.BlockSpec((1,) + a.shape[1:], lambda i, be, nu: (be[i], 0, 0))
    return pl.pallas_call(
        _expert_body,
        grid_spec=pltpu.PrefetchScalarGridSpec(
            num_scalar_prefetch=2,
            grid=(n_pad // MOE_BLK,),
            in_specs=[blk_in, wspec(wg), wspec(wu), wspec(wd)],
            out_specs=blk_out,
            scratch_shapes=[pltpu.VMEM(wg.shape[1:], BF16), pltpu.VMEM(wu.shape[1:], BF16),
                            pltpu.VMEM(wd.shape[1:], BF16)],
        ),
        out_shape=jax.ShapeDtypeStruct((n_pad, D_MODEL // 2), U32),
        compiler_params=pltpu.CompilerParams(
            dimension_semantics=("arbitrary",), vmem_limit_bytes=VMEM_LIMIT),
        name="moe_experts",
    )(block_expert, nused, xs, wg, wu, wd)


def _combine_body(meta_ref, next_meta_ref, loff_ref, rt_ref, x_ref, fg_ref, ys_ref, o_ref, buf_ref, sem,
                  *, final_norm):
    i = pl.program_id(0)
    slot = i % 2

    def copier(k):
        def make_copy(loc_row, glob_row, n):
            return pltpu.make_async_copy(ys_ref.at[pl.ds(glob_row, n)], buf_ref.at[k, pl.ds(loc_row, n)],
                                         sem.at[k])
        return make_copy

    rows = buf_ref.shape[1] - DUMP_ROWS
    dummy = lambda k: (rows + k, 0)

    @pl.when(i == 0)
    def _():
        buf_ref[...] = jnp.zeros_like(buf_ref)
        _segment_starts(meta_ref, copier(0), dummy)

    _segment_starts(next_meta_ref, copier(1 - slot), dummy)

    l1, l2 = _local_pos(rt_ref[...], loff_ref[0])
    col = lax.broadcasted_iota(I32, (rt_ref.shape[0], rows), 1)
    pt = _one_hot((col == l1.astype(I32)) | (col == l2.astype(I32)))
    _segment_wait(meta_ref, copier(slot))

    @pl.when(i == pl.num_programs(0) - 1)
    def _():
        _segment_wait(next_meta_ref, copier(1 - slot))

    yl = jnp.concatenate(_unpack_pair(buf_ref[slot, :rows]), axis=1).astype(BF16)
    out = x_ref[...] + jnp.dot(pt, yl, preferred_element_type=F32)
    if final_norm:
        out = _rms(out) * fg_ref[...]
    o_ref[...] = out


def _combine(meta, loff, rt, x, fg, ys, final_norm):
    T, D = x.shape
    nt = meta.shape[0]
    tm = T // nt
    return pl.pallas_call(
        functools.partial(_combine_body, final_norm=final_norm),
        grid=(nt,),
        in_specs=[pl.BlockSpec((1, 1, meta.shape[2]), lambda i: (i, 0, 0), memory_space=pltpu.SMEM),
                  pl.BlockSpec((1, 1, meta.shape[2]), lambda i: (jnp.minimum(i + 1, nt - 1), 0, 0),
                               memory_space=pltpu.SMEM),
                  pl.BlockSpec((1, 1, LANES), lambda i: (i, 0, 0)),
                  pl.BlockSpec((tm, LANES), lambda i: (i, 0)),
                  pl.BlockSpec((tm, D), lambda i: (i, 0)),
                  pl.BlockSpec((1, D), lambda i: (0, 0)),
                  pl.BlockSpec(memory_space=pl.ANY)],
        out_specs=pl.BlockSpec((tm, D), lambda i: (i, 0)),
        out_shape=jax.ShapeDtypeStruct((T, D), F32),
        scratch_shapes=[pltpu.VMEM((2, _local_rows(tm) + DUMP_ROWS, D // 2), U32),
                        pltpu.SemaphoreType.DMA((2,))],
        compiler_params=pltpu.CompilerParams(
            dimension_semantics=("arbitrary",), vmem_limit_bytes=VMEM_LIMIT),
        name="moe_combine",
    )(meta, meta, loff, rt, x, fg, ys)


def _rope_tables(L):
    half = HEAD_DIM // 2
    inv = ROPE_THETA ** (-jnp.arange(half, dtype=F32) / half)
    ang = jnp.arange(L, dtype=F32)[:, None] * inv[None, :]
    lane = np.arange(LANES)
    idx = lane % half
    first = jnp.asarray((lane % HEAD_DIM) < half)
    cos = jnp.cos(ang)[:, idx]
    sin = jnp.sin(ang)[:, idx]
    return cos, jnp.where(first, -sin, 0.0), jnp.where(first, 0.0, sin)


def _moe_layout(cnt, T, layer):
    nt = cnt.shape[0] // SUBLANES
    c = cnt.reshape(nt, SUBLANES, LANES)[:, 0, :N_EXPERTS].astype(I32)
    c8 = (c + SUBLANES - 1) // SUBLANES * SUBLANES
    tile_off = jnp.cumsum(c8, axis=0) - c8
    total = jnp.sum(c8, axis=0)
    padded = (total + MOE_BLK - 1) // MOE_BLK * MOE_BLK
    pend = jnp.cumsum(padded)
    dst = (pend - padded)[None, :] + tile_off
    loff = jnp.cumsum(c8, axis=1) - c8
    n8 = c8 // SUBLANES
    per_big = SEG_BIG // SUBLANES
    n_big = n8 // per_big
    extra = jnp.broadcast_to(jnp.sum(jnp.maximum(n_big - 1, 0), axis=1, keepdims=True), n8.shape)
    meta = jnp.concatenate([dst, loff, n_big, n8 % per_big, extra], axis=1).reshape(nt, 1, META_W)
    loff_f = jnp.pad(loff.astype(F32), ((0, 0), (0, LANES - N_EXPERTS))).reshape(nt, 1, LANES)
    n_pad = 2 * T + nt * N_EXPERTS * SUBLANES + N_EXPERTS * MOE_BLK
    n_pad = (n_pad + MOE_BLK - 1) // MOE_BLK * MOE_BLK
    starts = jnp.arange(n_pad // MOE_BLK, dtype=I32) * MOE_BLK
    block_expert = jnp.minimum(jnp.sum(starts[:, None] >= pend[None, :], axis=1), N_EXPERTS - 1)
    block_expert = block_expert.astype(I32) + layer * N_EXPERTS
    nused = (pend[-1:] // MOE_BLK).astype(I32)
    tail = jnp.concatenate([pend - padded + total, (padded - total) // SUBLANES, nused,
                            jnp.full((1,), n_pad // MOE_BLK + DUMP_BLOCKS, I32)]).reshape(1, 1, 2 * N_EXPERTS + 2)
    return meta, tail, loff_f, block_expert, nused, n_pad


def kernel(x, attn_norm_g, w_in, sink_logits, hgrn_lb_table, hgrn_norm_g, w_up_attn, w_up_hgrn, w_out,
           ffn_norm_g, w_router_group, b_router_group, w_router_expert, b_router_expert,
           w_expert_gate, w_expert_up, w_expert_down, final_norm_g):
    B, L, D = x.shape
    T = B * L
    depth = w_in.shape[0]
    sm = jax.nn.softmax(hgrn_lb_table.astype(F32), axis=0)
    lower_bounds = jnp.cumsum(sm, axis=0) - sm[:1]
    cos, s1, s2 = _rope_tables(L)
    tm_post = min(TM_POST, T)
    ltri = jnp.tril(jnp.ones((tm_post // POST_PARTS,) * 2, BF16), -1)
    fin_g = final_norm_g.reshape(1, D).astype(F32)
    w_in_bf16 = w_in.astype(BF16)
    all_layers = lambda w: w.reshape((depth * N_EXPERTS,) + w.shape[2:])
    wg_all, wu_all, wd_all = all_layers(w_expert_gate), all_layers(w_expert_up), all_layers(w_expert_down)

    for l in range(depth):
        qa, ka, va, qb, sf, sb, ib, og, ga, gb = _inproj(
            x, attn_norm_g[l].reshape(1, D), w_in_bf16, l, cos, s1, s2)
        ya = _attention(qa, ka, va, sink_logits[l].astype(F32) * LOG2E)
        of, orv = _hgrn(qb, sf, sb, ib, lower_bounds[l])

        rw = jnp.concatenate(
            [w_router_group[l], jnp.transpose(w_router_expert[l], (1, 0, 2)).reshape(D, N_EXPERTS)], axis=1)
        rw = jnp.pad(rw.astype(F32), ((0, 0), (0, LANES - rw.shape[1])))
        rwh = rw.astype(BF16)
        rw2 = jnp.concatenate([rwh, (rw - rwh.astype(F32)).astype(BF16)], axis=1)
        rb = jnp.concatenate([b_router_group[l], b_router_expert[l].reshape(-1)]).astype(F32)
        rb = jnp.pad(rb, (0, LANES - rb.shape[0])).reshape(1, LANES)
        flat = lambda a: a.reshape(T, a.shape[-1])
        xn, h2, rt, cnt = _post(
            flat(x), flat(ya), flat(of), flat(orv), flat(og), flat(ga), flat(gb),
            w_up_attn[l].astype(BF16), w_up_hgrn[l].astype(BF16), w_out[l].astype(BF16),
            hgrn_norm_g[l].reshape(1, HGRN_W).astype(F32), ffn_norm_g[l].reshape(1, D).astype(F32),
            rw2, rb, ltri)

        meta, tail, loff, block_expert, nused, n_pad = _moe_layout(cnt, T, l)
        xs = _dispatch(meta, tail, loff, rt, h2, n_pad)
        ys = _experts(block_expert, nused, xs, wg_all, wu_all, wd_all)
        x = _combine(meta, loff, rt, xn, fin_g, ys, l == depth - 1).reshape(B, L, D)
    return x
```

```python
import functools

import numpy as np
import jax
import jax.numpy as jnp
from jax import lax
from jax.experimental import pallas as pl
from jax.experimental.pallas import tpu as pltpu

F32 = jnp.float32
BF16 = jnp.bfloat16
U32 = jnp.uint32
I32 = jnp.int32

D_MODEL = 1024
N_Q_HEADS = 8
N_KV_HEADS = 2
HEAD_DIM = 64
WINDOW = 128
ROPE_THETA = 10000.0
MASK_VALUE = -1e30
HGRN_HEADS = 4
HGRN_DIM = 128
LOGF_MIN = -4.0
F_MIN = float(np.exp(LOGF_MIN))
LOG2E = float(np.log2(np.e))
Q_SCALE = HEAD_DIM ** -0.5 * LOG2E
LB_FLOOR = 1e-30
N_GROUPS = 4
EXPERTS_PER_GROUP = 8
N_EXPERTS = N_GROUPS * EXPERTS_PER_GROUP
D_EXPERT = 512
NORM_EPS = 1e-6

ATTN_Q_W = N_Q_HEADS * HEAD_DIM
ATTN_KV_W = N_KV_HEADS * HEAD_DIM
HGRN_W = HGRN_HEADS * HGRN_DIM
OFF_QA = 0
OFF_KA = OFF_QA + ATTN_Q_W
OFF_VA = OFF_KA + ATTN_KV_W
OFF_QB = OFF_VA + ATTN_KV_W
OFF_ZF = OFF_QB + HGRN_W
OFF_ZB = OFF_ZF + HGRN_W
OFF_IB = OFF_ZB + HGRN_W
OFF_OG = OFF_IB + HGRN_W
OFF_GA = OFF_OG + HGRN_W
OFF_GB = OFF_GA + D_MODEL

LANES = 128
VMEM_LIMIT = 56 * 1024 * 1024

TM_PROJ = 1024
TQ_ATTN = 1024
TH_HGRN = 1024
TM_POST = 512
MOE_BLK = 512


def _sigmoid(z):
    return 1.0 / (1.0 + jnp.exp(-z))


def _rms(x):
    return x * lax.rsqrt(jnp.mean(x * x, axis=-1, keepdims=True) + NORM_EPS)


def _nt(a, b):
    return lax.dot_general(a, b, (((1,), (1,)), ((), ())), preferred_element_type=F32)


def _pack_pair(a, b):
    ua = lax.bitcast_convert_type(a.astype(BF16).astype(F32), U32)
    ub = lax.bitcast_convert_type(b.astype(BF16).astype(F32), U32)
    return ua | (ub >> 16)


def _unpack_pair(u):
    hi = lax.bitcast_convert_type(u & jnp.uint32(0xFFFF0000), F32)
    lo = lax.bitcast_convert_type(u << 16, F32)
    return hi, lo


def _inproj_body(x_ref, g_ref, w_ref, cos_ref, s1_ref, s2_ref,
                 qa_ref, ka_ref, va_ref, qb_ref, sf_ref, sb_ref, ib_ref, og_ref, ga_ref, gb_ref):
    h = (_rms(x_ref[0]) * g_ref[...]).astype(BF16)

    def proj(off, width):
        return jnp.dot(h, w_ref[:, off:off + width], preferred_element_type=F32)

    cos, s1, s2 = cos_ref[...], s1_ref[...], s2_ref[...]

    def rope(t):
        return t * cos + pltpu.roll(t, 96, 1) * s1 + pltpu.roll(t, 32, 1) * s2

    ga_ref[0] = _sigmoid(proj(OFF_GA, D_MODEL)).astype(BF16)
    gb_ref[0] = _sigmoid(proj(OFF_GB, D_MODEL)).astype(BF16)
    t = proj(OFF_OG, HGRN_W)
    og_ref[0] = (t * _sigmoid(t)).astype(BF16)
    t = proj(OFF_QB, HGRN_W)
    qb_ref[0] = (t * _sigmoid(t)).astype(BF16)
    sf_ref[0] = _sigmoid(proj(OFF_ZF, HGRN_W)).astype(BF16)
    sb_ref[0] = _sigmoid(proj(OFF_ZB, HGRN_W)).astype(BF16)
    q = proj(OFF_QA, ATTN_Q_W)
    for c in range(ATTN_Q_W // LANES):
        sl = slice(c * LANES, (c + 1) * LANES)
        qa_ref[0, :, sl] = (rope(q[:, sl]) * Q_SCALE).astype(BF16)
    k = rope(proj(OFF_KA, ATTN_KV_W))
    ka_ref[0, :, :LANES] = k.astype(BF16)
    ka_ref[0, :, LANES:] = pltpu.roll(k, 64, 1).astype(BF16)
    v = proj(OFF_VA, ATTN_KV_W)
    va_ref[0, :, :LANES] = v.astype(BF16)
    va_ref[0, :, LANES:] = pltpu.roll(v, 64, 1).astype(BF16)
    ib_ref[0] = proj(OFF_IB, HGRN_W).astype(BF16)


def _inproj(x, g, w_all, layer, cos, s1, s2):
    B, L, D = x.shape
    tm = min(TM_PROJ, L)
    widths = (ATTN_Q_W, 2 * ATTN_KV_W, 2 * ATTN_KV_W, HGRN_W, HGRN_W, HGRN_W, HGRN_W, HGRN_W,
              D_MODEL, D_MODEL)
    tab = pl.BlockSpec((tm, LANES), lambda b, i: (i, 0))
    return pl.pallas_call(
        _inproj_body,
        grid=(B, L // tm),
        in_specs=[
            pl.BlockSpec((1, tm, D), lambda b, i: (b, i, 0)),
            pl.BlockSpec((1, D), lambda b, i: (0, 0)),
            pl.BlockSpec((None,) + w_all.shape[1:], lambda b, i: (layer, 0, 0), pipeline_mode=pl.Buffered(1)),
            tab, tab, tab,
        ],
        out_specs=[pl.BlockSpec((1, tm, w), lambda b, i: (b, i, 0)) for w in widths],
        out_shape=[jax.ShapeDtypeStruct((B, L, w), BF16) for w in widths],
        compiler_params=pltpu.CompilerParams(
            dimension_semantics=("arbitrary", "arbitrary"), vmem_limit_bytes=VMEM_LIMIT),
        name="inproj",
    )(x, g, w_all, cos, s1, s2)


def _attn_body(sink_ref, q_ref, kp_ref, km_ref, kn_ref, vp_ref, vm_ref, vn_ref, o_ref, *, tq, seq):
    i = pl.program_id(1)
    kw = jnp.concatenate([kp_ref[0], km_ref[0], kn_ref[0]], axis=0)
    vw = jnp.concatenate([vp_ref[0], vm_ref[0], vn_ref[0]], axis=0)
    lo = lax.broadcasted_iota(I32, (1, LANES), 1) < HEAD_DIM
    zero = jnp.zeros((), BF16)

    def variants(t):
        a, b = t[:, :LANES], t[:, LANES:]
        return ((jnp.where(lo, a, zero), jnp.where(lo, zero, b)),
                (jnp.where(lo, b, zero), jnp.where(lo, zero, a)))

    kvar, vvar = variants(kw), variants(vw)
    row = lax.broadcasted_iota(I32, (WINDOW, 3 * WINDOW), 0)
    col = lax.broadcasted_iota(I32, (WINDOW, 3 * WINDOW), 1)
    band = (col >= row) & (col <= row + 2 * WINDOW)
    ncol = ATTN_Q_W // LANES
    group = lambda c: (2 * c) // (N_Q_HEADS // N_KV_HEADS)
    heads = [(c, hh) for c in range(ncol) for hh in range(2)]
    for j in range(tq // WINDOW):
        kpos = col + (i * tq + (j - 1) * WINDOW)
        bias = jnp.where(band & (kpos >= 0) & (kpos < seq), 0.0, MASK_VALUE)
        rows = slice(j * WINDOW, (j + 1) * WINDOW)
        win = slice(j * WINDOW, (j + 3) * WINDOW)
        scores = [_nt(q_ref[0, rows, c * LANES:(c + 1) * LANES], kvar[group(c)][hh][win]) for c, hh in heads]
        probs, denoms = [], []
        for (c, hh), s in zip(heads, scores):
            s = jnp.concatenate([s[:, :WINDOW] + bias[:, :WINDOW], s[:, WINDOW:2 * WINDOW],
                                 s[:, 2 * WINDOW:] + bias[:, 2 * WINDOW:]], axis=1)
            sk = sink_ref[2 * c + hh]
            m = jnp.maximum(jnp.max(s, axis=1, keepdims=True), sk)
            p = jnp.exp2(s - m)
            denoms.append(jnp.sum(p, axis=1, keepdims=True) + jnp.exp2(sk - m))
            probs.append(p.astype(BF16))
        for c in range(ncol):
            g = group(c)
            pv = jnp.dot(jnp.concatenate(probs[2 * c:2 * c + 2], axis=1),
                         jnp.concatenate([vvar[g][0][win], vvar[g][1][win]], axis=0),
                         preferred_element_type=F32)
            inv = jnp.where(lo, 1.0 / denoms[2 * c], 1.0 / denoms[2 * c + 1])
            o_ref[0, rows, c * LANES:(c + 1) * LANES] = (pv * inv).astype(BF16)


def _attention(qa, ka, va, sink):
    B, L, _ = qa.shape
    tq = min(TQ_ATTN, L)
    r = tq // WINDOW
    nb = L // WINDOW
    kvw = 2 * ATTN_KV_W
    main = pl.BlockSpec((1, tq, kvw), lambda b, i: (b, i, 0))
    prev = pl.BlockSpec((1, WINDOW, kvw), lambda b, i: (b, jnp.maximum(i * r - 1, 0), 0))
    nxt = pl.BlockSpec((1, WINDOW, kvw), lambda b, i: (b, jnp.minimum(i * r + r, nb - 1), 0))
    return pl.pallas_call(
        functools.partial(_attn_body, tq=tq, seq=L),
        grid=(B, L // tq),
        in_specs=[
            pl.BlockSpec(memory_space=pltpu.SMEM),
            pl.BlockSpec((1, tq, ATTN_Q_W), lambda b, i: (b, i, 0)),
            prev, main, nxt, prev, main, nxt,
        ],
        out_specs=pl.BlockSpec((1, tq, ATTN_Q_W), lambda b, i: (b, i, 0)),
        out_shape=jax.ShapeDtypeStruct((B, L, ATTN_Q_W), BF16),
        compiler_params=pltpu.CompilerParams(
            dimension_semantics=("arbitrary", "arbitrary"), vmem_limit_bytes=VMEM_LIMIT),
        name="win_attn",
    )(sink, qa, ka, ka, ka, va, va, va)


HB = 128
LEAF = 32
HGRN_LEVELS = tuple(LEAF << k for k in range((HB // LEAF).bit_length() - 1))
LEAF_SHIFT = 0.5 * LEAF * (-LOGF_MIN) * LOG2E


def _hgrn_masks(reverse):
    r = lax.broadcasted_iota(I32, (HB, HB), 0)
    c = lax.broadcasted_iota(I32, (HB, HB), 1)
    if reverse:
        r, c = c, r
    tri = jnp.where(c <= r, 1.0, 0.0).astype(BF16)
    same = lambda w: (r >> (w.bit_length() - 1)) == (c >> (w.bit_length() - 1))
    cross = lambda w: ((r & w) != 0) & ((c & w) == 0)
    lev = jnp.full((HB, HB), len(HGRN_LEVELS), I32)
    for k, w in reversed(list(enumerate(HGRN_LEVELS[:-1]))):
        lev = jnp.where(same(2 * w) & cross(w), k + 1, lev)
    lev = jnp.where(same(LEAF) & (c <= r), 0, lev)
    return tri, lev


def _hgrn_gates(s, lb, lbc, tri):
    f = lbc + (1.0 - lb) * s
    logf = jnp.maximum(jnp.log(f), LOGF_MIN)
    kk = 1.0 - jnp.maximum(f, F_MIN)
    hi = logf.astype(BF16)
    mid = (logf - hi.astype(F32)).astype(BF16)
    dot = functools.partial(jnp.dot, preferred_element_type=F32)
    return kk, (dot(tri, hi) + dot(tri, mid)) * LOG2E


def _hgrn_scores(q, kk, cum, v, st_ref, reverse):
    def ref_row(idx):
        return cum[idx:idx + 1]

    total = ref_row(0) if reverse else ref_row(HB - 1)
    q_blk = q * jnp.exp2(cum)
    k_blk = kk * jnp.exp2(total - cum)

    ql, kl = [], []
    for c in range(HB // LEAF):
        rows = slice(c * LEAF, (c + 1) * LEAF)
        edge = (c + 1) * LEAF if reverse else c * LEAF - 1
        loc = (cum[rows] - ref_row(edge) if 0 <= edge < HB else cum[rows]) + LEAF_SHIFT
        ql.append(q[rows] * jnp.exp2(loc))
        kl.append(kk[rows] * jnp.exp2(-loc))
    pairs = [(jnp.concatenate(ql, axis=0), jnp.concatenate(kl, axis=0))]
    for w in HGRN_LEVELS:
        qp, kp = [], []
        zeros = jnp.zeros((w, HB), F32)
        for s0 in range(0, HB, 2 * w):
            first, second = slice(s0, s0 + w), slice(s0 + w, s0 + 2 * w)
            if reverse:
                ref = ref_row(s0 + w)
                qp += [q[first] * jnp.exp2(cum[first] - ref), zeros]
                kp += [zeros, kk[second] * jnp.exp2(ref - cum[second])]
            else:
                ref = ref_row(s0 + w - 1)
                qp += [zeros, q[second] * jnp.exp2(cum[second] - ref)]
                kp += [kk[first] * jnp.exp2(ref - cum[first]), zeros]
        pairs.append((jnp.concatenate(qp, axis=0), jnp.concatenate(kp, axis=0)))
    scores = [_nt(qq.astype(BF16), kq.astype(BF16)) for qq, kq in pairs]
    st = st_ref[...]
    o_state = _nt(q_blk.astype(BF16), st.astype(BF16))
    st_ref[...] = st * jnp.exp2(total) + lax.dot_general(
        v, k_blk.astype(BF16), (((0,), (0,)), ((), ())), preferred_element_type=F32)
    return scores, o_state


def _hgrn_output(scores, o_state, v, lev):
    a = scores[-1]
    for k in reversed(range(len(scores) - 1)):
        a = jnp.where(lev == k, scores[k], a)
    return jnp.dot(a.astype(BF16), v, preferred_element_type=F32) + o_state


def _hgrn_body(lb_ref, qf_ref, sf_ref, vf_ref, qr_ref, sr_ref, vr_ref, of_ref, or_ref, st_ref, *, th):
    @pl.when(pl.program_id(1) == 0)
    def _():
        st_ref[...] = jnp.zeros_like(st_ref)

    lb = lb_ref[...]
    lbc = jnp.maximum(lb, LB_FLOOR)
    masks = (_hgrn_masks(False), _hgrn_masks(True))
    nblk = th // HB
    streams = ((qf_ref, sf_ref, vf_ref, of_ref), (qr_ref, sr_ref, vr_ref, or_ref))

    def body(t, carry):
        rows = [pl.multiple_of((nblk - 1 - t if d else t) * HB, HB) for d in range(2)]
        units = [(d, h * HGRN_DIM) for d in range(2) for h in range(HGRN_HEADS)]

        def load(d, k, lane0):
            return streams[d][k][0, pl.ds(rows[d], HB), lane0:lane0 + HGRN_DIM]

        gates = [_hgrn_gates(load(d, 1, c0).astype(F32), lb[d:d + 1, c0:c0 + HGRN_DIM],
                             lbc[d:d + 1, c0:c0 + HGRN_DIM], masks[d][0]) for d, c0 in units]
        mids = [_hgrn_scores(load(d, 0, c0).astype(F32), kk, cum, load(d, 2, c0),
                             st_ref.at[d, c0 // HGRN_DIM], bool(d))
                for (d, c0), (kk, cum) in zip(units, gates)]
        for (d, c0), (scores, o_state) in zip(units, mids):
            o = _hgrn_output(scores, o_state, load(d, 2, c0), masks[d][1])
            streams[d][3][0, pl.ds(rows[d], HB), c0:c0 + HGRN_DIM] = o.astype(BF16)
        return carry

    lax.fori_loop(0, nblk, body, 0)


def _hgrn(qb, sf, sb, ib, lb):
    B, L, _ = qb.shape
    th = min(TH_HGRN, L)
    n = L // th
    fwd = pl.BlockSpec((1, th, HGRN_W), lambda b, i: (b, i, 0))
    rev = pl.BlockSpec((1, th, HGRN_W), lambda b, i: (b, n - 1 - i, 0))
    return pl.pallas_call(
        functools.partial(_hgrn_body, th=th),
        grid=(B, n),
        in_specs=[pl.BlockSpec((2, HGRN_W), lambda b, i: (0, 0)), fwd, fwd, fwd, rev, rev, rev],
        out_specs=[fwd, rev],
        out_shape=[jax.ShapeDtypeStruct((B, L, HGRN_W), BF16)] * 2,
        scratch_shapes=[pltpu.VMEM((2, HGRN_HEADS, HGRN_DIM, HGRN_DIM), F32)],
        compiler_params=pltpu.CompilerParams(
            dimension_semantics=("arbitrary", "arbitrary"), vmem_limit_bytes=VMEM_LIMIT),
        name="hgrn2",
    )(lb, qb, sf, ib, qb, sb, ib)


POST_PARTS = 2


def _post_mix(rows, x_ref, ya_ref, of_ref, or_ref, og_ref, ga_ref, gb_ref, wua_ref, wuh_ref, wo_ref,
              hg_ref, fg_ref, rw_ref, rb_ref, xn_ref, h2_ref):
    dot = functools.partial(jnp.dot, preferred_element_type=F32)
    o = of_ref[rows].astype(F32) + or_ref[rows].astype(F32)
    yh = (_rms(o) * hg_ref[...] * og_ref[rows].astype(F32)).astype(BF16)
    merged = (ga_ref[rows].astype(F32) * dot(ya_ref[rows], wua_ref[...])
              + gb_ref[rows].astype(F32) * dot(yh, wuh_ref[...]))
    xn = x_ref[rows] + dot(merged.astype(BF16), wo_ref[...])
    xn_ref[rows] = xn
    h2 = _rms(xn) * fg_ref[...]
    hi = h2.astype(BF16)
    h2_ref[rows] = hi
    lo = (h2 - hi.astype(F32)).astype(BF16)
    a = dot(hi, rw_ref[...])
    return a[:, :LANES] + a[:, LANES:] + dot(lo, rw_ref[:, :LANES]) + rb_ref[...]


def _post_route(logits):
    lane = lax.broadcasted_iota(I32, logits.shape, 1)
    lanef = lane.astype(F32)
    big = jnp.float32(1e9)
    ninf = jnp.float32(-jnp.inf)
    red = dict(axis=1, keepdims=True)
    gmask = lane < N_GROUPS
    mg = jnp.max(jnp.where(gmask, logits, ninf), **red)
    p_top = 1.0 / jnp.sum(jnp.where(gmask, jnp.exp(logits - mg), 0.0), **red)
    gi = jnp.min(jnp.where(gmask & (logits == mg), lanef, big), **red)
    emask = ((lane >= N_GROUPS) & (lane < N_GROUPS + N_EXPERTS)
             & (((lane - N_GROUPS) >> 3).astype(F32) == gi))
    le = jnp.where(emask, logits, ninf)
    m1 = jnp.max(le, **red)
    i1 = jnp.min(jnp.where(le == m1, lanef, big), **red)
    le2 = jnp.where(lanef == i1, ninf, le)
    m2 = jnp.max(le2, **red)
    i2 = jnp.min(jnp.where(le2 == m2, lanef, big), **red)
    r = jnp.exp(m2 - m1)
    w1 = 1.0 / (1.0 + r)
    w2 = r * w1
    e1 = i1 - N_GROUPS
    e2 = i2 - N_GROUPS
    is1, is2 = lanef == e1, lanef == e2
    return (e1, e2, p_top * w1, p_top * w2), is1, is2, jnp.where(is1 | is2, 1.0, 0.0)


def _post_body(x_ref, ya_ref, of_ref, or_ref, og_ref, ga_ref, gb_ref, wua_ref, wuh_ref, wo_ref,
               hg_ref, fg_ref, rw_ref, rb_ref, ltri_ref,
               xn_ref, h2_ref, rt_ref, cnt_ref):
    tm = x_ref.shape[0]
    part = tm // POST_PARTS
    parts = [slice(k * part, (k + 1) * part) for k in range(POST_PARTS)]
    logits = [_post_mix(rows, x_ref, ya_ref, of_ref, or_ref, og_ref, ga_ref, gb_ref, wua_ref, wuh_ref,
                        wo_ref, hg_ref, fg_ref, rw_ref, rb_ref, xn_ref, h2_ref) for rows in parts]
    routed = [_post_route(lg) for lg in logits]
    before = jnp.zeros((1, LANES), F32)
    red = dict(axis=1, keepdims=True)
    lane = lax.broadcasted_iota(I32, (part, LANES), 1)
    for rows, ((e1, e2, w1, w2), is1, is2, onehot) in zip(parts, routed):
        pref = jnp.dot(ltri_ref[...], onehot.astype(BF16), preferred_element_type=F32) + before
        before = before + jnp.sum(onehot, axis=0, keepdims=True)
        rank1 = jnp.sum(jnp.where(is1, pref, 0.0), **red)
        rank2 = jnp.sum(jnp.where(is2, pref, 0.0), **red)
        rt = jnp.zeros((part, LANES), F32)
        for idx, val in enumerate((e1, e2, rank1, rank2, w1, w2)):
            rt = jnp.where(lane == idx, val, rt)
        rt_ref[rows] = rt
    cnt_ref[...] = jnp.broadcast_to(before, cnt_ref.shape)


RT_E1, RT_E2, RT_RANK1, RT_RANK2, RT_W1, RT_W2 = range(6)
SUBLANES = 8


def _post(x, ya, of, orv, og, ga, gb, wua, wuh, wo, hg, fg, rw, rb, ltri):
    T, D = x.shape
    tm = ltri.shape[0] * POST_PARTS
    row = lambda w: pl.BlockSpec((tm, w), lambda i: (i, 0))
    full = lambda a: pl.BlockSpec(a.shape, lambda i: (0,) * a.ndim)
    return pl.pallas_call(
        _post_body,
        grid=(T // tm,),
        in_specs=[row(D), row(ATTN_Q_W), row(HGRN_W), row(HGRN_W), row(HGRN_W), row(D), row(D),
                  full(wua), full(wuh), full(wo), full(hg), full(fg), full(rw), full(rb), full(ltri)],
        out_specs=[row(D), row(D), row(LANES), pl.BlockSpec((SUBLANES, LANES), lambda i: (i, 0))],
        out_shape=[jax.ShapeDtypeStruct((T, D), F32), jax.ShapeDtypeStruct((T, D), BF16),
                   jax.ShapeDtypeStruct((T, LANES), F32),
                   jax.ShapeDtypeStruct((T // tm * SUBLANES, LANES), F32)],
        compiler_params=pltpu.CompilerParams(
            dimension_semantics=("arbitrary",), vmem_limit_bytes=VMEM_LIMIT),
        name="post_mixer_router",
    )(x, ya, of, orv, og, ga, gb, wua, wuh, wo, hg, fg, rw, rb, ltri)


XS_W = D_MODEL // 2 + LANES
SLOT_W1, SLOT_W2, SLOT_E1, SLOT_E2 = 0, 2, 4, 5
SEG_BIG = 32
META_DST, META_LOFF, META_NBIG, META_NSMALL, META_TOTAL8 = (k * N_EXPERTS for k in range(5))
META_W = 5 * N_EXPERTS


def _local_rows(tm):
    return 2 * tm + N_EXPERTS * SUBLANES


def _local_pos(rt, loff_row):
    lanef = lax.broadcasted_iota(I32, rt.shape, 1).astype(F32)
    out = []
    for e_lane, r_lane in ((RT_E1, RT_RANK1), (RT_E2, RT_RANK2)):
        e = rt[:, e_lane:e_lane + 1]
        off = jnp.sum(jnp.where(lanef == e, loff_row, 0.0), axis=1, keepdims=True)
        out.append(off + rt[:, r_lane:r_lane + 1])
    return out


def _one_hot(hit):
    return jnp.where(hit, 1.0, 0.0).astype(BF16)


def _segment_starts(meta_ref, make_copy):
    for e in range(N_EXPERTS):
        dst = meta_ref[0, 0, META_DST + e]
        loff = meta_ref[0, 0, META_LOFF + e]
        n_big = meta_ref[0, 0, META_NBIG + e]
        n_small = meta_ref[0, 0, META_NSMALL + e]

        def big(k, c):
            o = k * SEG_BIG
            make_copy(pl.multiple_of(loff + o, SUBLANES), pl.multiple_of(dst + o, SUBLANES), SEG_BIG).start()
            return c

        lax.fori_loop(0, n_big, big, 0)

        def small(k, c):
            o = n_big * SEG_BIG + k * SUBLANES
            make_copy(pl.multiple_of(loff + o, SUBLANES), pl.multiple_of(dst + o, SUBLANES), SUBLANES).start()
            return c

        lax.fori_loop(0, n_small, small, 0)


def _segment_wait(meta_ref, make_copy, rows):
    total8 = meta_ref[0, 0, META_TOTAL8]
    for bit in range((rows // SUBLANES).bit_length()):
        @pl.when((total8 & (1 << bit)) != 0)
        def _():
            make_copy(0, 0, SUBLANES << bit).wait()


TAIL_PIECES = tuple(SUBLANES << b for b in range((MOE_BLK // SUBLANES).bit_length() - 1))


def _tail_copies(tail_ref, make_copy, action):
    def per_expert(e, c):
        start = tail_ref[0, 0, e]
        n8 = tail_ref[0, 0, N_EXPERTS + e]
        for rows in TAIL_PIECES:
            bit = rows // SUBLANES

            @pl.when((n8 & bit) != 0)
            def _():
                off = (n8 & ~(2 * bit - 1)) * SUBLANES
                action(make_copy(0, pl.multiple_of(start + off, SUBLANES), rows))
        return c

    lax.fori_loop(0, N_EXPERTS, per_expert, 0)

    def per_block(b, c):
        action(make_copy(0, pl.multiple_of(b * MOE_BLK, MOE_BLK), MOE_BLK))
        return c

    lax.fori_loop(tail_ref[0, 0, 2 * N_EXPERTS], tail_ref[0, 0, 2 * N_EXPERTS + 1], per_block, 0)


def _dispatch_body(meta_ref, prev_meta_ref, tail_ref, loff_ref, rt_ref, h_ref, xs_ref, buf_ref, sem):
    i = pl.program_id(0)
    last = pl.num_programs(0) - 1
    slot = i % 2
    tm = rt_ref.shape[0]
    rows = buf_ref.shape[1]
    rt = rt_ref[...]
    dot = functools.partial(jnp.dot, preferred_element_type=F32)
    lane = lax.broadcasted_iota(I32, (tm, LANES), 1)
    l1, l2 = _local_pos(rt, loff_ref[0])
    parts = []
    for pos in (l1, l2):
        hi = jnp.floor(pos * (1.0 / 32.0))
        parts += [hi, pos - 32.0 * hi]
    cols = jnp.zeros((tm, LANES), F32)
    for k, part in enumerate(parts):
        cols = jnp.where(lane == k, part, cols)
    eye = _one_hot(lax.broadcasted_iota(I32, (SUBLANES, LANES), 0) == lax.broadcasted_iota(I32, (SUBLANES, LANES), 1))
    lanes = _nt(eye, cols.astype(BF16))
    r1 = (32.0 * lanes[0:1] + lanes[1:2]).astype(I32)
    r2 = (32.0 * lanes[2:3] + lanes[3:4]).astype(I32)
    row = lax.broadcasted_iota(I32, (rows, tm), 0)
    perm = _one_hot((row == r1) | (row == r2))
    side = jnp.zeros((tm, LANES), F32)
    for w_lane, e_lane, base in ((RT_W1, RT_E1, SLOT_W1), (RT_W2, RT_E2, SLOT_W2)):
        w = rt[:, w_lane:w_lane + 1]
        hi = w.astype(BF16).astype(F32)
        side = jnp.where(lane == base, hi, jnp.where(lane == base + 1, w - hi, side))
        side = jnp.where(lane == SLOT_E1 + (base - SLOT_W1) // 2, rt[:, e_lane:e_lane + 1], side)
    xl = dot(perm, jnp.concatenate([h_ref[...], side.astype(BF16)], axis=1))
    half = D_MODEL // 2
    buf_ref[slot, :, :half] = _pack_pair(xl[:, :half], xl[:, half:D_MODEL])
    buf_ref[slot, :, half:] = lax.bitcast_convert_type(xl[:, D_MODEL:], U32)

    def copier(k):
        def make_copy(src_row, dst_row, n):
            return pltpu.make_async_copy(buf_ref.at[k, pl.ds(src_row, n)], xs_ref.at[pl.ds(dst_row, n)],
                                         sem.at[k])
        return make_copy

    _segment_starts(meta_ref, copier(slot))

    @pl.when(i > 0)
    def _():
        _segment_wait(prev_meta_ref, copier(1 - slot), rows)

    @pl.when(i == last)
    def _():
        _segment_wait(meta_ref, copier(slot), rows)
        buf_ref[slot, :MOE_BLK] = jnp.zeros((MOE_BLK, XS_W), U32)
        _tail_copies(tail_ref, copier(slot), lambda cp: cp.start())
        _tail_copies(tail_ref, copier(slot), lambda cp: cp.wait())


def _dispatch(meta, tail, loff, rt, h2, n_pad):
    T, D = h2.shape
    nt = meta.shape[0]
    tm = T // nt
    return pl.pallas_call(
        _dispatch_body,
        grid=(nt,),
        in_specs=[pl.BlockSpec((1, 1, meta.shape[2]), lambda i: (i, 0, 0), memory_space=pltpu.SMEM),
                  pl.BlockSpec((1, 1, meta.shape[2]), lambda i: (jnp.maximum(i - 1, 0), 0, 0),
                               memory_space=pltpu.SMEM),
                  pl.BlockSpec((1, 1, tail.shape[2]), lambda i: (0, 0, 0), memory_space=pltpu.SMEM),
                  pl.BlockSpec((1, 1, LANES), lambda i: (i, 0, 0)),
                  pl.BlockSpec((tm, LANES), lambda i: (i, 0)),
                  pl.BlockSpec((tm, D), lambda i: (i, 0))],
        out_specs=pl.BlockSpec(memory_space=pl.ANY),
        out_shape=jax.ShapeDtypeStruct((n_pad, XS_W), U32),
        scratch_shapes=[pltpu.VMEM((2, _local_rows(tm), XS_W), U32), pltpu.SemaphoreType.DMA((2,))],
        compiler_params=pltpu.CompilerParams(
            dimension_semantics=("arbitrary",), vmem_limit_bytes=VMEM_LIMIT),
        name="moe_dispatch",
    )(meta, meta, tail, loff, rt, h2)


EXPERT_PARTS = 2


def _expert_body(be_ref, nused_ref, xs_ref, wg_ref, wu_ref, wd_ref, ys_ref, wgb_ref, wub_ref, wdb_ref):
    i = pl.program_id(0)

    @pl.when((i == 0) | (be_ref[i] != be_ref[jnp.maximum(i - 1, 0)]))
    def _():
        wgb_ref[...] = wg_ref[0].astype(BF16)
        wub_ref[...] = wu_ref[0].astype(BF16)
        wdb_ref[...] = wd_ref[0].astype(BF16)

    live = i < nused_ref[0]

    @pl.when(jnp.logical_not(live))
    def _():
        ys_ref[...] = jnp.zeros_like(ys_ref)

    @pl.when(live)
    def _():
        dot = functools.partial(jnp.dot, preferred_element_type=F32)
        part = MOE_BLK // EXPERT_PARTS
        half = D_MODEL // 2
        parts = [slice(k * part, (k + 1) * part) for k in range(EXPERT_PARTS)]
        hs = []
        for rows in parts:
            ha, hb = _unpack_pair(xs_ref[rows, :half])
            hs.append(jnp.concatenate([ha, hb], axis=1).astype(BF16))
        gs = [dot(h, wgb_ref[...]) for h in hs]
        us = [dot(h, wub_ref[...]) for h in hs]
        acts = [(g * _sigmoid(g) * u).astype(BF16) for g, u in zip(gs, us)]
        expert = (be_ref[i] % N_EXPERTS).astype(F32)
        for rows, act in zip(parts, acts):
            rec = lax.bitcast_convert_type(xs_ref[rows, half:], F32)
            w = jnp.where(rec[:, SLOT_E1:SLOT_E1 + 1] == expert,
                          rec[:, SLOT_W1:SLOT_W1 + 1] + rec[:, SLOT_W1 + 1:SLOT_W1 + 2],
                          rec[:, SLOT_W2:SLOT_W2 + 1] + rec[:, SLOT_W2 + 1:SLOT_W2 + 2])
            y = dot(act, wdb_ref[...]) * w
            ys_ref[rows] = _pack_pair(y[:, :half], y[:, half:])


def _experts(block_expert, nused, xs, wg, wu, wd):
    n_pad = xs.shape[0]
    blk_in = pl.BlockSpec((MOE_BLK, XS_W), lambda i, be, nu: (jnp.minimum(i, nu[0] - 1), 0))
    blk_out = pl.BlockSpec((MOE_BLK, D_MODEL // 2), lambda i, be, nu: (i, 0))
    wspec = lambda a: pl.BlockSpec((1,) + a.shape[1:], lambda i, be, nu: (be[i], 0, 0))
    return pl.pallas_call(
        _expert_body,
        grid_spec=pltpu.PrefetchScalarGridSpec(
            num_scalar_prefetch=2,
            grid=(n_pad // MOE_BLK,),
            in_specs=[blk_in, wspec(wg), wspec(wu), wspec(wd)],
            out_specs=blk_out,
            scratch_shapes=[pltpu.VMEM(wg.shape[1:], BF16), pltpu.VMEM(wu.shape[1:], BF16),
                            pltpu.VMEM(wd.shape[1:], BF16)],
        ),
        out_shape=jax.ShapeDtypeStruct((n_pad, D_MODEL // 2), U32),
        compiler_params=pltpu.CompilerParams(
            dimension_semantics=("arbitrary",), vmem_limit_bytes=VMEM_LIMIT),
        name="moe_experts",
    )(block_expert, nused, xs, wg, wu, wd)


def _combine_body(meta_ref, next_meta_ref, loff_ref, rt_ref, x_ref, fg_ref, ys_ref, o_ref, buf_ref, sem,
                  *, final_norm):
    i = pl.program_id(0)
    slot = i % 2

    def copier(k):
        def make_copy(loc_row, glob_row, n):
            return pltpu.make_async_copy(ys_ref.at[pl.ds(glob_row, n)], buf_ref.at[k, pl.ds(loc_row, n)],
                                         sem.at[k])
        return make_copy

    rows = buf_ref.shape[1]

    @pl.when(i == 0)
    def _():
        buf_ref[...] = jnp.zeros_like(buf_ref)
        _segment_starts(meta_ref, copier(0))

    @pl.when(i + 1 < pl.num_programs(0))
    def _():
        _segment_starts(next_meta_ref, copier(1 - slot))

    l1, l2 = _local_pos(rt_ref[...], loff_ref[0])
    col = lax.broadcasted_iota(I32, (rt_ref.shape[0], rows), 1)
    pt = _one_hot((col == l1.astype(I32)) | (col == l2.astype(I32)))
    _segment_wait(meta_ref, copier(slot), rows)
    yl = jnp.concatenate(_unpack_pair(buf_ref[slot]), axis=1).astype(BF16)
    out = x_ref[...] + jnp.dot(pt, yl, preferred_element_type=F32)
    if final_norm:
        out = _rms(out) * fg_ref[...]
    o_ref[...] = out


def _combine(meta, loff, rt, x, fg, ys, final_norm):
    T, D = x.shape
    nt = meta.shape[0]
    tm = T // nt
    return pl.pallas_call(
        functools.partial(_combine_body, final_norm=final_norm),
        grid=(nt,),
        in_specs=[pl.BlockSpec((1, 1, meta.shape[2]), lambda i: (i, 0, 0), memory_space=pltpu.SMEM),
                  pl.BlockSpec((1, 1, meta.shape[2]), lambda i: (jnp.minimum(i + 1, nt - 1), 0, 0),
                               memory_space=pltpu.SMEM),
                  pl.BlockSpec((1, 1, LANES), lambda i: (i, 0, 0)),
                  pl.BlockSpec((tm, LANES), lambda i: (i, 0)),
                  pl.BlockSpec((tm, D), lambda i: (i, 0)),
                  pl.BlockSpec((1, D), lambda i: (0, 0)),
                  pl.BlockSpec(memory_space=pl.ANY)],
        out_specs=pl.BlockSpec((tm, D), lambda i: (i, 0)),
        out_shape=jax.ShapeDtypeStruct((T, D), F32),
        scratch_shapes=[pltpu.VMEM((2, _local_rows(tm), D // 2), U32), pltpu.SemaphoreType.DMA((2,))],
        compiler_params=pltpu.CompilerParams(
            dimension_semantics=("arbitrary",), vmem_limit_bytes=VMEM_LIMIT),
        name="moe_combine",
    )(meta, meta, loff, rt, x, fg, ys)


def _rope_tables(L):
    half = HEAD_DIM // 2
    inv = ROPE_THETA ** (-jnp.arange(half, dtype=F32) / half)
    ang = jnp.arange(L, dtype=F32)[:, None] * inv[None, :]
    lane = np.arange(LANES)
    idx = lane % half
    first = jnp.asarray((lane % HEAD_DIM) < half)
    cos = jnp.cos(ang)[:, idx]
    sin = jnp.sin(ang)[:, idx]
    return cos, jnp.where(first, -sin, 0.0), jnp.where(first, 0.0, sin)


def _moe_layout(cnt, T, layer):
    nt = cnt.shape[0] // SUBLANES
    c = cnt.reshape(nt, SUBLANES, LANES)[:, 0, :N_EXPERTS].astype(I32)
    c8 = (c + SUBLANES - 1) // SUBLANES * SUBLANES
    tile_off = jnp.cumsum(c8, axis=0) - c8
    total = jnp.sum(c8, axis=0)
    padded = (total + MOE_BLK - 1) // MOE_BLK * MOE_BLK
    pend = jnp.cumsum(padded)
    dst = (pend - padded)[None, :] + tile_off
    loff = jnp.cumsum(c8, axis=1) - c8
    n8 = c8 // SUBLANES
    per_big = SEG_BIG // SUBLANES
    total8 = jnp.broadcast_to(jnp.sum(n8, axis=1, keepdims=True), n8.shape)
    meta = jnp.concatenate([dst, loff, n8 // per_big, n8 % per_big, total8], axis=1).reshape(nt, 1, META_W)
    loff_f = jnp.pad(loff.astype(F32), ((0, 0), (0, LANES - N_EXPERTS))).reshape(nt, 1, LANES)
    n_pad = 2 * T + nt * N_EXPERTS * SUBLANES + N_EXPERTS * MOE_BLK
    n_pad = (n_pad + MOE_BLK - 1) // MOE_BLK * MOE_BLK
    starts = jnp.arange(n_pad // MOE_BLK, dtype=I32) * MOE_BLK
    block_expert = jnp.minimum(jnp.sum(starts[:, None] >= pend[None, :], axis=1), N_EXPERTS - 1)
    block_expert = block_expert.astype(I32) + layer * N_EXPERTS
    nused = (pend[-1:] // MOE_BLK).astype(I32)
    tail = jnp.concatenate([pend - padded + total, (padded - total) // SUBLANES, nused,
                            jnp.full((1,), n_pad // MOE_BLK, I32)]).reshape(1, 1, 2 * N_EXPERTS + 2)
    return meta, tail, loff_f, block_expert, nused, n_pad


def kernel(x, attn_norm_g, w_in, sink_logits, hgrn_lb_table, hgrn_norm_g, w_up_attn, w_up_hgrn, w_out,
           ffn_norm_g, w_router_group, b_router_group, w_router_expert, b_router_expert,
           w_expert_gate, w_expert_up, w_expert_down, final_norm_g):
    B, L, D = x.shape
    T = B * L
    depth = w_in.shape[0]
    sm = jax.nn.softmax(hgrn_lb_table.astype(F32), axis=0)
    lower_bounds = jnp.cumsum(sm, axis=0) - sm[:1]
    cos, s1, s2 = _rope_tables(L)
    tm_post = min(TM_POST, T)
    ltri = jnp.tril(jnp.ones((tm_post // POST_PARTS,) * 2, BF16), -1)
    fin_g = final_norm_g.reshape(1, D).astype(F32)
    w_in_bf16 = w_in.astype(BF16)
    all_layers = lambda w: w.reshape((depth * N_EXPERTS,) + w.shape[2:])
    wg_all, wu_all, wd_all = all_layers(w_expert_gate), all_layers(w_expert_up), all_layers(w_expert_down)

    for l in range(depth):
        qa, ka, va, qb, sf, sb, ib, og, ga, gb = _inproj(
            x, attn_norm_g[l].reshape(1, D), w_in_bf16, l, cos, s1, s2)
        ya = _attention(qa, ka, va, sink_logits[l].astype(F32) * LOG2E)
        of, orv = _hgrn(qb, sf, sb, ib, lower_bounds[l])

        rw = jnp.concatenate(
            [w_router_group[l], jnp.transpose(w_router_expert[l], (1, 0, 2)).reshape(D, N_EXPERTS)], axis=1)
        rw = jnp.pad(rw.astype(F32), ((0, 0), (0, LANES - rw.shape[1])))
        rwh = rw.astype(BF16)
        rw2 = jnp.concatenate([rwh, (rw - rwh.astype(F32)).astype(BF16)], axis=1)
        rb = jnp.concatenate([b_router_group[l], b_router_expert[l].reshape(-1)]).astype(F32)
        rb = jnp.pad(rb, (0, LANES - rb.shape[0])).reshape(1, LANES)
        flat = lambda a: a.reshape(T, a.shape[-1])
        xn, h2, rt, cnt = _post(
            flat(x), flat(ya), flat(of), flat(orv), flat(og), flat(ga), flat(gb),
            w_up_attn[l].astype(BF16), w_up_hgrn[l].astype(BF16), w_out[l].astype(BF16),
            hgrn_norm_g[l].reshape(1, HGRN_W).astype(F32), ffn_norm_g[l].reshape(1, D).astype(F32),
            rw2, rb, ltri)

        meta, tail, loff, block_expert, nused, n_pad = _moe_layout(cnt, T, l)
        xs = _dispatch(meta, tail, loff, rt, h2, n_pad)
        ys = _experts(block_expert, nused, xs, wg_all, wu_all, wd_all)
        x = _combine(meta, loff, rt, xn, fin_g, ys, l == depth - 1).reshape(B, L, D)
    return x
```

```python
import functools

import numpy as np
import jax
import jax.numpy as jnp
from jax import lax
from jax.experimental import pallas as pl
from jax.experimental.pallas import tpu as pltpu

F32 = jnp.float32
BF16 = jnp.bfloat16
U32 = jnp.uint32
I32 = jnp.int32

D_MODEL = 1024
N_Q_HEADS = 8
N_KV_HEADS = 2
HEAD_DIM = 64
WINDOW = 128
ROPE_THETA = 10000.0
MASK_VALUE = -1e30
HGRN_HEADS = 4
HGRN_DIM = 128
LOGF_MIN = -4.0
F_MIN = float(np.exp(LOGF_MIN))
LOG2E = float(np.log2(np.e))
Q_SCALE = HEAD_DIM ** -0.5 * LOG2E
LB_FLOOR = 1e-30
N_GROUPS = 4
EXPERTS_PER_GROUP = 8
N_EXPERTS = N_GROUPS * EXPERTS_PER_GROUP
D_EXPERT = 512
NORM_EPS = 1e-6

ATTN_Q_W = N_Q_HEADS * HEAD_DIM
ATTN_KV_W = N_KV_HEADS * HEAD_DIM
HGRN_W = HGRN_HEADS * HGRN_DIM
OFF_QA = 0
OFF_KA = OFF_QA + ATTN_Q_W
OFF_VA = OFF_KA + ATTN_KV_W
OFF_QB = OFF_VA + ATTN_KV_W
OFF_ZF = OFF_QB + HGRN_W
OFF_ZB = OFF_ZF + HGRN_W
OFF_IB = OFF_ZB + HGRN_W
OFF_OG = OFF_IB + HGRN_W
OFF_GA = OFF_OG + HGRN_W
OFF_GB = OFF_GA + D_MODEL

LANES = 128
VMEM_LIMIT = 56 * 1024 * 1024

TM_PROJ = 1024
TQ_ATTN = 2048
TH_HGRN = 2048
TM_POST = 512
MOE_BLK = 512


def _sigmoid(z):
    return 1.0 / (1.0 + jnp.exp(-z))


def _rms(x):
    return x * lax.rsqrt(jnp.mean(x * x, axis=-1, keepdims=True) + NORM_EPS)


def _nt(a, b):
    return lax.dot_general(a, b, (((1,), (1,)), ((), ())), preferred_element_type=F32)


def _pack_pair(a, b):
    ua = lax.bitcast_convert_type(a.astype(BF16).astype(F32), U32)
    ub = lax.bitcast_convert_type(b.astype(BF16).astype(F32), U32)
    return ua | (ub >> 16)


def _unpack_pair(u):
    hi = lax.bitcast_convert_type(u & jnp.uint32(0xFFFF0000), F32)
    lo = lax.bitcast_convert_type(u << 16, F32)
    return hi, lo


def _inproj_body(x_ref, g_ref, w_ref, cos_ref, s1_ref, s2_ref,
                 qa_ref, ka_ref, va_ref, qb_ref, sf_ref, sb_ref, ib_ref, og_ref, ga_ref, gb_ref):
    h = (_rms(x_ref[0]) * g_ref[...]).astype(BF16)

    def proj(off, width):
        return jnp.dot(h, w_ref[:, off:off + width], preferred_element_type=F32)

    cos, s1, s2 = cos_ref[...], s1_ref[...], s2_ref[...]

    def rope(t):
        return t * cos + pltpu.roll(t, 96, 1) * s1 + pltpu.roll(t, 32, 1) * s2

    ga_ref[0] = _sigmoid(proj(OFF_GA, D_MODEL)).astype(BF16)
    gb_ref[0] = _sigmoid(proj(OFF_GB, D_MODEL)).astype(BF16)
    t = proj(OFF_OG, HGRN_W)
    og_ref[0] = (t * _sigmoid(t)).astype(BF16)
    t = proj(OFF_QB, HGRN_W)
    qb_ref[0] = (t * _sigmoid(t)).astype(BF16)
    sf_ref[0] = _sigmoid(proj(OFF_ZF, HGRN_W)).astype(BF16)
    sb_ref[0] = _sigmoid(proj(OFF_ZB, HGRN_W)).astype(BF16)
    q = proj(OFF_QA, ATTN_Q_W)
    for c in range(ATTN_Q_W // LANES):
        sl = slice(c * LANES, (c + 1) * LANES)
        qa_ref[0, :, sl] = (rope(q[:, sl]) * Q_SCALE).astype(BF16)
    k = rope(proj(OFF_KA, ATTN_KV_W))
    ka_ref[0, :, :LANES] = k.astype(BF16)
    ka_ref[0, :, LANES:] = pltpu.roll(k, 64, 1).astype(BF16)
    v = proj(OFF_VA, ATTN_KV_W)
    va_ref[0, :, :LANES] = v.astype(BF16)
    va_ref[0, :, LANES:] = pltpu.roll(v, 64, 1).astype(BF16)
    ib_ref[0] = proj(OFF_IB, HGRN_W).astype(BF16)


def _inproj(x, g, w_all, layer, cos, s1, s2):
    B, L, D = x.shape
    tm = min(TM_PROJ, L)
    widths = (ATTN_Q_W, 2 * ATTN_KV_W, 2 * ATTN_KV_W, HGRN_W, HGRN_W, HGRN_W, HGRN_W, HGRN_W,
              D_MODEL, D_MODEL)
    tab = pl.BlockSpec((tm, LANES), lambda b, i: (i, 0))
    return pl.pallas_call(
        _inproj_body,
        grid=(B, L // tm),
        in_specs=[
            pl.BlockSpec((1, tm, D), lambda b, i: (b, i, 0)),
            pl.BlockSpec((1, D), lambda b, i: (0, 0)),
            pl.BlockSpec((None,) + w_all.shape[1:], lambda b, i: (layer, 0, 0), pipeline_mode=pl.Buffered(1)),
            tab, tab, tab,
        ],
        out_specs=[pl.BlockSpec((1, tm, w), lambda b, i: (b, i, 0)) for w in widths],
        out_shape=[jax.ShapeDtypeStruct((B, L, w), BF16) for w in widths],
        compiler_params=pltpu.CompilerParams(
            dimension_semantics=("arbitrary", "arbitrary"), vmem_limit_bytes=VMEM_LIMIT),
        name="inproj",
    )(x, g, w_all, cos, s1, s2)


def _attn_body(sink_ref, q_ref, kp_ref, km_ref, kn_ref, vp_ref, vm_ref, vn_ref, o_ref, *, tq, seq):
    i = pl.program_id(1)
    kw = jnp.concatenate([kp_ref[0], km_ref[0], kn_ref[0]], axis=0)
    vw = jnp.concatenate([vp_ref[0], vm_ref[0], vn_ref[0]], axis=0)
    lo = lax.broadcasted_iota(I32, (1, LANES), 1) < HEAD_DIM
    zero = jnp.zeros((), BF16)

    def variants(t):
        a, b = t[:, :LANES], t[:, LANES:]
        return ((jnp.where(lo, a, zero), jnp.where(lo, zero, b)),
                (jnp.where(lo, b, zero), jnp.where(lo, zero, a)))

    kvar, vvar = variants(kw), variants(vw)
    row = lax.broadcasted_iota(I32, (WINDOW, 3 * WINDOW), 0)
    col = lax.broadcasted_iota(I32, (WINDOW, 3 * WINDOW), 1)
    band = (col >= row) & (col <= row + 2 * WINDOW)
    ncol = ATTN_Q_W // LANES
    group = lambda c: (2 * c) // (N_Q_HEADS // N_KV_HEADS)
    heads = [(c, hh) for c in range(ncol) for hh in range(2)]
    for j in range(tq // WINDOW):
        kpos = col + (i * tq + (j - 1) * WINDOW)
        bias = jnp.where(band & (kpos >= 0) & (kpos < seq), 0.0, MASK_VALUE)
        rows = slice(j * WINDOW, (j + 1) * WINDOW)
        win = slice(j * WINDOW, (j + 3) * WINDOW)
        scores = [_nt(q_ref[0, rows, c * LANES:(c + 1) * LANES], kvar[group(c)][hh][win]) for c, hh in heads]
        probs, denoms = [], []
        for (c, hh), s in zip(heads, scores):
            s = jnp.concatenate([s[:, :WINDOW] + bias[:, :WINDOW], s[:, WINDOW:2 * WINDOW],
                                 s[:, 2 * WINDOW:] + bias[:, 2 * WINDOW:]], axis=1)
            sk = sink_ref[2 * c + hh]
            m = jnp.maximum(jnp.max(s, axis=1, keepdims=True), sk)
            p = jnp.exp2(s - m)
            denoms.append(jnp.sum(p, axis=1, keepdims=True) + jnp.exp2(sk - m))
            probs.append(p.astype(BF16))
        for c in range(ncol):
            g = group(c)
            pv = jnp.dot(jnp.concatenate(probs[2 * c:2 * c + 2], axis=1),
                         jnp.concatenate([vvar[g][0][win], vvar[g][1][win]], axis=0),
                         preferred_element_type=F32)
            inv = jnp.where(lo, 1.0 / denoms[2 * c], 1.0 / denoms[2 * c + 1])
            o_ref[0, rows, c * LANES:(c + 1) * LANES] = (pv * inv).astype(BF16)


def _attention(qa, ka, va, sink):
    B, L, _ = qa.shape
    tq = min(TQ_ATTN, L)
    r = tq // WINDOW
    nb = L // WINDOW
    kvw = 2 * ATTN_KV_W
    main = pl.BlockSpec((1, tq, kvw), lambda b, i: (b, i, 0))
    prev = pl.BlockSpec((1, WINDOW, kvw), lambda b, i: (b, jnp.maximum(i * r - 1, 0), 0))
    nxt = pl.BlockSpec((1, WINDOW, kvw), lambda b, i: (b, jnp.minimum(i * r + r, nb - 1), 0))
    return pl.pallas_call(
        functools.partial(_attn_body, tq=tq, seq=L),
        grid=(B, L // tq),
        in_specs=[
            pl.BlockSpec(memory_space=pltpu.SMEM),
            pl.BlockSpec((1, tq, ATTN_Q_W), lambda b, i: (b, i, 0)),
            prev, main, nxt, prev, main, nxt,
        ],
        out_specs=pl.BlockSpec((1, tq, ATTN_Q_W), lambda b, i: (b, i, 0)),
        out_shape=jax.ShapeDtypeStruct((B, L, ATTN_Q_W), BF16),
        compiler_params=pltpu.CompilerParams(
            dimension_semantics=("arbitrary", "arbitrary"), vmem_limit_bytes=VMEM_LIMIT),
        name="win_attn",
    )(sink, qa, ka, ka, ka, va, va, va)


HB = 128
LEAF = 32
HGRN_LEVELS = tuple(LEAF << k for k in range((HB // LEAF).bit_length() - 1))
LEAF_SHIFT = 0.5 * LEAF * (-LOGF_MIN) * LOG2E


def _hgrn_masks(reverse):
    r = lax.broadcasted_iota(I32, (HB, HB), 0)
    c = lax.broadcasted_iota(I32, (HB, HB), 1)
    if reverse:
        r, c = c, r
    tri = jnp.where(c <= r, 1.0, 0.0).astype(BF16)
    same = lambda w: (r >> (w.bit_length() - 1)) == (c >> (w.bit_length() - 1))
    cross = lambda w: ((r & w) != 0) & ((c & w) == 0)
    lev = jnp.full((HB, HB), len(HGRN_LEVELS), I32)
    for k, w in reversed(list(enumerate(HGRN_LEVELS[:-1]))):
        lev = jnp.where(same(2 * w) & cross(w), k + 1, lev)
    lev = jnp.where(same(LEAF) & (c <= r), 0, lev)
    return tri, lev


def _hgrn_gates(s, lb, lbc, tri):
    f = lbc + (1.0 - lb) * s
    logf = jnp.maximum(jnp.log(f), LOGF_MIN)
    kk = 1.0 - jnp.maximum(f, F_MIN)
    hi = logf.astype(BF16)
    mid = (logf - hi.astype(F32)).astype(BF16)
    dot = functools.partial(jnp.dot, preferred_element_type=F32)
    return kk, (dot(tri, hi) + dot(tri, mid)) * LOG2E


def _hgrn_scores(q, kk, cum, v, st_ref, reverse):
    def ref_row(idx):
        return cum[idx:idx + 1]

    total = ref_row(0) if reverse else ref_row(HB - 1)
    q_blk = q * jnp.exp2(cum)
    k_blk = kk * jnp.exp2(total - cum)

    ql, kl = [], []
    for c in range(HB // LEAF):
        rows = slice(c * LEAF, (c + 1) * LEAF)
        edge = (c + 1) * LEAF if reverse else c * LEAF - 1
        loc = (cum[rows] - ref_row(edge) if 0 <= edge < HB else cum[rows]) + LEAF_SHIFT
        ql.append(q[rows] * jnp.exp2(loc))
        kl.append(kk[rows] * jnp.exp2(-loc))
    pairs = [(jnp.concatenate(ql, axis=0), jnp.concatenate(kl, axis=0))]
    for w in HGRN_LEVELS:
        qp, kp = [], []
        zeros = jnp.zeros((w, HB), F32)
        for s0 in range(0, HB, 2 * w):
            first, second = slice(s0, s0 + w), slice(s0 + w, s0 + 2 * w)
            if reverse:
                ref = ref_row(s0 + w)
                qp += [q[first] * jnp.exp2(cum[first] - ref), zeros]
                kp += [zeros, kk[second] * jnp.exp2(ref - cum[second])]
            else:
                ref = ref_row(s0 + w - 1)
                qp += [zeros, q[second] * jnp.exp2(cum[second] - ref)]
                kp += [kk[first] * jnp.exp2(ref - cum[first]), zeros]
        pairs.append((jnp.concatenate(qp, axis=0), jnp.concatenate(kp, axis=0)))
    scores = [_nt(qq.astype(BF16), kq.astype(BF16)) for qq, kq in pairs]
    st = st_ref[...]
    o_state = _nt(q_blk.astype(BF16), st.astype(BF16))
    st_ref[...] = st * jnp.exp2(total) + lax.dot_general(
        v, k_blk.astype(BF16), (((0,), (0,)), ((), ())), preferred_element_type=F32)
    return scores, o_state


def _hgrn_output(scores, o_state, v, lev):
    a = scores[-1]
    for k in reversed(range(len(scores) - 1)):
        a = jnp.where(lev == k, scores[k], a)
    return jnp.dot(a.astype(BF16), v, preferred_element_type=F32) + o_state


def _hgrn_body(lb_ref, qf_ref, sf_ref, vf_ref, qr_ref, sr_ref, vr_ref, of_ref, or_ref, st_ref, *, th):
    @pl.when(pl.program_id(1) == 0)
    def _():
        st_ref[...] = jnp.zeros_like(st_ref)

    lb = lb_ref[...]
    lbc = jnp.maximum(lb, LB_FLOOR)
    masks = (_hgrn_masks(False), _hgrn_masks(True))
    nblk = th // HB
    streams = ((qf_ref, sf_ref, vf_ref, of_ref), (qr_ref, sr_ref, vr_ref, or_ref))

    def body(t, carry):
        rows = [pl.multiple_of((nblk - 1 - t if d else t) * HB, HB) for d in range(2)]
        units = [(d, h * HGRN_DIM) for d in range(2) for h in range(HGRN_HEADS)]

        def load(d, k, lane0):
            return streams[d][k][0, pl.ds(rows[d], HB), lane0:lane0 + HGRN_DIM]

        gates = [_hgrn_gates(load(d, 1, c0).astype(F32), lb[d:d + 1, c0:c0 + HGRN_DIM],
                             lbc[d:d + 1, c0:c0 + HGRN_DIM], masks[d][0]) for d, c0 in units]
        mids = [_hgrn_scores(load(d, 0, c0).astype(F32), kk, cum, load(d, 2, c0),
                             st_ref.at[d, c0 // HGRN_DIM], bool(d))
                for (d, c0), (kk, cum) in zip(units, gates)]
        for (d, c0), (scores, o_state) in zip(units, mids):
            o = _hgrn_output(scores, o_state, load(d, 2, c0), masks[d][1])
            streams[d][3][0, pl.ds(rows[d], HB), c0:c0 + HGRN_DIM] = o.astype(BF16)
        return carry

    lax.fori_loop(0, nblk, body, 0)


def _hgrn(qb, sf, sb, ib, lb):
    B, L, _ = qb.shape
    th = min(TH_HGRN, L)
    n = L // th
    fwd = pl.BlockSpec((1, th, HGRN_W), lambda b, i: (b, i, 0))
    rev = pl.BlockSpec((1, th, HGRN_W), lambda b, i: (b, n - 1 - i, 0))
    return pl.pallas_call(
        functools.partial(_hgrn_body, th=th),
        grid=(B, n),
        in_specs=[pl.BlockSpec((2, HGRN_W), lambda b, i: (0, 0)), fwd, fwd, fwd, rev, rev, rev],
        out_specs=[fwd, rev],
        out_shape=[jax.ShapeDtypeStruct((B, L, HGRN_W), BF16)] * 2,
        scratch_shapes=[pltpu.VMEM((2, HGRN_HEADS, HGRN_DIM, HGRN_DIM), F32)],
        compiler_params=pltpu.CompilerParams(
            dimension_semantics=("arbitrary", "arbitrary"), vmem_limit_bytes=VMEM_LIMIT),
        name="hgrn2",
    )(lb, qb, sf, ib, qb, sb, ib)


POST_PARTS = 2


def _post_mix(rows, x_ref, ya_ref, of_ref, or_ref, og_ref, ga_ref, gb_ref, wua_ref, wuh_ref, wo_ref,
              hg_ref, fg_ref, rw_ref, rb_ref, xn_ref, h2_ref):
    dot = functools.partial(jnp.dot, preferred_element_type=F32)
    o = of_ref[rows].astype(F32) + or_ref[rows].astype(F32)
    yh = (_rms(o) * hg_ref[...] * og_ref[rows].astype(F32)).astype(BF16)
    merged = (ga_ref[rows].astype(F32) * dot(ya_ref[rows], wua_ref[...])
              + gb_ref[rows].astype(F32) * dot(yh, wuh_ref[...]))
    xn = x_ref[rows] + dot(merged.astype(BF16), wo_ref[...])
    xn_ref[rows] = xn
    h2 = _rms(xn) * fg_ref[...]
    hi = h2.astype(BF16)
    h2_ref[rows] = hi
    lo = (h2 - hi.astype(F32)).astype(BF16)
    a = dot(hi, rw_ref[...])
    return a[:, :LANES] + a[:, LANES:] + dot(lo, rw_ref[:, :LANES]) + rb_ref[...]


def _post_route(logits):
    lane = lax.broadcasted_iota(I32, logits.shape, 1)
    lanef = lane.astype(F32)
    big = jnp.float32(1e9)
    ninf = jnp.float32(-jnp.inf)
    red = dict(axis=1, keepdims=True)
    gmask = lane < N_GROUPS
    mg = jnp.max(jnp.where(gmask, logits, ninf), **red)
    p_top = 1.0 / jnp.sum(jnp.where(gmask, jnp.exp(logits - mg), 0.0), **red)
    gi = jnp.min(jnp.where(gmask & (logits == mg), lanef, big), **red)
    emask = ((lane >= N_GROUPS) & (lane < N_GROUPS + N_EXPERTS)
             & (((lane - N_GROUPS) >> 3).astype(F32) == gi))
    le = jnp.where(emask, logits, ninf)
    m1 = jnp.max(le, **red)
    i1 = jnp.min(jnp.where(le == m1, lanef, big), **red)
    le2 = jnp.where(lanef == i1, ninf, le)
    m2 = jnp.max(le2, **red)
    i2 = jnp.min(jnp.where(le2 == m2, lanef, big), **red)
    r = jnp.exp(m2 - m1)
    w1 = 1.0 / (1.0 + r)
    w2 = r * w1
    e1 = i1 - N_GROUPS
    e2 = i2 - N_GROUPS
    is1, is2 = lanef == e1, lanef == e2
    return (e1, e2, p_top * w1, p_top * w2), is1, is2, jnp.where(is1 | is2, 1.0, 0.0)


def _post_body(x_ref, ya_ref, of_ref, or_ref, og_ref, ga_ref, gb_ref, wua_ref, wuh_ref, wo_ref,
               hg_ref, fg_ref, rw_ref, rb_ref, ltri_ref,
               xn_ref, h2_ref, rt_ref, cnt_ref):
    tm = x_ref.shape[0]
    part = tm // POST_PARTS
    parts = [slice(k * part, (k + 1) * part) for k in range(POST_PARTS)]
    logits = [_post_mix(rows, x_ref, ya_ref, of_ref, or_ref, og_ref, ga_ref, gb_ref, wua_ref, wuh_ref,
                        wo_ref, hg_ref, fg_ref, rw_ref, rb_ref, xn_ref, h2_ref) for rows in parts]
    routed = [_post_route(lg) for lg in logits]
    before = jnp.zeros((1, LANES), F32)
    red = dict(axis=1, keepdims=True)
    lane = lax.broadcasted_iota(I32, (part, LANES), 1)
    for rows, ((e1, e2, w1, w2), is1, is2, onehot) in zip(parts, routed):
        pref = jnp.dot(ltri_ref[...], onehot.astype(BF16), preferred_element_type=F32) + before
        before = before + jnp.sum(onehot, axis=0, keepdims=True)
        rank1 = jnp.sum(jnp.where(is1, pref, 0.0), **red)
        rank2 = jnp.sum(jnp.where(is2, pref, 0.0), **red)
        rt = jnp.zeros((part, LANES), F32)
        for idx, val in enumerate((e1, e2, rank1, rank2, w1, w2)):
            rt = jnp.where(lane == idx, val, rt)
        rt_ref[rows] = rt
    cnt_ref[...] = jnp.broadcast_to(before, cnt_ref.shape)


RT_E1, RT_E2, RT_RANK1, RT_RANK2, RT_W1, RT_W2 = range(6)
SUBLANES = 8


def _post(x, ya, of, orv, og, ga, gb, wua, wuh, wo, hg, fg, rw, rb, ltri):
    T, D = x.shape
    tm = ltri.shape[0] * POST_PARTS
    row = lambda w: pl.BlockSpec((tm, w), lambda i: (i, 0))
    full = lambda a: pl.BlockSpec(a.shape, lambda i: (0,) * a.ndim)
    return pl.pallas_call(
        _post_body,
        grid=(T // tm,),
        in_specs=[row(D), row(ATTN_Q_W), row(HGRN_W), row(HGRN_W), row(HGRN_W), row(D), row(D),
                  full(wua), full(wuh), full(wo), full(hg), full(fg), full(rw), full(rb), full(ltri)],
        out_specs=[row(D), row(D), row(LANES), pl.BlockSpec((SUBLANES, LANES), lambda i: (i, 0))],
        out_shape=[jax.ShapeDtypeStruct((T, D), F32), jax.ShapeDtypeStruct((T, D), BF16),
                   jax.ShapeDtypeStruct((T, LANES), F32),
                   jax.ShapeDtypeStruct((T // tm * SUBLANES, LANES), F32)],
        compiler_params=pltpu.CompilerParams(
            dimension_semantics=("arbitrary",), vmem_limit_bytes=VMEM_LIMIT),
        name="post_mixer_router",
    )(x, ya, of, orv, og, ga, gb, wua, wuh, wo, hg, fg, rw, rb, ltri)


XS_W = D_MODEL // 2 + LANES
SLOT_W1, SLOT_W2, SLOT_E1, SLOT_E2 = 0, 2, 4, 5
SEG_BIG = 32
META_DST, META_LOFF, META_NBIG, META_NSMALL, META_TOTAL8 = (k * N_EXPERTS for k in range(5))
META_W = 5 * N_EXPERTS


def _local_rows(tm):
    return 2 * tm + N_EXPERTS * SUBLANES


def _local_pos(rt, loff_row):
    lanef = lax.broadcasted_iota(I32, rt.shape, 1).astype(F32)
    out = []
    for e_lane, r_lane in ((RT_E1, RT_RANK1), (RT_E2, RT_RANK2)):
        e = rt[:, e_lane:e_lane + 1]
        off = jnp.sum(jnp.where(lanef == e, loff_row, 0.0), axis=1, keepdims=True)
        out.append(off + rt[:, r_lane:r_lane + 1])
    return out


def _one_hot(hit):
    return jnp.where(hit, 1.0, 0.0).astype(BF16)


def _segment_starts(meta_ref, make_copy):
    for e in range(N_EXPERTS):
        dst = meta_ref[0, 0, META_DST + e]
        loff = meta_ref[0, 0, META_LOFF + e]
        n_big = meta_ref[0, 0, META_NBIG + e]
        n_small = meta_ref[0, 0, META_NSMALL + e]

        def big(k, c):
            o = k * SEG_BIG
            make_copy(pl.multiple_of(loff + o, SUBLANES), pl.multiple_of(dst + o, SUBLANES), SEG_BIG).start()
            return c

        lax.fori_loop(0, n_big, big, 0)

        def small(k, c):
            o = n_big * SEG_BIG + k * SUBLANES
            make_copy(pl.multiple_of(loff + o, SUBLANES), pl.multiple_of(dst + o, SUBLANES), SUBLANES).start()
            return c

        lax.fori_loop(0, n_small, small, 0)


def _segment_wait(meta_ref, make_copy, rows):
    total8 = meta_ref[0, 0, META_TOTAL8]
    for bit in range((rows // SUBLANES).bit_length()):
        @pl.when((total8 & (1 << bit)) != 0)
        def _():
            make_copy(0, 0, SUBLANES << bit).wait()


TAIL_PIECES = tuple(SUBLANES << b for b in range((MOE_BLK // SUBLANES).bit_length() - 1))


def _tail_copies(tail_ref, make_copy, action):
    def per_expert(e, c):
        start = tail_ref[0, 0, e]
        n8 = tail_ref[0, 0, N_EXPERTS + e]
        for rows in TAIL_PIECES:
            bit = rows // SUBLANES

            @pl.when((n8 & bit) != 0)
            def _():
                off = (n8 & ~(2 * bit - 1)) * SUBLANES
                action(make_copy(0, pl.multiple_of(start + off, SUBLANES), rows))
        return c

    lax.fori_loop(0, N_EXPERTS, per_expert, 0)

    def per_block(b, c):
        action(make_copy(0, pl.multiple_of(b * MOE_BLK, MOE_BLK), MOE_BLK))
        return c

    lax.fori_loop(tail_ref[0, 0, 2 * N_EXPERTS], tail_ref[0, 0, 2 * N_EXPERTS + 1], per_block, 0)


def _dispatch_body(meta_ref, prev_meta_ref, tail_ref, loff_ref, rt_ref, h_ref, xs_ref, buf_ref, sem):
    i = pl.program_id(0)
    last = pl.num_programs(0) - 1
    slot = i % 2
    tm = rt_ref.shape[0]
    rows = buf_ref.shape[1]
    rt = rt_ref[...]
    dot = functools.partial(jnp.dot, preferred_element_type=F32)
    lane = lax.broadcasted_iota(I32, (tm, LANES), 1)
    l1, l2 = _local_pos(rt, loff_ref[0])
    parts = []
    for pos in (l1, l2):
        hi = jnp.floor(pos * (1.0 / 32.0))
        parts += [hi, pos - 32.0 * hi]
    cols = jnp.zeros((tm, LANES), F32)
    for k, part in enumerate(parts):
        cols = jnp.where(lane == k, part, cols)
    eye = _one_hot(lax.broadcasted_iota(I32, (SUBLANES, LANES), 0) == lax.broadcasted_iota(I32, (SUBLANES, LANES), 1))
    lanes = _nt(eye, cols.astype(BF16))
    r1 = (32.0 * lanes[0:1] + lanes[1:2]).astype(I32)
    r2 = (32.0 * lanes[2:3] + lanes[3:4]).astype(I32)
    row = lax.broadcasted_iota(I32, (rows, tm), 0)
    perm = _one_hot((row == r1) | (row == r2))
    side = jnp.zeros((tm, LANES), F32)
    for w_lane, e_lane, base in ((RT_W1, RT_E1, SLOT_W1), (RT_W2, RT_E2, SLOT_W2)):
        w = rt[:, w_lane:w_lane + 1]
        hi = w.astype(BF16).astype(F32)
        side = jnp.where(lane == base, hi, jnp.where(lane == base + 1, w - hi, side))
        side = jnp.where(lane == SLOT_E1 + (base - SLOT_W1) // 2, rt[:, e_lane:e_lane + 1], side)
    xl = dot(perm, jnp.concatenate([h_ref[...], side.astype(BF16)], axis=1))
    half = D_MODEL // 2
    buf_ref[slot, :, :half] = _pack_pair(xl[:, :half], xl[:, half:D_MODEL])
    buf_ref[slot, :, half:] = lax.bitcast_convert_type(xl[:, D_MODEL:], U32)

    def copier(k):
        def make_copy(src_row, dst_row, n):
            return pltpu.make_async_copy(buf_ref.at[k, pl.ds(src_row, n)], xs_ref.at[pl.ds(dst_row, n)],
                                         sem.at[k])
        return make_copy

    _segment_starts(meta_ref, copier(slot))

    @pl.when(i > 0)
    def _():
        _segment_wait(prev_meta_ref, copier(1 - slot), rows)

    @pl.when(i == last)
    def _():
        _segment_wait(meta_ref, copier(slot), rows)
        buf_ref[slot, :MOE_BLK] = jnp.zeros((MOE_BLK, XS_W), U32)
        _tail_copies(tail_ref, copier(slot), lambda cp: cp.start())
        _tail_copies(tail_ref, copier(slot), lambda cp: cp.wait())


def _dispatch(meta, tail, loff, rt, h2, n_pad):
    T, D = h2.shape
    nt = meta.shape[0]
    tm = T // nt
    return pl.pallas_call(
        _dispatch_body,
        grid=(nt,),
        in_specs=[pl.BlockSpec((1, 1, meta.shape[2]), lambda i: (i, 0, 0), memory_space=pltpu.SMEM),
                  pl.BlockSpec((1, 1, meta.shape[2]), lambda i: (jnp.maximum(i - 1, 0), 0, 0),
                               memory_space=pltpu.SMEM),
                  pl.BlockSpec((1, 1, tail.shape[2]), lambda i: (0, 0, 0), memory_space=pltpu.SMEM),
                  pl.BlockSpec((1, 1, LANES), lambda i: (i, 0, 0)),
                  pl.BlockSpec((tm, LANES), lambda i: (i, 0)),
                  pl.BlockSpec((tm, D), lambda i: (i, 0))],
        out_specs=pl.BlockSpec(memory_space=pl.ANY),
        out_shape=jax.ShapeDtypeStruct((n_pad, XS_W), U32),
        scratch_shapes=[pltpu.VMEM((2, _local_rows(tm), XS_W), U32), pltpu.SemaphoreType.DMA((2,))],
        compiler_params=pltpu.CompilerParams(
            dimension_semantics=("arbitrary",), vmem_limit_bytes=VMEM_LIMIT),
        name="moe_dispatch",
    )(meta, meta, tail, loff, rt, h2)


EXPERT_PARTS = 2


def _expert_body(be_ref, nused_ref, xs_ref, wg_ref, wu_ref, wd_ref, ys_ref, wgb_ref, wub_ref, wdb_ref):
    i = pl.program_id(0)

    @pl.when((i == 0) | (be_ref[i] != be_ref[jnp.maximum(i - 1, 0)]))
    def _():
        wgb_ref[...] = wg_ref[0].astype(BF16)
        wub_ref[...] = wu_ref[0].astype(BF16)
        wdb_ref[...] = wd_ref[0].astype(BF16)

    live = i < nused_ref[0]

    @pl.when(jnp.logical_not(live))
    def _():
        ys_ref[...] = jnp.zeros_like(ys_ref)

    @pl.when(live)
    def _():
        dot = functools.partial(jnp.dot, preferred_element_type=F32)
        part = MOE_BLK // EXPERT_PARTS
        half = D_MODEL // 2
        parts = [slice(k * part, (k + 1) * part) for k in range(EXPERT_PARTS)]
        hs = []
        for rows in parts:
            ha, hb = _unpack_pair(xs_ref[rows, :half])
            hs.append(jnp.concatenate([ha, hb], axis=1).astype(BF16))
        gs = [dot(h, wgb_ref[...]) for h in hs]
        us = [dot(h, wub_ref[...]) for h in hs]
        acts = [(g * _sigmoid(g) * u).astype(BF16) for g, u in zip(gs, us)]
        expert = (be_ref[i] % N_EXPERTS).astype(F32)
        for rows, act in zip(parts, acts):
            rec = lax.bitcast_convert_type(xs_ref[rows, half:], F32)
            w = jnp.where(rec[:, SLOT_E1:SLOT_E1 + 1] == expert,
                          rec[:, SLOT_W1:SLOT_W1 + 1] + rec[:, SLOT_W1 + 1:SLOT_W1 + 2],
                          rec[:, SLOT_W2:SLOT_W2 + 1] + rec[:, SLOT_W2 + 1:SLOT_W2 + 2])
            y = dot(act, wdb_ref[...]) * w
            ys_ref[rows] = _pack_pair(y[:, :half], y[:, half:])


def _experts(block_expert, nused, xs, wg, wu, wd):
    n_pad = xs.shape[0]
    blk_in = pl.BlockSpec((MOE_BLK, XS_W), lambda i, be, nu: (jnp.minimum(i, nu[0] - 1), 0))
    blk_out = pl.BlockSpec((MOE_BLK, D_MODEL // 2), lambda i, be, nu: (i, 0))
    wspec = lambda a: pl.BlockSpec((1,) + a.shape[1:], lambda i, be, nu: (be[i], 0, 0))
    return pl.pallas_call(
        _expert_body,
        grid_spec=pltpu.PrefetchScalarGridSpec(
            num_scalar_prefetch=2,
            grid=(n_pad // MOE_BLK,),
            in_specs=[blk_in, wspec(wg), wspec(wu), wspec(wd)],
            out_specs=blk_out,
            scratch_shapes=[pltpu.VMEM(wg.shape[1:], BF16), pltpu.VMEM(wu.shape[1:], BF16),
                            pltpu.VMEM(wd.shape[1:], BF16)],
        ),
        out_shape=jax.ShapeDtypeStruct((n_pad, D_MODEL // 2), U32),
        compiler_params=pltpu.CompilerParams(
            dimension_semantics=("arbitrary",), vmem_limit_bytes=VMEM_LIMIT),
        name="moe_experts",
    )(block_expert, nused, xs, wg, wu, wd)


def _combine_body(meta_ref, next_meta_ref, loff_ref, rt_ref, x_ref, fg_ref, ys_ref, o_ref, buf_ref, sem,
                  *, final_norm):
    i = pl.program_id(0)
    slot = i % 2

    def copier(k):
        def make_copy(loc_row, glob_row, n):
            return pltpu.make_async_copy(ys_ref.at[pl.ds(glob_row, n)], buf_ref.at[k, pl.ds(loc_row, n)],
                                         sem.at[k])
        return make_copy

    rows = buf_ref.shape[1]

    @pl.when(i == 0)
    def _():
        buf_ref[...] = jnp.zeros_like(buf_ref)
        _segment_starts(meta_ref, copier(0))

    @pl.when(i + 1 < pl.num_programs(0))
    def _():
        _segment_starts(next_meta_ref, copier(1 - slot))

    l1, l2 = _local_pos(rt_ref[...], loff_ref[0])
    col = lax.broadcasted_iota(I32, (rt_ref.shape[0], rows), 1)
    pt = _one_hot((col == l1.astype(I32)) | (col == l2.astype(I32)))
    _segment_wait(meta_ref, copier(slot), rows)
    yl = jnp.concatenate(_unpack_pair(buf_ref[slot]), axis=1).astype(BF16)
    out = x_ref[...] + jnp.dot(pt, yl, preferred_element_type=F32)
    if final_norm:
        out = _rms(out) * fg_ref[...]
    o_ref[...] = out


def _combine(meta, loff, rt, x, fg, ys, final_norm):
    T, D = x.shape
    nt = meta.shape[0]
    tm = T // nt
    return pl.pallas_call(
        functools.partial(_combine_body, final_norm=final_norm),
        grid=(nt,),
        in_specs=[pl.BlockSpec((1, 1, meta.shape[2]), lambda i: (i, 0, 0), memory_space=pltpu.SMEM),
                  pl.BlockSpec((1, 1, meta.shape[2]), lambda i: (jnp.minimum(i + 1, nt - 1), 0, 0),
                               memory_space=pltpu.SMEM),
                  pl.BlockSpec((1, 1, LANES), lambda i: (i, 0, 0)),
                  pl.BlockSpec((tm, LANES), lambda i: (i, 0)),
                  pl.BlockSpec((tm, D), lambda i: (i, 0)),
                  pl.BlockSpec((1, D), lambda i: (0, 0)),
                  pl.BlockSpec(memory_space=pl.ANY)],
        out_specs=pl.BlockSpec((tm, D), lambda i: (i, 0)),
        out_shape=jax.ShapeDtypeStruct((T, D), F32),
        scratch_shapes=[pltpu.VMEM((2, _local_rows(tm), D // 2), U32), pltpu.SemaphoreType.DMA((2,))],
        compiler_params=pltpu.CompilerParams(
            dimension_semantics=("arbitrary",), vmem_limit_bytes=VMEM_LIMIT),
        name="moe_combine",
    )(meta, meta, loff, rt, x, fg, ys)


def _rope_tables(L):
    half = HEAD_DIM // 2
    inv = ROPE_THETA ** (-jnp.arange(half, dtype=F32) / half)
    ang = jnp.arange(L, dtype=F32)[:, None] * inv[None, :]
    lane = np.arange(LANES)
    idx = lane % half
    first = jnp.asarray((lane % HEAD_DIM) < half)
    cos = jnp.cos(ang)[:, idx]
    sin = jnp.sin(ang)[:, idx]
    return cos, jnp.where(first, -sin, 0.0), jnp.where(first, 0.0, sin)


def _moe_layout(cnt, T, layer):
    nt = cnt.shape[0] // SUBLANES
    c = cnt.reshape(nt, SUBLANES, LANES)[:, 0, :N_EXPERTS].astype(I32)
    c8 = (c + SUBLANES - 1) // SUBLANES * SUBLANES
    tile_off = jnp.cumsum(c8, axis=0) - c8
    total = jnp.sum(c8, axis=0)
    padded = (total + MOE_BLK - 1) // MOE_BLK * MOE_BLK
    pend = jnp.cumsum(padded)
    dst = (pend - padded)[None, :] + tile_off
    loff = jnp.cumsum(c8, axis=1) - c8
    n8 = c8 // SUBLANES
    per_big = SEG_BIG // SUBLANES
    total8 = jnp.broadcast_to(jnp.sum(n8, axis=1, keepdims=True), n8.shape)
    meta = jnp.concatenate([dst, loff, n8 // per_big, n8 % per_big, total8], axis=1).reshape(nt, 1, META_W)
    loff_f = jnp.pad(loff.astype(F32), ((0, 0), (0, LANES - N_EXPERTS))).reshape(nt, 1, LANES)
    n_pad = 2 * T + nt * N_EXPERTS * SUBLANES + N_EXPERTS * MOE_BLK
    n_pad = (n_pad + MOE_BLK - 1) // MOE_BLK * MOE_BLK
    starts = jnp.arange(n_pad // MOE_BLK, dtype=I32) * MOE_BLK
    block_expert = jnp.minimum(jnp.sum(starts[:, None] >= pend[None, :], axis=1), N_EXPERTS - 1)
    block_expert = block_expert.astype(I32) + layer * N_EXPERTS
    nused = (pend[-1:] // MOE_BLK).astype(I32)
    tail = jnp.concatenate([pend - padded + total, (padded - total) // SUBLANES, nused,
                            jnp.full((1,), n_pad // MOE_BLK, I32)]).reshape(1, 1, 2 * N_EXPERTS + 2)
    return meta, tail, loff_f, block_expert, nused, n_pad


def kernel(x, attn_norm_g, w_in, sink_logits, hgrn_lb_table, hgrn_norm_g, w_up_attn, w_up_hgrn, w_out,
           ffn_norm_g, w_router_group, b_router_group, w_router_expert, b_router_expert,
           w_expert_gate, w_expert_up, w_expert_down, final_norm_g):
    B, L, D = x.shape
    T = B * L
    depth = w_in.shape[0]
    sm = jax.nn.softmax(hgrn_lb_table.astype(F32), axis=0)
    lower_bounds = jnp.cumsum(sm, axis=0) - sm[:1]
    cos, s1, s2 = _rope_tables(L)
    tm_post = min(TM_POST, T)
    ltri = jnp.tril(jnp.ones((tm_post // POST_PARTS,) * 2, BF16), -1)
    fin_g = final_norm_g.reshape(1, D).astype(F32)
    w_in_bf16 = w_in.astype(BF16)
    all_layers = lambda w: w.reshape((depth * N_EXPERTS,) + w.shape[2:])
    wg_all, wu_all, wd_all = all_layers(w_expert_gate), all_layers(w_expert_up), all_layers(w_expert_down)

    for l in range(depth):
        qa, ka, va, qb, sf, sb, ib, og, ga, gb = _inproj(
            x, attn_norm_g[l].reshape(1, D), w_in_bf16, l, cos, s1, s2)
        ya = _attention(qa, ka, va, sink_logits[l].astype(F32) * LOG2E)
        of, orv = _hgrn(qb, sf, sb, ib, lower_bounds[l])

        rw = jnp.concatenate(
            [w_router_group[l], jnp.transpose(w_router_expert[l], (1, 0, 2)).reshape(D, N_EXPERTS)], axis=1)
        rw = jnp.pad(rw.astype(F32), ((0, 0), (0, LANES - rw.shape[1])))
        rwh = rw.astype(BF16)
        rw2 = jnp.concatenate([rwh, (rw - rwh.astype(F32)).astype(BF16)], axis=1)
        rb = jnp.concatenate([b_router_group[l], b_router_expert[l].reshape(-1)]).astype(F32)
        rb = jnp.pad(rb, (0, LANES - rb.shape[0])).reshape(1, LANES)
        flat = lambda a: a.reshape(T, a.shape[-1])
        xn, h2, rt, cnt = _post(
            flat(x), flat(ya), flat(of), flat(orv), flat(og), flat(ga), flat(gb),
            w_up_attn[l].astype(BF16), w_up_hgrn[l].astype(BF16), w_out[l].astype(BF16),
            hgrn_norm_g[l].reshape(1, HGRN_W).astype(F32), ffn_norm_g[l].reshape(1, D).astype(F32),
            rw2, rb, ltri)

        meta, tail, loff, block_expert, nused, n_pad = _moe_layout(cnt, T, l)
        xs = _dispatch(meta, tail, loff, rt, h2, n_pad)
        ys = _experts(block_expert, nused, xs, wg_all, wu_all, wd_all)
        x = _combine(meta, loff, rt, xn, fin_g, ys, l == depth - 1).reshape(B, L, D)
    return x
```

```python
import functools

import numpy as np
import jax
import jax.numpy as jnp
from jax import lax
from jax.experimental import pallas as pl
from jax.experimental.pallas import tpu as pltpu

F32 = jnp.float32
BF16 = jnp.bfloat16
U32 = jnp.uint32
I32 = jnp.int32

D_MODEL = 1024
N_Q_HEADS = 8
N_KV_HEADS = 2
HEAD_DIM = 64
WINDOW = 128
ROPE_THETA = 10000.0
MASK_VALUE = -1e30
HGRN_HEADS = 4
HGRN_DIM = 128
LOGF_MIN = -4.0
F_MIN = float(np.exp(LOGF_MIN))
LOG2E = float(np.log2(np.e))
Q_SCALE = HEAD_DIM ** -0.5 * LOG2E
LB_FLOOR = 1e-30
N_GROUPS = 4
EXPERTS_PER_GROUP = 8
N_EXPERTS = N_GROUPS * EXPERTS_PER_GROUP
D_EXPERT = 512
NORM_EPS = 1e-6

ATTN_Q_W = N_Q_HEADS * HEAD_DIM
ATTN_KV_W = N_KV_HEADS * HEAD_DIM
HGRN_W = HGRN_HEADS * HGRN_DIM
OFF_QA = 0
OFF_KA = OFF_QA + ATTN_Q_W
OFF_VA = OFF_KA + ATTN_KV_W
OFF_QB = OFF_VA + ATTN_KV_W
OFF_ZF = OFF_QB + HGRN_W
OFF_ZB = OFF_ZF + HGRN_W
OFF_IB = OFF_ZB + HGRN_W
OFF_OG = OFF_IB + HGRN_W
OFF_GA = OFF_OG + HGRN_W
OFF_GB = OFF_GA + D_MODEL

LANES = 128
VMEM_LIMIT = 56 * 1024 * 1024

TM_PROJ = 1024
TQ_ATTN = 2048
TH_HGRN = 2048
TM_POST = 512
MOE_BLK = 512


def _sigmoid(z):
    return 1.0 / (1.0 + jnp.exp(-z))


def _rms(x):
    return x * lax.rsqrt(jnp.mean(x * x, axis=-1, keepdims=True) + NORM_EPS)


def _nt(a, b):
    return lax.dot_general(a, b, (((1,), (1,)), ((), ())), preferred_element_type=F32)


def _pack_pair(a, b):
    ua = lax.bitcast_convert_type(a.astype(BF16).astype(F32), U32)
    ub = lax.bitcast_convert_type(b.astype(BF16).astype(F32), U32)
    return ua | (ub >> 16)


def _unpack_pair(u):
    hi = lax.bitcast_convert_type(u & jnp.uint32(0xFFFF0000), F32)
    lo = lax.bitcast_convert_type(u << 16, F32)
    return hi, lo


def _inproj_body(x_ref, g_ref, w_ref, cos_ref, s1_ref, s2_ref,
                 qa_ref, ka_ref, va_ref, qb_ref, sf_ref, sb_ref, ib_ref, og_ref, ga_ref, gb_ref):
    h = (_rms(x_ref[0]) * g_ref[...]).astype(BF16)

    def proj(off, width):
        return jnp.dot(h, w_ref[:, off:off + width], preferred_element_type=F32)

    cos, s1, s2 = cos_ref[...], s1_ref[...], s2_ref[...]

    def rope(t):
        return t * cos + pltpu.roll(t, 96, 1) * s1 + pltpu.roll(t, 32, 1) * s2

    ga_ref[0] = _sigmoid(proj(OFF_GA, D_MODEL)).astype(BF16)
    gb_ref[0] = _sigmoid(proj(OFF_GB, D_MODEL)).astype(BF16)
    t = proj(OFF_OG, HGRN_W)
    og_ref[0] = (t * _sigmoid(t)).astype(BF16)
    t = proj(OFF_QB, HGRN_W)
    qb_ref[0] = (t * _sigmoid(t)).astype(BF16)
    sf_ref[0] = _sigmoid(proj(OFF_ZF, HGRN_W)).astype(BF16)
    sb_ref[0] = _sigmoid(proj(OFF_ZB, HGRN_W)).astype(BF16)
    q = proj(OFF_QA, ATTN_Q_W)
    for c in range(ATTN_Q_W // LANES):
        sl = slice(c * LANES, (c + 1) * LANES)
        qa_ref[0, :, sl] = (rope(q[:, sl]) * Q_SCALE).astype(BF16)
    k = rope(proj(OFF_KA, ATTN_KV_W))
    ka_ref[0, :, :LANES] = k.astype(BF16)
    ka_ref[0, :, LANES:] = pltpu.roll(k, 64, 1).astype(BF16)
    v = proj(OFF_VA, ATTN_KV_W)
    va_ref[0, :, :LANES] = v.astype(BF16)
    va_ref[0, :, LANES:] = pltpu.roll(v, 64, 1).astype(BF16)
    ib_ref[0] = proj(OFF_IB, HGRN_W).astype(BF16)


def _inproj(x, g, w_all, layer, cos, s1, s2):
    B, L, D = x.shape
    tm = min(TM_PROJ, L)
    widths = (ATTN_Q_W, 2 * ATTN_KV_W, 2 * ATTN_KV_W, HGRN_W, HGRN_W, HGRN_W, HGRN_W, HGRN_W,
              D_MODEL, D_MODEL)
    tab = pl.BlockSpec((tm, LANES), lambda b, i: (i, 0))
    return pl.pallas_call(
        _inproj_body,
        grid=(B, L // tm),
        in_specs=[
            pl.BlockSpec((1, tm, D), lambda b, i: (b, i, 0)),
            pl.BlockSpec((1, D), lambda b, i: (0, 0)),
            pl.BlockSpec((None,) + w_all.shape[1:], lambda b, i: (layer, 0, 0), pipeline_mode=pl.Buffered(1)),
            tab, tab, tab,
        ],
        out_specs=[pl.BlockSpec((1, tm, w), lambda b, i: (b, i, 0)) for w in widths],
        out_shape=[jax.ShapeDtypeStruct((B, L, w), BF16) for w in widths],
        compiler_params=pltpu.CompilerParams(
            dimension_semantics=("arbitrary", "arbitrary"), vmem_limit_bytes=VMEM_LIMIT),
        name="inproj",
    )(x, g, w_all, cos, s1, s2)


def _attn_body(sink_ref, q_ref, kp_ref, km_ref, kn_ref, vp_ref, vm_ref, vn_ref, o_ref, *, tq, seq):
    i = pl.program_id(1)
    kw = jnp.concatenate([kp_ref[0], km_ref[0], kn_ref[0]], axis=0)
    vw = jnp.concatenate([vp_ref[0], vm_ref[0], vn_ref[0]], axis=0)
    lo = lax.broadcasted_iota(I32, (1, LANES), 1) < HEAD_DIM
    zero = jnp.zeros((), BF16)

    def variants(t):
        a, b = t[:, :LANES], t[:, LANES:]
        return ((jnp.where(lo, a, zero), jnp.where(lo, zero, b)),
                (jnp.where(lo, b, zero), jnp.where(lo, zero, a)))

    kvar, vvar = variants(kw), variants(vw)
    row = lax.broadcasted_iota(I32, (WINDOW, 3 * WINDOW), 0)
    col = lax.broadcasted_iota(I32, (WINDOW, 3 * WINDOW), 1)
    band = (col >= row) & (col <= row + 2 * WINDOW)
    ncol = ATTN_Q_W // LANES
    group = lambda c: (2 * c) // (N_Q_HEADS // N_KV_HEADS)
    heads = [(c, hh) for c in range(ncol) for hh in range(2)]
    for j in range(tq // WINDOW):
        kpos = col + (i * tq + (j - 1) * WINDOW)
        bias = jnp.where(band & (kpos >= 0) & (kpos < seq), 0.0, MASK_VALUE)
        rows = slice(j * WINDOW, (j + 1) * WINDOW)
        win = slice(j * WINDOW, (j + 3) * WINDOW)
        scores = [_nt(q_ref[0, rows, c * LANES:(c + 1) * LANES], kvar[group(c)][hh][win]) for c, hh in heads]
        probs, denoms = [], []
        for (c, hh), s in zip(heads, scores):
            s = jnp.concatenate([s[:, :WINDOW] + bias[:, :WINDOW], s[:, WINDOW:2 * WINDOW],
                                 s[:, 2 * WINDOW:] + bias[:, 2 * WINDOW:]], axis=1)
            sk = sink_ref[2 * c + hh]
            m = jnp.maximum(jnp.max(s, axis=1, keepdims=True), sk)
            p = jnp.exp2(s - m)
            denoms.append(jnp.sum(p, axis=1, keepdims=True) + jnp.exp2(sk - m))
            probs.append(p.astype(BF16))
        for c in range(ncol):
            g = group(c)
            pv = jnp.dot(jnp.concatenate(probs[2 * c:2 * c + 2], axis=1),
                         jnp.concatenate([vvar[g][0][win], vvar[g][1][win]], axis=0),
                         preferred_element_type=F32)
            inv = jnp.where(lo, 1.0 / denoms[2 * c], 1.0 / denoms[2 * c + 1])
            o_ref[0, rows, c * LANES:(c + 1) * LANES] = (pv * inv).astype(BF16)


def _attention(qa, ka, va, sink):
    B, L, _ = qa.shape
    tq = min(TQ_ATTN, L)
    r = tq // WINDOW
    nb = L // WINDOW
    kvw = 2 * ATTN_KV_W
    main = pl.BlockSpec((1, tq, kvw), lambda b, i: (b, i, 0))
    prev = pl.BlockSpec((1, WINDOW, kvw), lambda b, i: (b, jnp.maximum(i * r - 1, 0), 0))
    nxt = pl.BlockSpec((1, WINDOW, kvw), lambda b, i: (b, jnp.minimum(i * r + r, nb - 1), 0))
    return pl.pallas_call(
        functools.partial(_attn_body, tq=tq, seq=L),
        grid=(B, L // tq),
        in_specs=[
            pl.BlockSpec(memory_space=pltpu.SMEM),
            pl.BlockSpec((1, tq, ATTN_Q_W), lambda b, i: (b, i, 0)),
            prev, main, nxt, prev, main, nxt,
        ],
        out_specs=pl.BlockSpec((1, tq, ATTN_Q_W), lambda b, i: (b, i, 0)),
        out_shape=jax.ShapeDtypeStruct((B, L, ATTN_Q_W), BF16),
        compiler_params=pltpu.CompilerParams(
            dimension_semantics=("arbitrary", "arbitrary"), vmem_limit_bytes=VMEM_LIMIT),
        name="win_attn",
    )(sink, qa, ka, ka, ka, va, va, va)


HB = 128
LEAF = 32
HGRN_LEVELS = tuple(LEAF << k for k in range((HB // LEAF).bit_length() - 1))
LEAF_SHIFT = 0.5 * LEAF * (-LOGF_MIN) * LOG2E


def _hgrn_masks(reverse):
    r = lax.broadcasted_iota(I32, (HB, HB), 0)
    c = lax.broadcasted_iota(I32, (HB, HB), 1)
    if reverse:
        r, c = c, r
    tri = jnp.where(c <= r, 1.0, 0.0).astype(BF16)
    same = lambda w: (r >> (w.bit_length() - 1)) == (c >> (w.bit_length() - 1))
    cross = lambda w: ((r & w) != 0) & ((c & w) == 0)
    lev = jnp.full((HB, HB), len(HGRN_LEVELS), I32)
    for k, w in reversed(list(enumerate(HGRN_LEVELS[:-1]))):
        lev = jnp.where(same(2 * w) & cross(w), k + 1, lev)
    lev = jnp.where(same(LEAF) & (c <= r), 0, lev)
    return tri, lev


def _hgrn_gates(s, lb, lbc, tri):
    f = lbc + (1.0 - lb) * s
    logf = jnp.maximum(jnp.log(f), LOGF_MIN)
    kk = 1.0 - jnp.maximum(f, F_MIN)
    hi = logf.astype(BF16)
    mid = (logf - hi.astype(F32)).astype(BF16)
    dot = functools.partial(jnp.dot, preferred_element_type=F32)
    return kk, (dot(tri, hi) + dot(tri, mid)) * LOG2E


def _hgrn_scores(q, kk, cum, v, st_ref, reverse):
    def ref_row(idx):
        return cum[idx:idx + 1]

    total = ref_row(0) if reverse else ref_row(HB - 1)
    q_blk = q * jnp.exp2(cum)
    k_blk = kk * jnp.exp2(total - cum)

    ql, kl = [], []
    for c in range(HB // LEAF):
        rows = slice(c * LEAF, (c + 1) * LEAF)
        edge = (c + 1) * LEAF if reverse else c * LEAF - 1
        loc = (cum[rows] - ref_row(edge) if 0 <= edge < HB else cum[rows]) + LEAF_SHIFT
        ql.append(q[rows] * jnp.exp2(loc))
        kl.append(kk[rows] * jnp.exp2(-loc))
    pairs = [(jnp.concatenate(ql, axis=0), jnp.concatenate(kl, axis=0))]
    for w in HGRN_LEVELS:
        qp, kp = [], []
        zeros = jnp.zeros((w, HB), F32)
        for s0 in range(0, HB, 2 * w):
            first, second = slice(s0, s0 + w), slice(s0 + w, s0 + 2 * w)
            if reverse:
                ref = ref_row(s0 + w)
                qp += [q[first] * jnp.exp2(cum[first] - ref), zeros]
                kp += [zeros, kk[second] * jnp.exp2(ref - cum[second])]
            else:
                ref = ref_row(s0 + w - 1)
                qp += [zeros, q[second] * jnp.exp2(cum[second] - ref)]
                kp += [kk[first] * jnp.exp2(ref - cum[first]), zeros]
        pairs.append((jnp.concatenate(qp, axis=0), jnp.concatenate(kp, axis=0)))
    scores = [_nt(qq.astype(BF16), kq.astype(BF16)) for qq, kq in pairs]
    st = st_ref[...]
    o_state = _nt(q_blk.astype(BF16), st.astype(BF16))
    st_ref[...] = st * jnp.exp2(total) + lax.dot_general(
        v, k_blk.astype(BF16), (((0,), (0,)), ((), ())), preferred_element_type=F32)
    return scores, o_state


def _hgrn_output(scores, o_state, v, lev):
    a = scores[-1]
    for k in reversed(range(len(scores) - 1)):
        a = jnp.where(lev == k, scores[k], a)
    return jnp.dot(a.astype(BF16), v, preferred_element_type=F32) + o_state


def _hgrn_body(lb_ref, qf_ref, sf_ref, vf_ref, qr_ref, sr_ref, vr_ref, of_ref, or_ref, st_ref, *, th):
    @pl.when(pl.program_id(1) == 0)
    def _():
        st_ref[...] = jnp.zeros_like(st_ref)

    lb = lb_ref[...]
    lbc = jnp.maximum(lb, LB_FLOOR)
    masks = (_hgrn_masks(False), _hgrn_masks(True))
    nblk = th // HB
    streams = ((qf_ref, sf_ref, vf_ref, of_ref), (qr_ref, sr_ref, vr_ref, or_ref))

    def body(t, carry):
        rows = [pl.multiple_of((nblk - 1 - t if d else t) * HB, HB) for d in range(2)]
        units = [(d, h * HGRN_DIM) for d in range(2) for h in range(HGRN_HEADS)]

        def load(d, k, lane0):
            return streams[d][k][0, pl.ds(rows[d], HB), lane0:lane0 + HGRN_DIM]

        gates = [_hgrn_gates(load(d, 1, c0).astype(F32), lb[d:d + 1, c0:c0 + HGRN_DIM],
                             lbc[d:d + 1, c0:c0 + HGRN_DIM], masks[d][0]) for d, c0 in units]
        mids = [_hgrn_scores(load(d, 0, c0).astype(F32), kk, cum, load(d, 2, c0),
                             st_ref.at[d, c0 // HGRN_DIM], bool(d))
                for (d, c0), (kk, cum) in zip(units, gates)]
        for (d, c0), (scores, o_state) in zip(units, mids):
            o = _hgrn_output(scores, o_state, load(d, 2, c0), masks[d][1])
            streams[d][3][0, pl.ds(rows[d], HB), c0:c0 + HGRN_DIM] = o.astype(BF16)
        return carry

    lax.fori_loop(0, nblk, body, 0)


def _hgrn(qb, sf, sb, ib, lb):
    B, L, _ = qb.shape
    th = min(TH_HGRN, L)
    n = L // th
    fwd = pl.BlockSpec((1, th, HGRN_W), lambda b, i: (b, i, 0))
    rev = pl.BlockSpec((1, th, HGRN_W), lambda b, i: (b, n - 1 - i, 0))
    return pl.pallas_call(
        functools.partial(_hgrn_body, th=th),
        grid=(B, n),
        in_specs=[pl.BlockSpec((2, HGRN_W), lambda b, i: (0, 0)), fwd, fwd, fwd, rev, rev, rev],
        out_specs=[fwd, rev],
        out_shape=[jax.ShapeDtypeStruct((B, L, HGRN_W), BF16)] * 2,
        scratch_shapes=[pltpu.VMEM((2, HGRN_HEADS, HGRN_DIM, HGRN_DIM), F32)],
        compiler_params=pltpu.CompilerParams(
            dimension_semantics=("arbitrary", "arbitrary"), vmem_limit_bytes=VMEM_LIMIT),
        name="hgrn2",
    )(lb, qb, sf, ib, qb, sb, ib)


POST_PARTS = 2
POST_TILES_PER_STEP = 2


def _post_mix(rows, x_ref, ya_ref, of_ref, or_ref, og_ref, ga_ref, gb_ref, wua_ref, wuh_ref, wo_ref,
              hg_ref, fg_ref, rw_ref, rb_ref, xn_ref, h2_ref):
    dot = functools.partial(jnp.dot, preferred_element_type=F32)
    o = of_ref[rows].astype(F32) + or_ref[rows].astype(F32)
    yh = (_rms(o) * hg_ref[...] * og_ref[rows].astype(F32)).astype(BF16)
    merged = (ga_ref[rows].astype(F32) * dot(ya_ref[rows], wua_ref[...])
              + gb_ref[rows].astype(F32) * dot(yh, wuh_ref[...]))
    xn = x_ref[rows] + dot(merged.astype(BF16), wo_ref[...])
    xn_ref[rows] = xn
    h2 = _rms(xn) * fg_ref[...]
    hi = h2.astype(BF16)
    h2_ref[rows] = hi
    lo = (h2 - hi.astype(F32)).astype(BF16)
    a = dot(hi, rw_ref[...])
    return a[:, :LANES] + a[:, LANES:] + dot(lo, rw_ref[:, :LANES]) + rb_ref[...]


def _post_route(logits):
    lane = lax.broadcasted_iota(I32, logits.shape, 1)
    lanef = lane.astype(F32)
    big = jnp.float32(1e9)
    ninf = jnp.float32(-jnp.inf)
    red = dict(axis=1, keepdims=True)
    gmask = lane < N_GROUPS
    mg = jnp.max(jnp.where(gmask, logits, ninf), **red)
    p_top = 1.0 / jnp.sum(jnp.where(gmask, jnp.exp(logits - mg), 0.0), **red)
    gi = jnp.min(jnp.where(gmask & (logits == mg), lanef, big), **red)
    emask = ((lane >= N_GROUPS) & (lane < N_GROUPS + N_EXPERTS)
             & (((lane - N_GROUPS) >> 3).astype(F32) == gi))
    le = jnp.where(emask, logits, ninf)
    m1 = jnp.max(le, **red)
    i1 = jnp.min(jnp.where(le == m1, lanef, big), **red)
    le2 = jnp.where(lanef == i1, ninf, le)
    m2 = jnp.max(le2, **red)
    i2 = jnp.min(jnp.where(le2 == m2, lanef, big), **red)
    r = jnp.exp(m2 - m1)
    w1 = 1.0 / (1.0 + r)
    w2 = r * w1
    e1 = i1 - N_GROUPS
    e2 = i2 - N_GROUPS
    is1, is2 = lanef == e1, lanef == e2
    return (e1, e2, p_top * w1, p_top * w2), is1, is2, jnp.where(is1 | is2, 1.0, 0.0)


def _post_body(x_ref, ya_ref, of_ref, or_ref, og_ref, ga_ref, gb_ref, wua_ref, wuh_ref, wo_ref,
               hg_ref, fg_ref, rw_ref, rb_ref, ltri_ref,
               xn_ref, h2_ref, rt_ref, cnt_ref):
    part = ltri_ref.shape[0]
    n_parts = x_ref.shape[0] // part
    parts = [slice(k * part, (k + 1) * part) for k in range(n_parts)]
    logits = [_post_mix(rows, x_ref, ya_ref, of_ref, or_ref, og_ref, ga_ref, gb_ref, wua_ref, wuh_ref,
                        wo_ref, hg_ref, fg_ref, rw_ref, rb_ref, xn_ref, h2_ref) for rows in parts]
    routed = [_post_route(lg) for lg in logits]
    red = dict(axis=1, keepdims=True)
    lane = lax.broadcasted_iota(I32, (part, LANES), 1)
    for k, (rows, ((e1, e2, w1, w2), is1, is2, onehot)) in enumerate(zip(parts, routed)):
        if k % POST_PARTS == 0:
            before = jnp.zeros((1, LANES), F32)
        pref = jnp.dot(ltri_ref[...], onehot.astype(BF16), preferred_element_type=F32) + before
        before = before + jnp.sum(onehot, axis=0, keepdims=True)
        rank1 = jnp.sum(jnp.where(is1, pref, 0.0), **red)
        rank2 = jnp.sum(jnp.where(is2, pref, 0.0), **red)
        rt = jnp.zeros((part, LANES), F32)
        for idx, val in enumerate((e1, e2, rank1, rank2, w1, w2)):
            rt = jnp.where(lane == idx, val, rt)
        rt_ref[rows] = rt
        if k % POST_PARTS == POST_PARTS - 1:
            tile = k // POST_PARTS
            cnt_ref[tile * SUBLANES:(tile + 1) * SUBLANES] = jnp.broadcast_to(before, (SUBLANES, LANES))


RT_E1, RT_E2, RT_RANK1, RT_RANK2, RT_W1, RT_W2 = range(6)
SUBLANES = 8


def _post(x, ya, of, orv, og, ga, gb, wua, wuh, wo, hg, fg, rw, rb, ltri):
    T, D = x.shape
    tile = ltri.shape[0] * POST_PARTS
    tm = min(POST_TILES_PER_STEP * tile, T)
    row = lambda w: pl.BlockSpec((tm, w), lambda i: (i, 0))
    full = lambda a: pl.BlockSpec(a.shape, lambda i: (0,) * a.ndim, pipeline_mode=pl.Buffered(1))
    return pl.pallas_call(
        _post_body,
        grid=(T // tm,),
        in_specs=[row(D), row(ATTN_Q_W), row(HGRN_W), row(HGRN_W), row(HGRN_W), row(D), row(D),
                  full(wua), full(wuh), full(wo), full(hg), full(fg), full(rw), full(rb), full(ltri)],
        out_specs=[row(D), row(D), row(LANES),
                   pl.BlockSpec((tm // tile * SUBLANES, LANES), lambda i: (i, 0))],
        out_shape=[jax.ShapeDtypeStruct((T, D), F32), jax.ShapeDtypeStruct((T, D), BF16),
                   jax.ShapeDtypeStruct((T, LANES), F32),
                   jax.ShapeDtypeStruct((T // tile * SUBLANES, LANES), F32)],
        compiler_params=pltpu.CompilerParams(
            dimension_semantics=("arbitrary",), vmem_limit_bytes=VMEM_LIMIT),
        name="post_mixer_router",
    )(x, ya, of, orv, og, ga, gb, wua, wuh, wo, hg, fg, rw, rb, ltri)


XS_W = D_MODEL // 2 + LANES
SLOT_W1, SLOT_W2, SLOT_E1, SLOT_E2 = 0, 2, 4, 5
SEG_BIG = 32
META_DST, META_LOFF, META_NBIG, META_NSMALL, META_TOTAL8 = (k * N_EXPERTS for k in range(5))
META_W = 5 * N_EXPERTS


def _local_rows(tm):
    return 2 * tm + N_EXPERTS * SUBLANES


def _local_pos(rt, loff_row):
    lanef = lax.broadcasted_iota(I32, rt.shape, 1).astype(F32)
    out = []
    for e_lane, r_lane in ((RT_E1, RT_RANK1), (RT_E2, RT_RANK2)):
        e = rt[:, e_lane:e_lane + 1]
        off = jnp.sum(jnp.where(lanef == e, loff_row, 0.0), axis=1, keepdims=True)
        out.append(off + rt[:, r_lane:r_lane + 1])
    return out


def _one_hot(hit):
    return jnp.where(hit, 1.0, 0.0).astype(BF16)


def _segment_starts(meta_ref, make_copy):
    for e in range(N_EXPERTS):
        dst = meta_ref[0, 0, META_DST + e]
        loff = meta_ref[0, 0, META_LOFF + e]
        n_big = meta_ref[0, 0, META_NBIG + e]
        n_small = meta_ref[0, 0, META_NSMALL + e]

        def big(k, c):
            o = k * SEG_BIG
            make_copy(pl.multiple_of(loff + o, SUBLANES), pl.multiple_of(dst + o, SUBLANES), SEG_BIG).start()
            return c

        lax.fori_loop(0, n_big, big, 0)

        def small(k, c):
            o = n_big * SEG_BIG + k * SUBLANES
            make_copy(pl.multiple_of(loff + o, SUBLANES), pl.multiple_of(dst + o, SUBLANES), SUBLANES).start()
            return c

        lax.fori_loop(0, n_small, small, 0)


def _segment_wait(meta_ref, make_copy, rows):
    total8 = meta_ref[0, 0, META_TOTAL8]
    for bit in range((rows // SUBLANES).bit_length()):
        @pl.when((total8 & (1 << bit)) != 0)
        def _():
            make_copy(0, 0, SUBLANES << bit).wait()


TAIL_PIECES = tuple(SUBLANES << b for b in range((MOE_BLK // SUBLANES).bit_length() - 1))


def _tail_copies(tail_ref, make_copy, action):
    def per_expert(e, c):
        start = tail_ref[0, 0, e]
        n8 = tail_ref[0, 0, N_EXPERTS + e]
        for rows in TAIL_PIECES:
            bit = rows // SUBLANES

            @pl.when((n8 & bit) != 0)
            def _():
                off = (n8 & ~(2 * bit - 1)) * SUBLANES
                action(make_copy(0, pl.multiple_of(start + off, SUBLANES), rows))
        return c

    lax.fori_loop(0, N_EXPERTS, per_expert, 0)

    def per_block(b, c):
        action(make_copy(0, pl.multiple_of(b * MOE_BLK, MOE_BLK), MOE_BLK))
        return c

    lax.fori_loop(tail_ref[0, 0, 2 * N_EXPERTS], tail_ref[0, 0, 2 * N_EXPERTS + 1], per_block, 0)


def _dispatch_body(meta_ref, prev_meta_ref, tail_ref, loff_ref, rt_ref, h_ref, xs_ref, buf_ref, sem):
    i = pl.program_id(0)
    last = pl.num_programs(0) - 1
    slot = i % 2
    tm = rt_ref.shape[0]
    rows = buf_ref.shape[1]
    rt = rt_ref[...]
    dot = functools.partial(jnp.dot, preferred_element_type=F32)
    lane = lax.broadcasted_iota(I32, (tm, LANES), 1)
    l1, l2 = _local_pos(rt, loff_ref[0])
    parts = []
    for pos in (l1, l2):
        hi = jnp.floor(pos * (1.0 / 32.0))
        parts += [hi, pos - 32.0 * hi]
    cols = jnp.zeros((tm, LANES), F32)
    for k, part in enumerate(parts):
        cols = jnp.where(lane == k, part, cols)
    eye = _one_hot(lax.broadcasted_iota(I32, (SUBLANES, LANES), 0) == lax.broadcasted_iota(I32, (SUBLANES, LANES), 1))
    lanes = _nt(eye, cols.astype(BF16))
    r1 = (32.0 * lanes[0:1] + lanes[1:2]).astype(I32)
    r2 = (32.0 * lanes[2:3] + lanes[3:4]).astype(I32)
    row = lax.broadcasted_iota(I32, (rows, tm), 0)
    perm = _one_hot((row == r1) | (row == r2))
    side = jnp.zeros((tm, LANES), F32)
    for w_lane, e_lane, base in ((RT_W1, RT_E1, SLOT_W1), (RT_W2, RT_E2, SLOT_W2)):
        w = rt[:, w_lane:w_lane + 1]
        hi = w.astype(BF16).astype(F32)
        side = jnp.where(lane == base, hi, jnp.where(lane == base + 1, w - hi, side))
        side = jnp.where(lane == SLOT_E1 + (base - SLOT_W1) // 2, rt[:, e_lane:e_lane + 1], side)
    xl = dot(perm, jnp.concatenate([h_ref[...], side.astype(BF16)], axis=1))
    half = D_MODEL // 2
    buf_ref[slot, :, :half] = _pack_pair(xl[:, :half], xl[:, half:D_MODEL])
    buf_ref[slot, :, half:] = lax.bitcast_convert_type(xl[:, D_MODEL:], U32)

    def copier(k):
        def make_copy(src_row, dst_row, n):
            return pltpu.make_async_copy(buf_ref.at[k, pl.ds(src_row, n)], xs_ref.at[pl.ds(dst_row, n)],
                                         sem.at[k])
        return make_copy

    _segment_starts(meta_ref, copier(slot))

    @pl.when(i > 0)
    def _():
        _segment_wait(prev_meta_ref, copier(1 - slot), rows)

    @pl.when(i == last)
    def _():
        _segment_wait(meta_ref, copier(slot), rows)
        buf_ref[slot, :MOE_BLK] = jnp.zeros((MOE_BLK, XS_W), U32)
        _tail_copies(tail_ref, copier(slot), lambda cp: cp.start())
        _tail_copies(tail_ref, copier(slot), lambda cp: cp.wait())


def _dispatch(meta, tail, loff, rt, h2, n_pad):
    T, D = h2.shape
    nt = meta.shape[0]
    tm = T // nt
    return pl.pallas_call(
        _dispatch_body,
        grid=(nt,),
        in_specs=[pl.BlockSpec((1, 1, meta.shape[2]), lambda i: (i, 0, 0), memory_space=pltpu.SMEM),
                  pl.BlockSpec((1, 1, meta.shape[2]), lambda i: (jnp.maximum(i - 1, 0), 0, 0),
                               memory_space=pltpu.SMEM),
                  pl.BlockSpec((1, 1, tail.shape[2]), lambda i: (0, 0, 0), memory_space=pltpu.SMEM),
                  pl.BlockSpec((1, 1, LANES), lambda i: (i, 0, 0)),
                  pl.BlockSpec((tm, LANES), lambda i: (i, 0)),
                  pl.BlockSpec((tm, D), lambda i: (i, 0))],
        out_specs=pl.BlockSpec(memory_space=pl.ANY),
        out_shape=jax.ShapeDtypeStruct((n_pad, XS_W), U32),
        scratch_shapes=[pltpu.VMEM((2, _local_rows(tm), XS_W), U32), pltpu.SemaphoreType.DMA((2,))],
        compiler_params=pltpu.CompilerParams(
            dimension_semantics=("arbitrary",), vmem_limit_bytes=VMEM_LIMIT),
        name="moe_dispatch",
    )(meta, meta, tail, loff, rt, h2)


EXPERT_PARTS = 2


def _expert_body(be_ref, nused_ref, xs_ref, wg_ref, wu_ref, wd_ref, ys_ref, wgb_ref, wub_ref, wdb_ref):
    i = pl.program_id(0)

    @pl.when((i == 0) | (be_ref[i] != be_ref[jnp.maximum(i - 1, 0)]))
    def _():
        wgb_ref[...] = wg_ref[0].astype(BF16)
        wub_ref[...] = wu_ref[0].astype(BF16)
        wdb_ref[...] = wd_ref[0].astype(BF16)

    live = i < nused_ref[0]

    @pl.when(jnp.logical_not(live))
    def _():
        ys_ref[...] = jnp.zeros_like(ys_ref)

    @pl.when(live)
    def _():
        dot = functools.partial(jnp.dot, preferred_element_type=F32)
        part = MOE_BLK // EXPERT_PARTS
        half = D_MODEL // 2
        parts = [slice(k * part, (k + 1) * part) for k in range(EXPERT_PARTS)]
        hs = []
        for rows in parts:
            ha, hb = _unpack_pair(xs_ref[rows, :half])
            hs.append(jnp.concatenate([ha, hb], axis=1).astype(BF16))
        gs = [dot(h, wgb_ref[...]) for h in hs]
        us = [dot(h, wub_ref[...]) for h in hs]
        acts = [(g * _sigmoid(g) * u).astype(BF16) for g, u in zip(gs, us)]
        expert = (be_ref[i] % N_EXPERTS).astype(F32)
        for rows, act in zip(parts, acts):
            rec = lax.bitcast_convert_type(xs_ref[rows, half:], F32)
            w = jnp.where(rec[:, SLOT_E1:SLOT_E1 + 1] == expert,
                          rec[:, SLOT_W1:SLOT_W1 + 1] + rec[:, SLOT_W1 + 1:SLOT_W1 + 2],
                          rec[:, SLOT_W2:SLOT_W2 + 1] + rec[:, SLOT_W2 + 1:SLOT_W2 + 2])
            y = dot(act, wdb_ref[...]) * w
            ys_ref[rows] = _pack_pair(y[:, :half], y[:, half:])


def _experts(block_expert, nused, xs, wg, wu, wd):
    n_pad = xs.shape[0]
    blk_in = pl.BlockSpec((MOE_BLK, XS_W), lambda i, be, nu: (jnp.minimum(i, nu[0] - 1), 0))
    blk_out = pl.BlockSpec((MOE_BLK, D_MODEL // 2), lambda i, be, nu: (i, 0))
    wspec = lambda a: pl.BlockSpec((1,) + a.shape[1:], lambda i, be, nu: (be[i], 0, 0))
    return pl.pallas_call(
        _expert_body,
        grid_spec=pltpu.PrefetchScalarGridSpec(
            num_scalar_prefetch=2,
            grid=(n_pad // MOE_BLK,),
            in_specs=[blk_in, wspec(wg), wspec(wu), wspec(wd)],
            out_specs=blk_out,
            scratch_shapes=[pltpu.VMEM(wg.shape[1:], BF16), pltpu.VMEM(wu.shape[1:], BF16),
                            pltpu.VMEM(wd.shape[1:], BF16)],
        ),
        out_shape=jax.ShapeDtypeStruct((n_pad, D_MODEL // 2), U32),
        compiler_params=pltpu.CompilerParams(
            dimension_semantics=("arbitrary",), vmem_limit_bytes=VMEM_LIMIT),
        name="moe_experts",
    )(block_expert, nused, xs, wg, wu, wd)


def _combine_body(meta_ref, next_meta_ref, loff_ref, rt_ref, x_ref, fg_ref, ys_ref, o_ref, buf_ref, sem,
                  *, final_norm):
    i = pl.program_id(0)
    slot = i % 2

    def copier(k):
        def make_copy(loc_row, glob_row, n):
            return pltpu.make_async_copy(ys_ref.at[pl.ds(glob_row, n)], buf_ref.at[k, pl.ds(loc_row, n)],
                                         sem.at[k])
        return make_copy

    rows = buf_ref.shape[1]

    @pl.when(i == 0)
    def _():
        buf_ref[...] = jnp.zeros_like(buf_ref)
        _segment_starts(meta_ref, copier(0))

    @pl.when(i + 1 < pl.num_programs(0))
    def _():
        _segment_starts(next_meta_ref, copier(1 - slot))

    l1, l2 = _local_pos(rt_ref[...], loff_ref[0])
    col = lax.broadcasted_iota(I32, (rt_ref.shape[0], rows), 1)
    pt = _one_hot((col == l1.astype(I32)) | (col == l2.astype(I32)))
    _segment_wait(meta_ref, copier(slot), rows)
    yl = jnp.concatenate(_unpack_pair(buf_ref[slot]), axis=1).astype(BF16)
    out = x_ref[...] + jnp.dot(pt, yl, preferred_element_type=F32)
    if final_norm:
        out = _rms(out) * fg_ref[...]
    o_ref[...] = out


def _combine(meta, loff, rt, x, fg, ys, final_norm):
    T, D = x.shape
    nt = meta.shape[0]
    tm = T // nt
    return pl.pallas_call(
        functools.partial(_combine_body, final_norm=final_norm),
        grid=(nt,),
        in_specs=[pl.BlockSpec((1, 1, meta.shape[2]), lambda i: (i, 0, 0), memory_space=pltpu.SMEM),
                  pl.BlockSpec((1, 1, meta.shape[2]), lambda i: (jnp.minimum(i + 1, nt - 1), 0, 0),
                               memory_space=pltpu.SMEM),
                  pl.BlockSpec((1, 1, LANES), lambda i: (i, 0, 0)),
                  pl.BlockSpec((tm, LANES), lambda i: (i, 0)),
                  pl.BlockSpec((tm, D), lambda i: (i, 0)),
                  pl.BlockSpec((1, D), lambda i: (0, 0)),
                  pl.BlockSpec(memory_space=pl.ANY)],
        out_specs=pl.BlockSpec((tm, D), lambda i: (i, 0)),
        out_shape=jax.ShapeDtypeStruct((T, D), F32),
        scratch_shapes=[pltpu.VMEM((2, _local_rows(tm), D // 2), U32), pltpu.SemaphoreType.DMA((2,))],
        compiler_params=pltpu.CompilerParams(
            dimension_semantics=("arbitrary",), vmem_limit_bytes=VMEM_LIMIT),
        name="moe_combine",
    )(meta, meta, loff, rt, x, fg, ys)


def _rope_tables(L):
    half = HEAD_DIM // 2
    inv = ROPE_THETA ** (-jnp.arange(half, dtype=F32) / half)
    ang = jnp.arange(L, dtype=F32)[:, None] * inv[None, :]
    lane = np.arange(LANES)
    idx = lane % half
    first = jnp.asarray((lane % HEAD_DIM) < half)
    cos = jnp.cos(ang)[:, idx]
    sin = jnp.sin(ang)[:, idx]
    return cos, jnp.where(first, -sin, 0.0), jnp.where(first, 0.0, sin)


def _moe_layout(cnt, T, layer):
    nt = cnt.shape[0] // SUBLANES
    c = cnt.reshape(nt, SUBLANES, LANES)[:, 0, :N_EXPERTS].astype(I32)
    c8 = (c + SUBLANES - 1) // SUBLANES * SUBLANES
    tile_off = jnp.cumsum(c8, axis=0) - c8
    total = jnp.sum(c8, axis=0)
    padded = (total + MOE_BLK - 1) // MOE_BLK * MOE_BLK
    pend = jnp.cumsum(padded)
    dst = (pend - padded)[None, :] + tile_off
    loff = jnp.cumsum(c8, axis=1) - c8
    n8 = c8 // SUBLANES
    per_big = SEG_BIG // SUBLANES
    total8 = jnp.broadcast_to(jnp.sum(n8, axis=1, keepdims=True), n8.shape)
    meta = jnp.concatenate([dst, loff, n8 // per_big, n8 % per_big, total8], axis=1).reshape(nt, 1, META_W)
    loff_f = jnp.pad(loff.astype(F32), ((0, 0), (0, LANES - N_EXPERTS))).reshape(nt, 1, LANES)
    n_pad = 2 * T + nt * N_EXPERTS * SUBLANES + N_EXPERTS * MOE_BLK
    n_pad = (n_pad + MOE_BLK - 1) // MOE_BLK * MOE_BLK
    starts = jnp.arange(n_pad // MOE_BLK, dtype=I32) * MOE_BLK
    block_expert = jnp.minimum(jnp.sum(starts[:, None] >= pend[None, :], axis=1), N_EXPERTS - 1)
    block_expert = block_expert.astype(I32) + layer * N_EXPERTS
    nused = (pend[-1:] // MOE_BLK).astype(I32)
    tail = jnp.concatenate([pend - padded + total, (padded - total) // SUBLANES, nused,
                            jnp.full((1,), n_pad // MOE_BLK, I32)]).reshape(1, 1, 2 * N_EXPERTS + 2)
    return meta, tail, loff_f, block_expert, nused, n_pad


def kernel(x, attn_norm_g, w_in, sink_logits, hgrn_lb_table, hgrn_norm_g, w_up_attn, w_up_hgrn, w_out,
           ffn_norm_g, w_router_group, b_router_group, w_router_expert, b_router_expert,
           w_expert_gate, w_expert_up, w_expert_down, final_norm_g):
    B, L, D = x.shape
    T = B * L
    depth = w_in.shape[0]
    sm = jax.nn.softmax(hgrn_lb_table.astype(F32), axis=0)
    lower_bounds = jnp.cumsum(sm, axis=0) - sm[:1]
    cos, s1, s2 = _rope_tables(L)
    tm_post = min(TM_POST, T)
    ltri = jnp.tril(jnp.ones((tm_post // POST_PARTS,) * 2, BF16), -1)
    fin_g = final_norm_g.reshape(1, D).astype(F32)
    w_in_bf16 = w_in.astype(BF16)
    all_layers = lambda w: w.reshape((depth * N_EXPERTS,) + w.shape[2:])
    wg_all, wu_all, wd_all = all_layers(w_expert_gate), all_layers(w_expert_up), all_layers(w_expert_down)

    for l in range(depth):
        qa, ka, va, qb, sf, sb, ib, og, ga, gb = _inproj(
            x, attn_norm_g[l].reshape(1, D), w_in_bf16, l, cos, s1, s2)
        ya = _attention(qa, ka, va, sink_logits[l].astype(F32) * LOG2E)
        of, orv = _hgrn(qb, sf, sb, ib, lower_bounds[l])

        rw = jnp.concatenate(
            [w_router_group[l], jnp.transpose(w_router_expert[l], (1, 0, 2)).reshape(D, N_EXPERTS)], axis=1)
        rw = jnp.pad(rw.astype(F32), ((0, 0), (0, LANES - rw.shape[1])))
        rwh = rw.astype(BF16)
        rw2 = jnp.concatenate([rwh, (rw - rwh.astype(F32)).astype(BF16)], axis=1)
        rb = jnp.concatenate([b_router_group[l], b_router_expert[l].reshape(-1)]).astype(F32)
        rb = jnp.pad(rb, (0, LANES - rb.shape[0])).reshape(1, LANES)
        flat = lambda a: a.reshape(T, a.shape[-1])
        xn, h2, rt, cnt = _post(
            flat(x), flat(ya), flat(of), flat(orv), flat(og), flat(ga), flat(gb),
            w_up_attn[l].astype(BF16), w_up_hgrn[l].astype(BF16), w_out[l].astype(BF16),
            hgrn_norm_g[l].reshape(1, HGRN_W).astype(F32), ffn_norm_g[l].reshape(1, D).astype(F32),
            rw2, rb, ltri)

        meta, tail, loff, block_expert, nused, n_pad = _moe_layout(cnt, T, l)
        xs = _dispatch(meta, tail, loff, rt, h2, n_pad)
        ys = _experts(block_expert, nused, xs, wg_all, wu_all, wd_all)
        x = _combine(meta, loff, rt, xn, fin_g, ys, l == depth - 1).reshape(B, L, D)
    return x
```

```python
import functools

import numpy as np
import jax
import jax.numpy as jnp
from jax import lax
from jax.experimental import pallas as pl
from jax.experimental.pallas import tpu as pltpu

F32 = jnp.float32
BF16 = jnp.bfloat16
U32 = jnp.uint32
I32 = jnp.int32

D_MODEL = 1024
N_Q_HEADS = 8
N_KV_HEADS = 2
HEAD_DIM = 64
WINDOW = 128
ROPE_THETA = 10000.0
MASK_VALUE = -1e30
HGRN_HEADS = 4
HGRN_DIM = 128
LOGF_MIN = -4.0
F_MIN = float(np.exp(LOGF_MIN))
LOG2E = float(np.log2(np.e))
Q_SCALE = HEAD_DIM ** -0.5 * LOG2E
LB_FLOOR = 1e-30
N_GROUPS = 4
EXPERTS_PER_GROUP = 8
N_EXPERTS = N_GROUPS * EXPERTS_PER_GROUP
D_EXPERT = 512
NORM_EPS = 1e-6

ATTN_Q_W = N_Q_HEADS * HEAD_DIM
ATTN_KV_W = N_KV_HEADS * HEAD_DIM
HGRN_W = HGRN_HEADS * HGRN_DIM
OFF_QA = 0
OFF_KA = OFF_QA + ATTN_Q_W
OFF_VA = OFF_KA + ATTN_KV_W
OFF_QB = OFF_VA + ATTN_KV_W
OFF_ZF = OFF_QB + HGRN_W
OFF_ZB = OFF_ZF + HGRN_W
OFF_IB = OFF_ZB + HGRN_W
OFF_OG = OFF_IB + HGRN_W
OFF_GA = OFF_OG + HGRN_W
OFF_GB = OFF_GA + D_MODEL

LANES = 128
VMEM_LIMIT = 56 * 1024 * 1024

TM_PROJ = 1024
TQ_ATTN = 2048
TH_HGRN = 2048
TM_POST = 512
MOE_BLK = 512


def _sigmoid(z):
    return 1.0 / (1.0 + jnp.exp(-z))


def _rms(x):
    return x * lax.rsqrt(jnp.mean(x * x, axis=-1, keepdims=True) + NORM_EPS)


def _nt(a, b):
    return lax.dot_general(a, b, (((1,), (1,)), ((), ())), preferred_element_type=F32)


def _pack_pair(a, b):
    ua = lax.bitcast_convert_type(a.astype(BF16).astype(F32), U32)
    ub = lax.bitcast_convert_type(b.astype(BF16).astype(F32), U32)
    return ua | (ub >> 16)


def _unpack_pair(u):
    hi = lax.bitcast_convert_type(u & jnp.uint32(0xFFFF0000), F32)
    lo = lax.bitcast_convert_type(u << 16, F32)
    return hi, lo


def _inproj_body(x_ref, g_ref, w_ref, cos_ref, s1_ref, s2_ref,
                 qa_ref, ka_ref, va_ref, qb_ref, sf_ref, sb_ref, ib_ref, og_ref, ga_ref, gb_ref):
    h = (_rms(x_ref[0]) * g_ref[...]).astype(BF16)

    def proj(off, width):
        return jnp.dot(h, w_ref[:, off:off + width], preferred_element_type=F32)

    cos, s1, s2 = cos_ref[...], s1_ref[...], s2_ref[...]

    def rope(t):
        return t * cos + pltpu.roll(t, 96, 1) * s1 + pltpu.roll(t, 32, 1) * s2

    ga_ref[0] = _sigmoid(proj(OFF_GA, D_MODEL)).astype(BF16)
    gb_ref[0] = _sigmoid(proj(OFF_GB, D_MODEL)).astype(BF16)
    t = proj(OFF_OG, HGRN_W)
    og_ref[0] = (t * _sigmoid(t)).astype(BF16)
    t = proj(OFF_QB, HGRN_W)
    qb_ref[0] = (t * _sigmoid(t)).astype(BF16)
    sf_ref[0] = _sigmoid(proj(OFF_ZF, HGRN_W)).astype(BF16)
    sb_ref[0] = _sigmoid(proj(OFF_ZB, HGRN_W)).astype(BF16)
    q = proj(OFF_QA, ATTN_Q_W)
    for c in range(ATTN_Q_W // LANES):
        sl = slice(c * LANES, (c + 1) * LANES)
        qa_ref[0, :, sl] = (rope(q[:, sl]) * Q_SCALE).astype(BF16)
    k = rope(proj(OFF_KA, ATTN_KV_W))
    ka_ref[0, :, :LANES] = k.astype(BF16)
    ka_ref[0, :, LANES:] = pltpu.roll(k, 64, 1).astype(BF16)
    v = proj(OFF_VA, ATTN_KV_W)
    va_ref[0, :, :LANES] = v.astype(BF16)
    va_ref[0, :, LANES:] = pltpu.roll(v, 64, 1).astype(BF16)
    ib_ref[0] = proj(OFF_IB, HGRN_W).astype(BF16)


def _inproj(x, g, w_all, layer, cos, s1, s2):
    B, L, D = x.shape
    tm = min(TM_PROJ, L)
    widths = (ATTN_Q_W, 2 * ATTN_KV_W, 2 * ATTN_KV_W, HGRN_W, HGRN_W, HGRN_W, HGRN_W, HGRN_W,
              D_MODEL, D_MODEL)
    tab = pl.BlockSpec((tm, LANES), lambda b, i: (i, 0))
    return pl.pallas_call(
        _inproj_body,
        grid=(B, L // tm),
        in_specs=[
            pl.BlockSpec((1, tm, D), lambda b, i: (b, i, 0)),
            pl.BlockSpec((1, D), lambda b, i: (0, 0)),
            pl.BlockSpec((None,) + w_all.shape[1:], lambda b, i: (layer, 0, 0), pipeline_mode=pl.Buffered(1)),
            tab, tab, tab,
        ],
        out_specs=[pl.BlockSpec((1, tm, w), lambda b, i: (b, i, 0)) for w in widths],
        out_shape=[jax.ShapeDtypeStruct((B, L, w), BF16) for w in widths],
        compiler_params=pltpu.CompilerParams(
            dimension_semantics=("arbitrary", "arbitrary"), vmem_limit_bytes=VMEM_LIMIT),
        name="inproj",
    )(x, g, w_all, cos, s1, s2)


def _attn_body(sink_ref, q_ref, kp_ref, km_ref, kn_ref, vp_ref, vm_ref, vn_ref, o_ref, *, tq, seq):
    i = pl.program_id(1)
    kw = jnp.concatenate([kp_ref[0], km_ref[0], kn_ref[0]], axis=0)
    vw = jnp.concatenate([vp_ref[0], vm_ref[0], vn_ref[0]], axis=0)
    lo = lax.broadcasted_iota(I32, (1, LANES), 1) < HEAD_DIM
    zero = jnp.zeros((), BF16)

    def variants(t):
        a, b = t[:, :LANES], t[:, LANES:]
        return ((jnp.where(lo, a, zero), jnp.where(lo, zero, b)),
                (jnp.where(lo, b, zero), jnp.where(lo, zero, a)))

    kvar, vvar = variants(kw), variants(vw)
    row = lax.broadcasted_iota(I32, (WINDOW, 3 * WINDOW), 0)
    col = lax.broadcasted_iota(I32, (WINDOW, 3 * WINDOW), 1)
    band = (col >= row) & (col <= row + 2 * WINDOW)
    ncol = ATTN_Q_W // LANES
    group = lambda c: (2 * c) // (N_Q_HEADS // N_KV_HEADS)
    heads = [(c, hh) for c in range(ncol) for hh in range(2)]
    for j in range(tq // WINDOW):
        kpos = col + (i * tq + (j - 1) * WINDOW)
        bias = jnp.where(band & (kpos >= 0) & (kpos < seq), 0.0, MASK_VALUE)
        rows = slice(j * WINDOW, (j + 1) * WINDOW)
        win = slice(j * WINDOW, (j + 3) * WINDOW)
        scores = [_nt(q_ref[0, rows, c * LANES:(c + 1) * LANES], kvar[group(c)][hh][win]) for c, hh in heads]
        probs, denoms = [], []
        for (c, hh), s in zip(heads, scores):
            s = jnp.concatenate([s[:, :WINDOW] + bias[:, :WINDOW], s[:, WINDOW:2 * WINDOW],
                                 s[:, 2 * WINDOW:] + bias[:, 2 * WINDOW:]], axis=1)
            sk = sink_ref[2 * c + hh]
            m = jnp.maximum(jnp.max(s, axis=1, keepdims=True), sk)
            p = jnp.exp2(s - m)
            denoms.append(jnp.sum(p, axis=1, keepdims=True) + jnp.exp2(sk - m))
            probs.append(p.astype(BF16))
        for c in range(ncol):
            g = group(c)
            pv = jnp.dot(jnp.concatenate(probs[2 * c:2 * c + 2], axis=1),
                         jnp.concatenate([vvar[g][0][win], vvar[g][1][win]], axis=0),
                         preferred_element_type=F32)
            inv = jnp.where(lo, 1.0 / denoms[2 * c], 1.0 / denoms[2 * c + 1])
            o_ref[0, rows, c * LANES:(c + 1) * LANES] = (pv * inv).astype(BF16)


def _attention(qa, ka, va, sink):
    B, L, _ = qa.shape
    tq = min(TQ_ATTN, L)
    r = tq // WINDOW
    nb = L // WINDOW
    kvw = 2 * ATTN_KV_W
    main = pl.BlockSpec((1, tq, kvw), lambda b, i: (b, i, 0))
    prev = pl.BlockSpec((1, WINDOW, kvw), lambda b, i: (b, jnp.maximum(i * r - 1, 0), 0))
    nxt = pl.BlockSpec((1, WINDOW, kvw), lambda b, i: (b, jnp.minimum(i * r + r, nb - 1), 0))
    return pl.pallas_call(
        functools.partial(_attn_body, tq=tq, seq=L),
        grid=(B, L // tq),
        in_specs=[
            pl.BlockSpec(memory_space=pltpu.SMEM),
            pl.BlockSpec((1, tq, ATTN_Q_W), lambda b, i: (b, i, 0)),
            prev, main, nxt, prev, main, nxt,
        ],
        out_specs=pl.BlockSpec((1, tq, ATTN_Q_W), lambda b, i: (b, i, 0)),
        out_shape=jax.ShapeDtypeStruct((B, L, ATTN_Q_W), BF16),
        compiler_params=pltpu.CompilerParams(
            dimension_semantics=("arbitrary", "arbitrary"), vmem_limit_bytes=VMEM_LIMIT),
        name="win_attn",
    )(sink, qa, ka, ka, ka, va, va, va)


HB = 128
LEAF = 32
HGRN_LEVELS = tuple(LEAF << k for k in range((HB // LEAF).bit_length() - 1))
LEAF_SHIFT = 0.5 * LEAF * (-LOGF_MIN) * LOG2E


def _hgrn_masks(reverse):
    r = lax.broadcasted_iota(I32, (HB, HB), 0)
    c = lax.broadcasted_iota(I32, (HB, HB), 1)
    if reverse:
        r, c = c, r
    tri = jnp.where(c <= r, 1.0, 0.0).astype(BF16)
    same = lambda w: (r >> (w.bit_length() - 1)) == (c >> (w.bit_length() - 1))
    cross = lambda w: ((r & w) != 0) & ((c & w) == 0)
    lev = jnp.full((HB, HB), len(HGRN_LEVELS), I32)
    for k, w in reversed(list(enumerate(HGRN_LEVELS[:-1]))):
        lev = jnp.where(same(2 * w) & cross(w), k + 1, lev)
    lev = jnp.where(same(LEAF) & (c <= r), 0, lev)
    return tri, lev


def _hgrn_gates(s, lb, lbc, tri):
    f = lbc + (1.0 - lb) * s
    logf = jnp.maximum(jnp.log(f), LOGF_MIN)
    kk = 1.0 - jnp.maximum(f, F_MIN)
    hi = logf.astype(BF16)
    mid = (logf - hi.astype(F32)).astype(BF16)
    dot = functools.partial(jnp.dot, preferred_element_type=F32)
    return kk, (dot(tri, hi) + dot(tri, mid)) * LOG2E


def _hgrn_scores(q, kk, cum, v, st_ref, reverse):
    def ref_row(idx):
        return cum[idx:idx + 1]

    total = ref_row(0) if reverse else ref_row(HB - 1)
    q_blk = q * jnp.exp2(cum)
    k_blk = kk * jnp.exp2(total - cum)

    ql, kl = [], []
    for c in range(HB // LEAF):
        rows = slice(c * LEAF, (c + 1) * LEAF)
        edge = (c + 1) * LEAF if reverse else c * LEAF - 1
        loc = (cum[rows] - ref_row(edge) if 0 <= edge < HB else cum[rows]) + LEAF_SHIFT
        ql.append(q[rows] * jnp.exp2(loc))
        kl.append(kk[rows] * jnp.exp2(-loc))
    pairs = [(jnp.concatenate(ql, axis=0), jnp.concatenate(kl, axis=0))]
    for w in HGRN_LEVELS:
        qp, kp = [], []
        zeros = jnp.zeros((w, HB), F32)
        for s0 in range(0, HB, 2 * w):
            first, second = slice(s0, s0 + w), slice(s0 + w, s0 + 2 * w)
            if reverse:
                ref = ref_row(s0 + w)
                qp += [q[first] * jnp.exp2(cum[first] - ref), zeros]
                kp += [zeros, kk[second] * jnp.exp2(ref - cum[second])]
            else:
                ref = ref_row(s0 + w - 1)
                qp += [zeros, q[second] * jnp.exp2(cum[second] - ref)]
                kp += [kk[first] * jnp.exp2(ref - cum[first]), zeros]
        pairs.append((jnp.concatenate(qp, axis=0), jnp.concatenate(kp, axis=0)))
    scores = [_nt(qq.astype(BF16), kq.astype(BF16)) for qq, kq in pairs]
    st = st_ref[...]
    o_state = _nt(q_blk.astype(BF16), st.astype(BF16))
    st_ref[...] = st * jnp.exp2(total) + lax.dot_general(
        v, k_blk.astype(BF16), (((0,), (0,)), ((), ())), preferred_element_type=F32)
    return scores, o_state


def _hgrn_output(scores, o_state, v, lev):
    a = scores[-1]
    for k in reversed(range(len(scores) - 1)):
        a = jnp.where(lev == k, scores[k], a)
    return jnp.dot(a.astype(BF16), v, preferred_element_type=F32) + o_state


def _hgrn_body(lb_ref, qf_ref, sf_ref, vf_ref, qr_ref, sr_ref, vr_ref, of_ref, or_ref, st_ref, *, th):
    @pl.when(pl.program_id(1) == 0)
    def _():
        st_ref[...] = jnp.zeros_like(st_ref)

    lb = lb_ref[...]
    lbc = jnp.maximum(lb, LB_FLOOR)
    masks = (_hgrn_masks(False), _hgrn_masks(True))
    nblk = th // HB
    streams = ((qf_ref, sf_ref, vf_ref, of_ref), (qr_ref, sr_ref, vr_ref, or_ref))

    def body(t, carry):
        rows = [pl.multiple_of((nblk - 1 - t if d else t) * HB, HB) for d in range(2)]
        units = [(d, h * HGRN_DIM) for d in range(2) for h in range(HGRN_HEADS)]

        def load(d, k, lane0):
            return streams[d][k][0, pl.ds(rows[d], HB), lane0:lane0 + HGRN_DIM]

        gates = [_hgrn_gates(load(d, 1, c0).astype(F32), lb[d:d + 1, c0:c0 + HGRN_DIM],
                             lbc[d:d + 1, c0:c0 + HGRN_DIM], masks[d][0]) for d, c0 in units]
        mids = [_hgrn_scores(load(d, 0, c0).astype(F32), kk, cum, load(d, 2, c0),
                             st_ref.at[d, c0 // HGRN_DIM], bool(d))
                for (d, c0), (kk, cum) in zip(units, gates)]
        for (d, c0), (scores, o_state) in zip(units, mids):
            o = _hgrn_output(scores, o_state, load(d, 2, c0), masks[d][1])
            streams[d][3][0, pl.ds(rows[d], HB), c0:c0 + HGRN_DIM] = o.astype(BF16)
        return carry

    lax.fori_loop(0, nblk, body, 0)


def _hgrn(qb, sf, sb, ib, lb):
    B, L, _ = qb.shape
    th = min(TH_HGRN, L)
    n = L // th
    fwd = pl.BlockSpec((1, th, HGRN_W), lambda b, i: (b, i, 0))
    rev = pl.BlockSpec((1, th, HGRN_W), lambda b, i: (b, n - 1 - i, 0))
    return pl.pallas_call(
        functools.partial(_hgrn_body, th=th),
        grid=(B, n),
        in_specs=[pl.BlockSpec((2, HGRN_W), lambda b, i: (0, 0)), fwd, fwd, fwd, rev, rev, rev],
        out_specs=[fwd, rev],
        out_shape=[jax.ShapeDtypeStruct((B, L, HGRN_W), BF16)] * 2,
        scratch_shapes=[pltpu.VMEM((2, HGRN_HEADS, HGRN_DIM, HGRN_DIM), F32)],
        compiler_params=pltpu.CompilerParams(
            dimension_semantics=("arbitrary", "arbitrary"), vmem_limit_bytes=VMEM_LIMIT),
        name="hgrn2",
    )(lb, qb, sf, ib, qb, sb, ib)


POST_PARTS = 2
POST_TILES_PER_STEP = 2


def _post_mix(rows, x_ref, ya_ref, of_ref, or_ref, og_ref, ga_ref, gb_ref, wua_ref, wuh_ref, wo_ref,
              hg_ref, fg_ref, rw_ref, rb_ref, xn_ref, h2_ref):
    dot = functools.partial(jnp.dot, preferred_element_type=F32)
    o = of_ref[rows].astype(F32) + or_ref[rows].astype(F32)
    yh = (_rms(o) * hg_ref[...] * og_ref[rows].astype(F32)).astype(BF16)
    merged = (ga_ref[rows].astype(F32) * dot(ya_ref[rows], wua_ref[...])
              + gb_ref[rows].astype(F32) * dot(yh, wuh_ref[...]))
    xn = x_ref[rows] + dot(merged.astype(BF16), wo_ref[...])
    xn_ref[rows] = xn
    h2 = _rms(xn) * fg_ref[...]
    hi = h2.astype(BF16)
    h2_ref[rows] = hi
    lo = (h2 - hi.astype(F32)).astype(BF16)
    a = dot(hi, rw_ref[...])
    return a[:, :LANES] + a[:, LANES:] + dot(lo, rw_ref[:, :LANES]) + rb_ref[...]


def _post_route(logits):
    lane = lax.broadcasted_iota(I32, logits.shape, 1)
    lanef = lane.astype(F32)
    big = jnp.float32(1e9)
    ninf = jnp.float32(-jnp.inf)
    red = dict(axis=1, keepdims=True)
    gmask = lane < N_GROUPS
    mg = jnp.max(jnp.where(gmask, logits, ninf), **red)
    p_top = 1.0 / jnp.sum(jnp.where(gmask, jnp.exp(logits - mg), 0.0), **red)
    gi = jnp.min(jnp.where(gmask & (logits == mg), lanef, big), **red)
    emask = ((lane >= N_GROUPS) & (lane < N_GROUPS + N_EXPERTS)
             & (((lane - N_GROUPS) >> 3).astype(F32) == gi))
    le = jnp.where(emask, logits, ninf)
    m1 = jnp.max(le, **red)
    i1 = jnp.min(jnp.where(le == m1, lanef, big), **red)
    le2 = jnp.where(lanef == i1, ninf, le)
    m2 = jnp.max(le2, **red)
    i2 = jnp.min(jnp.where(le2 == m2, lanef, big), **red)
    r = jnp.exp(m2 - m1)
    w1 = 1.0 / (1.0 + r)
    w2 = r * w1
    e1 = i1 - N_GROUPS
    e2 = i2 - N_GROUPS
    is1, is2 = lanef == e1, lanef == e2
    return (e1, e2, p_top * w1, p_top * w2), is1, is2, jnp.where(is1 | is2, 1.0, 0.0)


def _post_body(x_ref, ya_ref, of_ref, or_ref, og_ref, ga_ref, gb_ref, wua_ref, wuh_ref, wo_ref,
               hg_ref, fg_ref, rw_ref, rb_ref, ltri_ref,
               xn_ref, h2_ref, rt_ref, cnt_ref):
    part = ltri_ref.shape[0]
    n_parts = x_ref.shape[0] // part
    parts = [slice(k * part, (k + 1) * part) for k in range(n_parts)]
    logits = [_post_mix(rows, x_ref, ya_ref, of_ref, or_ref, og_ref, ga_ref, gb_ref, wua_ref, wuh_ref,
                        wo_ref, hg_ref, fg_ref, rw_ref, rb_ref, xn_ref, h2_ref) for rows in parts]
    routed = [_post_route(lg) for lg in logits]
    red = dict(axis=1, keepdims=True)
    lane = lax.broadcasted_iota(I32, (part, LANES), 1)
    for k, (rows, ((e1, e2, w1, w2), is1, is2, onehot)) in enumerate(zip(parts, routed)):
        if k % POST_PARTS == 0:
            before = jnp.zeros((1, LANES), F32)
        pref = jnp.dot(ltri_ref[...], onehot.astype(BF16), preferred_element_type=F32) + before
        before = before + jnp.sum(onehot, axis=0, keepdims=True)
        rank1 = jnp.sum(jnp.where(is1, pref, 0.0), **red)
        rank2 = jnp.sum(jnp.where(is2, pref, 0.0), **red)
        rt = jnp.zeros((part, LANES), F32)
        for idx, val in enumerate((e1, e2, rank1, rank2, w1, w2)):
            rt = jnp.where(lane == idx, val, rt)
        rt_ref[rows] = rt
        if k % POST_PARTS == POST_PARTS - 1:
            tile = k // POST_PARTS
            cnt_ref[tile * SUBLANES:(tile + 1) * SUBLANES] = jnp.broadcast_to(before, (SUBLANES, LANES))


RT_E1, RT_E2, RT_RANK1, RT_RANK2, RT_W1, RT_W2 = range(6)
SUBLANES = 8


def _post(x, ya, of, orv, og, ga, gb, wua, wuh, wo, hg, fg, rw, rb, ltri):
    T, D = x.shape
    tile = ltri.shape[0] * POST_PARTS
    tm = min(POST_TILES_PER_STEP * tile, T)
    row = lambda w: pl.BlockSpec((tm, w), lambda i: (i, 0))
    full = lambda a: pl.BlockSpec(a.shape, lambda i: (0,) * a.ndim, pipeline_mode=pl.Buffered(1))
    return pl.pallas_call(
        _post_body,
        grid=(T // tm,),
        in_specs=[row(D), row(ATTN_Q_W), row(HGRN_W), row(HGRN_W), row(HGRN_W), row(D), row(D),
                  full(wua), full(wuh), full(wo), full(hg), full(fg), full(rw), full(rb), full(ltri)],
        out_specs=[row(D), row(D), row(LANES),
                   pl.BlockSpec((tm // tile * SUBLANES, LANES), lambda i: (i, 0))],
        out_shape=[jax.ShapeDtypeStruct((T, D), F32), jax.ShapeDtypeStruct((T, D), BF16),
                   jax.ShapeDtypeStruct((T, LANES), F32),
                   jax.ShapeDtypeStruct((T // tile * SUBLANES, LANES), F32)],
        compiler_params=pltpu.CompilerParams(
            dimension_semantics=("arbitrary",), vmem_limit_bytes=VMEM_LIMIT),
        name="post_mixer_router",
    )(x, ya, of, orv, og, ga, gb, wua, wuh, wo, hg, fg, rw, rb, ltri)


XS_W = D_MODEL // 2 + LANES
SLOT_W1, SLOT_W2, SLOT_E1, SLOT_E2 = 0, 2, 4, 5
SEG_BIG = 32
META_DST, META_LOFF, META_NBIG, META_NSMALL, META_TOTAL8 = (k * N_EXPERTS for k in range(5))
META_W = 5 * N_EXPERTS


def _local_rows(tm):
    return 2 * tm + N_EXPERTS * SUBLANES


def _local_pos(rt, loff_row):
    lanef = lax.broadcasted_iota(I32, rt.shape, 1).astype(F32)
    out = []
    for e_lane, r_lane in ((RT_E1, RT_RANK1), (RT_E2, RT_RANK2)):
        e = rt[:, e_lane:e_lane + 1]
        off = jnp.sum(jnp.where(lanef == e, loff_row, 0.0), axis=1, keepdims=True)
        out.append(off + rt[:, r_lane:r_lane + 1])
    return out


def _one_hot(hit):
    return jnp.where(hit, 1.0, 0.0).astype(BF16)


def _segment_starts(meta_ref, make_copy):
    for e in range(N_EXPERTS):
        dst = meta_ref[0, 0, META_DST + e]
        loff = meta_ref[0, 0, META_LOFF + e]
        n_big = meta_ref[0, 0, META_NBIG + e]
        n_small = meta_ref[0, 0, META_NSMALL + e]

        def big(k, c):
            o = k * SEG_BIG
            make_copy(pl.multiple_of(loff + o, SUBLANES), pl.multiple_of(dst + o, SUBLANES), SEG_BIG).start(priority=0)
            return c

        lax.fori_loop(0, n_big, big, 0)

        def small(k, c):
            o = n_big * SEG_BIG + k * SUBLANES
            make_copy(pl.multiple_of(loff + o, SUBLANES), pl.multiple_of(dst + o, SUBLANES), SUBLANES).start(priority=1)
            return c

        lax.fori_loop(0, n_small, small, 0)


def _segment_wait(meta_ref, make_copy, rows):
    total8 = meta_ref[0, 0, META_TOTAL8]
    for bit in range((rows // SUBLANES).bit_length()):
        @pl.when((total8 & (1 << bit)) != 0)
        def _():
            make_copy(0, 0, SUBLANES << bit).wait()


TAIL_PIECES = tuple(SUBLANES << b for b in range((MOE_BLK // SUBLANES).bit_length() - 1))


def _tail_copies(tail_ref, make_copy, action):
    def per_expert(e, c):
        start = tail_ref[0, 0, e]
        n8 = tail_ref[0, 0, N_EXPERTS + e]
        for rows in TAIL_PIECES:
            bit = rows // SUBLANES

            @pl.when((n8 & bit) != 0)
            def _():
                off = (n8 & ~(2 * bit - 1)) * SUBLANES
                action(make_copy(0, pl.multiple_of(start + off, SUBLANES), rows))
        return c

    lax.fori_loop(0, N_EXPERTS, per_expert, 0)

    def per_block(b, c):
        action(make_copy(0, pl.multiple_of(b * MOE_BLK, MOE_BLK), MOE_BLK))
        return c

    lax.fori_loop(tail_ref[0, 0, 2 * N_EXPERTS], tail_ref[0, 0, 2 * N_EXPERTS + 1], per_block, 0)


def _dispatch_body(meta_ref, prev_meta_ref, tail_ref, loff_ref, rt_ref, h_ref, xs_ref, buf_ref, sem):
    i = pl.program_id(0)
    last = pl.num_programs(0) - 1
    slot = i % 2
    tm = rt_ref.shape[0]
    rows = buf_ref.shape[1]
    rt = rt_ref[...]
    dot = functools.partial(jnp.dot, preferred_element_type=F32)
    lane = lax.broadcasted_iota(I32, (tm, LANES), 1)
    l1, l2 = _local_pos(rt, loff_ref[0])
    parts = []
    for pos in (l1, l2):
        hi = jnp.floor(pos * (1.0 / 32.0))
        parts += [hi, pos - 32.0 * hi]
    cols = jnp.zeros((tm, LANES), F32)
    for k, part in enumerate(parts):
        cols = jnp.where(lane == k, part, cols)
    eye = _one_hot(lax.broadcasted_iota(I32, (SUBLANES, LANES), 0) == lax.broadcasted_iota(I32, (SUBLANES, LANES), 1))
    lanes = _nt(eye, cols.astype(BF16))
    r1 = (32.0 * lanes[0:1] + lanes[1:2]).astype(I32)
    r2 = (32.0 * lanes[2:3] + lanes[3:4]).astype(I32)
    row = lax.broadcasted_iota(I32, (rows, tm), 0)
    perm = _one_hot((row == r1) | (row == r2))
    side = jnp.zeros((tm, LANES), F32)
    for w_lane, e_lane, base in ((RT_W1, RT_E1, SLOT_W1), (RT_W2, RT_E2, SLOT_W2)):
        w = rt[:, w_lane:w_lane + 1]
        hi = w.astype(BF16).astype(F32)
        side = jnp.where(lane == base, hi, jnp.where(lane == base + 1, w - hi, side))
        side = jnp.where(lane == SLOT_E1 + (base - SLOT_W1) // 2, rt[:, e_lane:e_lane + 1], side)
    xl = dot(perm, jnp.concatenate([h_ref[...], side.astype(BF16)], axis=1))
    half = D_MODEL // 2
    buf_ref[slot, :, :half] = _pack_pair(xl[:, :half], xl[:, half:D_MODEL])
    buf_ref[slot, :, half:] = lax.bitcast_convert_type(xl[:, D_MODEL:], U32)

    def copier(k):
        def make_copy(src_row, dst_row, n):
            return pltpu.make_async_copy(buf_ref.at[k, pl.ds(src_row, n)], xs_ref.at[pl.ds(dst_row, n)],
                                         sem.at[k])
        return make_copy

    _segment_starts(meta_ref, copier(slot))

    @pl.when(i > 0)
    def _():
        _segment_wait(prev_meta_ref, copier(1 - slot), rows)

    @pl.when(i == last)
    def _():
        _segment_wait(meta_ref, copier(slot), rows)
        buf_ref[slot, :MOE_BLK] = jnp.zeros((MOE_BLK, XS_W), U32)
        _tail_copies(tail_ref, copier(slot), lambda cp: cp.start())
        _tail_copies(tail_ref, copier(slot), lambda cp: cp.wait())


def _dispatch(meta, tail, loff, rt, h2, n_pad):
    T, D = h2.shape
    nt = meta.shape[0]
    tm = T // nt
    return pl.pallas_call(
        _dispatch_body,
        grid=(nt,),
        in_specs=[pl.BlockSpec((1, 1, meta.shape[2]), lambda i: (i, 0, 0), memory_space=pltpu.SMEM),
                  pl.BlockSpec((1, 1, meta.shape[2]), lambda i: (jnp.maximum(i - 1, 0), 0, 0),
                               memory_space=pltpu.SMEM),
                  pl.BlockSpec((1, 1, tail.shape[2]), lambda i: (0, 0, 0), memory_space=pltpu.SMEM),
                  pl.BlockSpec((1, 1, LANES), lambda i: (i, 0, 0)),
                  pl.BlockSpec((tm, LANES), lambda i: (i, 0)),
                  pl.BlockSpec((tm, D), lambda i: (i, 0))],
        out_specs=pl.BlockSpec(memory_space=pl.ANY),
        out_shape=jax.ShapeDtypeStruct((n_pad, XS_W), U32),
        scratch_shapes=[pltpu.VMEM((2, _local_rows(tm), XS_W), U32), pltpu.SemaphoreType.DMA((2,))],
        compiler_params=pltpu.CompilerParams(
            dimension_semantics=("arbitrary",), vmem_limit_bytes=VMEM_LIMIT),
        name="moe_dispatch",
    )(meta, meta, tail, loff, rt, h2)


EXPERT_PARTS = 2


def _expert_body(be_ref, nused_ref, xs_ref, wg_ref, wu_ref, wd_ref, ys_ref, wgb_ref, wub_ref, wdb_ref):
    i = pl.program_id(0)

    @pl.when((i == 0) | (be_ref[i] != be_ref[jnp.maximum(i - 1, 0)]))
    def _():
        wgb_ref[...] = wg_ref[0].astype(BF16)
        wub_ref[...] = wu_ref[0].astype(BF16)
        wdb_ref[...] = wd_ref[0].astype(BF16)

    live = i < nused_ref[0]

    @pl.when(jnp.logical_not(live))
    def _():
        ys_ref[...] = jnp.zeros_like(ys_ref)

    @pl.when(live)
    def _():
        dot = functools.partial(jnp.dot, preferred_element_type=F32)
        part = MOE_BLK // EXPERT_PARTS
        half = D_MODEL // 2
        parts = [slice(k * part, (k + 1) * part) for k in range(EXPERT_PARTS)]
        hs = []
        for rows in parts:
            ha, hb = _unpack_pair(xs_ref[rows, :half])
            hs.append(jnp.concatenate([ha, hb], axis=1).astype(BF16))
        gs = [dot(h, wgb_ref[...]) for h in hs]
        us = [dot(h, wub_ref[...]) for h in hs]
        acts = [(g * _sigmoid(g) * u).astype(BF16) for g, u in zip(gs, us)]
        expert = (be_ref[i] % N_EXPERTS).astype(F32)
        for rows, act in zip(parts, acts):
            rec = lax.bitcast_convert_type(xs_ref[rows, half:], F32)
            w = jnp.where(rec[:, SLOT_E1:SLOT_E1 + 1] == expert,
                          rec[:, SLOT_W1:SLOT_W1 + 1] + rec[:, SLOT_W1 + 1:SLOT_W1 + 2],
                          rec[:, SLOT_W2:SLOT_W2 + 1] + rec[:, SLOT_W2 + 1:SLOT_W2 + 2])
            y = dot(act, wdb_ref[...]) * w
            ys_ref[rows] = _pack_pair(y[:, :half], y[:, half:])


def _experts(block_expert, nused, xs, wg, wu, wd):
    n_pad = xs.shape[0]
    blk_in = pl.BlockSpec((MOE_BLK, XS_W), lambda i, be, nu: (jnp.minimum(i, nu[0] - 1), 0))
    blk_out = pl.BlockSpec((MOE_BLK, D_MODEL // 2), lambda i, be, nu: (i, 0))
    wspec = lambda a: pl.BlockSpec((1,) + a.shape[1:], lambda i, be, nu: (be[i], 0, 0))
    return pl.pallas_call(
        _expert_body,
        grid_spec=pltpu.PrefetchScalarGridSpec(
            num_scalar_prefetch=2,
            grid=(n_pad // MOE_BLK,),
            in_specs=[blk_in, wspec(wg), wspec(wu), wspec(wd)],
            out_specs=blk_out,
            scratch_shapes=[pltpu.VMEM(wg.shape[1:], BF16), pltpu.VMEM(wu.shape[1:], BF16),
                            pltpu.VMEM(wd.shape[1:], BF16)],
        ),
        out_shape=jax.ShapeDtypeStruct((n_pad, D_MODEL // 2), U32),
        compiler_params=pltpu.CompilerParams(
            dimension_semantics=("arbitrary",), vmem_limit_bytes=VMEM_LIMIT),
        name="moe_experts",
    )(block_expert, nused, xs, wg, wu, wd)


def _combine_body(meta_ref, next_meta_ref, loff_ref, rt_ref, x_ref, fg_ref, ys_ref, o_ref, buf_ref, sem,
                  *, final_norm):
    i = pl.program_id(0)
    slot = i % 2

    def copier(k):
        def make_copy(loc_row, glob_row, n):
            return pltpu.make_async_copy(ys_ref.at[pl.ds(glob_row, n)], buf_ref.at[k, pl.ds(loc_row, n)],
                                         sem.at[k])
        return make_copy

    rows = buf_ref.shape[1]

    @pl.when(i == 0)
    def _():
        buf_ref[...] = jnp.zeros_like(buf_ref)
        _segment_starts(meta_ref, copier(0))

    @pl.when(i + 1 < pl.num_programs(0))
    def _():
        _segment_starts(next_meta_ref, copier(1 - slot))

    l1, l2 = _local_pos(rt_ref[...], loff_ref[0])
    col = lax.broadcasted_iota(I32, (rt_ref.shape[0], rows), 1)
    pt = _one_hot((col == l1.astype(I32)) | (col == l2.astype(I32)))
    _segment_wait(meta_ref, copier(slot), rows)
    yl = jnp.concatenate(_unpack_pair(buf_ref[slot]), axis=1).astype(BF16)
    out = x_ref[...] + jnp.dot(pt, yl, preferred_element_type=F32)
    if final_norm:
        out = _rms(out) * fg_ref[...]
    o_ref[...] = out


def _combine(meta, loff, rt, x, fg, ys, final_norm):
    T, D = x.shape
    nt = meta.shape[0]
    tm = T // nt
    return pl.pallas_call(
        functools.partial(_combine_body, final_norm=final_norm),
        grid=(nt,),
        in_specs=[pl.BlockSpec((1, 1, meta.shape[2]), lambda i: (i, 0, 0), memory_space=pltpu.SMEM),
                  pl.BlockSpec((1, 1, meta.shape[2]), lambda i: (jnp.minimum(i + 1, nt - 1), 0, 0),
                               memory_space=pltpu.SMEM),
                  pl.BlockSpec((1, 1, LANES), lambda i: (i, 0, 0)),
                  pl.BlockSpec((tm, LANES), lambda i: (i, 0)),
                  pl.BlockSpec((tm, D), lambda i: (i, 0)),
                  pl.BlockSpec((1, D), lambda i: (0, 0)),
                  pl.BlockSpec(memory_space=pl.ANY)],
        out_specs=pl.BlockSpec((tm, D), lambda i: (i, 0)),
        out_shape=jax.ShapeDtypeStruct((T, D), F32),
        scratch_shapes=[pltpu.VMEM((2, _local_rows(tm), D // 2), U32), pltpu.SemaphoreType.DMA((2,))],
        compiler_params=pltpu.CompilerParams(
            dimension_semantics=("arbitrary",), vmem_limit_bytes=VMEM_LIMIT),
        name="moe_combine",
    )(meta, meta, loff, rt, x, fg, ys)


def _rope_tables(L):
    half = HEAD_DIM // 2
    inv = ROPE_THETA ** (-jnp.arange(half, dtype=F32) / half)
    ang = jnp.arange(L, dtype=F32)[:, None] * inv[None, :]
    lane = np.arange(LANES)
    idx = lane % half
    first = jnp.asarray((lane % HEAD_DIM) < half)
    cos = jnp.cos(ang)[:, idx]
    sin = jnp.sin(ang)[:, idx]
    return cos, jnp.where(first, -sin, 0.0), jnp.where(first, 0.0, sin)


def _moe_layout(cnt, T, layer):
    nt = cnt.shape[0] // SUBLANES
    c = cnt.reshape(nt, SUBLANES, LANES)[:, 0, :N_EXPERTS].astype(I32)
    c8 = (c + SUBLANES - 1) // SUBLANES * SUBLANES
    tile_off = jnp.cumsum(c8, axis=0) - c8
    total = jnp.sum(c8, axis=0)
    padded = (total + MOE_BLK - 1) // MOE_BLK * MOE_BLK
    pend = jnp.cumsum(padded)
    dst = (pend - padded)[None, :] + tile_off
    loff = jnp.cumsum(c8, axis=1) - c8
    n8 = c8 // SUBLANES
    per_big = SEG_BIG // SUBLANES
    total8 = jnp.broadcast_to(jnp.sum(n8, axis=1, keepdims=True), n8.shape)
    meta = jnp.concatenate([dst, loff, n8 // per_big, n8 % per_big, total8], axis=1).reshape(nt, 1, META_W)
    loff_f = jnp.pad(loff.astype(F32), ((0, 0), (0, LANES - N_EXPERTS))).reshape(nt, 1, LANES)
    n_pad = 2 * T + nt * N_EXPERTS * SUBLANES + N_EXPERTS * MOE_BLK
    n_pad = (n_pad + MOE_BLK - 1) // MOE_BLK * MOE_BLK
    starts = jnp.arange(n_pad // MOE_BLK, dtype=I32) * MOE_BLK
    block_expert = jnp.minimum(jnp.sum(starts[:, None] >= pend[None, :], axis=1), N_EXPERTS - 1)
    block_expert = block_expert.astype(I32) + layer * N_EXPERTS
    nused = (pend[-1:] // MOE_BLK).astype(I32)
    tail = jnp.concatenate([pend - padded + total, (padded - total) // SUBLANES, nused,
                            jnp.full((1,), n_pad // MOE_BLK, I32)]).reshape(1, 1, 2 * N_EXPERTS + 2)
    return meta, tail, loff_f, block_expert, nused, n_pad


def kernel(x, attn_norm_g, w_in, sink_logits, hgrn_lb_table, hgrn_norm_g, w_up_attn, w_up_hgrn, w_out,
           ffn_norm_g, w_router_group, b_router_group, w_router_expert, b_router_expert,
           w_expert_gate, w_expert_up, w_expert_down, final_norm_g):
    B, L, D = x.shape
    T = B * L
    depth = w_in.shape[0]
    sm = jax.nn.softmax(hgrn_lb_table.astype(F32), axis=0)
    lower_bounds = jnp.cumsum(sm, axis=0) - sm[:1]
    cos, s1, s2 = _rope_tables(L)
    tm_post = min(TM_POST, T)
    ltri = jnp.tril(jnp.ones((tm_post // POST_PARTS,) * 2, BF16), -1)
    fin_g = final_norm_g.reshape(1, D).astype(F32)
    w_in_bf16 = w_in.astype(BF16)
    all_layers = lambda w: w.reshape((depth * N_EXPERTS,) + w.shape[2:])
    wg_all, wu_all, wd_all = all_layers(w_expert_gate), all_layers(w_expert_up), all_layers(w_expert_down)

    for l in range(depth):
        qa, ka, va, qb, sf, sb, ib, og, ga, gb = _inproj(
            x, attn_norm_g[l].reshape(1, D), w_in_bf16, l, cos, s1, s2)
        ya = _attention(qa, ka, va, sink_logits[l].astype(F32) * LOG2E)
        of, orv = _hgrn(qb, sf, sb, ib, lower_bounds[l])

        rw = jnp.concatenate(
            [w_router_group[l], jnp.transpose(w_router_expert[l], (1, 0, 2)).reshape(D, N_EXPERTS)], axis=1)
        rw = jnp.pad(rw.astype(F32), ((0, 0), (0, LANES - rw.shape[1])))
        rwh = rw.astype(BF16)
        rw2 = jnp.concatenate([rwh, (rw - rwh.astype(F32)).astype(BF16)], axis=1)
        rb = jnp.concatenate([b_router_group[l], b_router_expert[l].reshape(-1)]).astype(F32)
        rb = jnp.pad(rb, (0, LANES - rb.shape[0])).reshape(1, LANES)
        flat = lambda a: a.reshape(T, a.shape[-1])
        xn, h2, rt, cnt = _post(
            flat(x), flat(ya), flat(of), flat(orv), flat(og), flat(ga), flat(gb),
            w_up_attn[l].astype(BF16), w_up_hgrn[l].astype(BF16), w_out[l].astype(BF16),
            hgrn_norm_g[l].reshape(1, HGRN_W).astype(F32), ffn_norm_g[l].reshape(1, D).astype(F32),
            rw2, rb, ltri)

        meta, tail, loff, block_expert, nused, n_pad = _moe_layout(cnt, T, l)
        xs = _dispatch(meta, tail, loff, rt, h2, n_pad)
        ys = _experts(block_expert, nused, xs, wg_all, wu_all, wd_all)
        x = _combine(meta, loff, rt, xn, fin_g, ys, l == depth - 1).reshape(B, L, D)
    return x
```
